```python
import math
import jax, jax.numpy as jnp
from jax import lax
import numpy as np

D_MODEL = 1024
BATCH = 8
SEQ = 4096
DEPTH = 1
DEC_BATCH = 2
DEC_SEQ = 16384
PAST_LEN = 128

CHUNK = 128
GM_WIDTH = 1024
GM_GROUPS = 8
GM_GROUP_DIM = GM_WIDTH // GM_GROUPS
N_HEADS = 8
QK_NOPE = 128
QK_ROPE = 64
V_DIM = 128
Q_LORA = 384
KV_LORA = 256
ROPE_BASE = 10000.0
Q_BLOCK = 128
N_EXPERTS = 16
EC_CAPACITY_FACTOR = 2
D_FF_EXPERT = 2048
LN_EPS = 1e-5
RMS_EPS = 1e-6
ALPHA = (2.0 * DEPTH) ** 0.25
BETA = (8.0 * DEPTH) ** -0.25
OFF_V = GM_WIDTH
OFF_CQ = 2 * GM_WIDTH
OFF_CKV = OFF_CQ + Q_LORA
OFF_KR = OFF_CKV + KV_LORA
OFF_GA = OFF_KR + QK_ROPE
OFF_GB = OFF_GA + D_MODEL
IN_COLS = OFF_GB + D_MODEL

kernel_name = "hybrid_gmlp_mla_ec_encoder"


def _layernorm(x):
    x32 = x.astype(jnp.float32)
    mu = jnp.mean(x32, axis=-1, keepdims=True)
    var = jnp.mean(jnp.square(x32 - mu), axis=-1, keepdims=True)
    return ((x32 - mu) * lax.rsqrt(var + LN_EPS)).astype(x.dtype)


def _rmsnorm(x, g):
    x32 = x.astype(jnp.float32)
    y = x32 * lax.rsqrt(jnp.mean(x32 * x32, axis=-1, keepdims=True) + RMS_EPS)
    return y.astype(x.dtype) * g


def _rope(x, seq_len):
    half = QK_ROPE // 2
    inv = ROPE_BASE ** (-jnp.arange(half, dtype=jnp.float32) / half)
    ang = jnp.arange(seq_len, dtype=jnp.float32)[:, None] * inv[None, :]
    cos = jnp.cos(ang)[None, :, None, :].astype(x.dtype)
    sin = jnp.sin(ang)[None, :, None, :].astype(x.dtype)
    x1, x2 = x[..., :half], x[..., half:]
    return jnp.concatenate([x1 * cos - x2 * sin, x2 * cos + x1 * sin], axis=-1)


def _spatial_gating(u, v, ln_v_g, ln_v_b, w_s, b_s):
    B, S, _ = v.shape
    v = _layernorm(v) * ln_v_g + ln_v_b
    vc = v.reshape(B, S // CHUNK, CHUNK, GM_GROUPS, GM_GROUP_DIM)
    mixed = jnp.einsum('gpq,bnqgc->bnpgc', w_s, vc) + b_s.T[:, :, None]
    return u * mixed.reshape(B, S, GM_WIDTH)


def _mla(c_q, c_kv, k_r, q_norm_g, w_uq, kv_norm_g, w_ukv):
    B, S, _ = c_q.shape
    nb = S // Q_BLOCK
    q = (_rmsnorm(c_q, q_norm_g) @ w_uq).reshape(B, S, N_HEADS, QK_NOPE + QK_ROPE)
    q_nope = q[..., :QK_NOPE]
    q_rope = _rope(q[..., QK_NOPE:], S)
    kv = (_rmsnorm(c_kv, kv_norm_g) @ w_ukv).reshape(B, S, N_HEADS, QK_NOPE + V_DIM)
    k_nope, v = kv[..., :QK_NOPE], kv[..., QK_NOPE:]
    k_rope = _rope(k_r[:, :, None, :], S)[:, :, 0, :]
    scale = (QK_NOPE + QK_ROPE) ** -0.5
    qn_blocks = q_nope.reshape(B, nb, Q_BLOCK, N_HEADS, QK_NOPE).transpose(1, 0, 2, 3, 4)
    qr_blocks = q_rope.reshape(B, nb, Q_BLOCK, N_HEADS, QK_ROPE).transpose(1, 0, 2, 3, 4)

    def attend(blk):
        qn, qr = blk
        s = jnp.einsum('bqhd,bkhd->bhqk', qn, k_nope) + jnp.einsum('bqhr,bkr->bhqk', qr, k_rope)
        p = jax.nn.softmax(s.astype(jnp.float32) * scale, axis=-1).astype(v.dtype)
        return jnp.einsum('bhqk,bkhd->bqhd', p, v)

    o = lax.map(attend, (qn_blocks, qr_blocks))
    return o.transpose(1, 0, 2, 3, 4).reshape(B, S, N_HEADS * V_DIM)


def _expert_choice(h, w_router, w1, w3, w2):
    B, S, D = h.shape
    T = B * S
    C = EC_CAPACITY_FACTOR * T // N_EXPERTS
    ht = h.reshape(T, D)
    aff = jax.nn.softmax((ht @ w_router).astype(jnp.float32), axis=-1)
    gate, idx = lax.top_k(aff.T, C)
    xe = ht[idx]
    hid = jax.nn.silu(jnp.einsum('ecd,edf->ecf', xe, w1)) * jnp.einsum('ecd,edf->ecf', xe, w3)
    ye = jnp.einsum('ecf,efd->ecd', hid, w2) * gate[..., None].astype(h.dtype)
    out = jnp.zeros_like(ht).at[idx.reshape(-1)].add(ye.reshape(E_C_flat(C), D))
    return out.reshape(B, S, D)


def E_C_flat(C):
    return N_EXPERTS * C


def _trunk(x, c, weights):
    (w_ada, b_ada, w_in, b_in, ln_v_g, ln_v_b, w_spatial, b_spatial, w_oa,
     q_norm_g, w_uq, kv_norm_g, w_ukv, w_ob, w_out, ln1_g, ln1_b,
     w_router, w1, w3, w2, ln2_g, ln2_b) = weights
    for l in range(DEPTH):
        mod = (jax.nn.silu(c) @ w_ada[l] + b_ada[l])[:, None, :]
        sh1, sc1, g1, sh2, sc2, g2 = jnp.split(mod, 6, axis=-1)
        h = _layernorm(x) * (1.0 + sc1) + sh1
        z = h @ w_in[l] + b_in[l]
        z_u, z_v, z_cq, z_ckv, z_kr, z_ga, z_gb = jnp.split(
            z, [OFF_V, OFF_CQ, OFF_CKV, OFF_KR, OFF_GA, OFF_GB], axis=-1)
        ya = _spatial_gating(jax.nn.gelu(z_u, approximate=False), jax.nn.gelu(z_v, approximate=False),
                             ln_v_g[l], ln_v_b[l], w_spatial[l], b_spatial[l]) @ w_oa[l]
        yb = _mla(z_cq, z_ckv, z_kr, q_norm_g[l], w_uq[l], kv_norm_g[l], w_ukv[l]) @ w_ob[l]
        mix = (jax.nn.sigmoid(z_ga) * ya + jax.nn.sigmoid(z_gb) * yb) @ w_out[l]
        x = _layernorm(ALPHA * x + g1 * mix) * ln1_g[l] + ln1_b[l]
        h2 = _layernorm(x) * (1.0 + sc2) + sh2
        moe = _expert_choice(h2, w_router[l], w1[l], w3[l], w2[l])
        x = _layernorm(ALPHA * x + g2 * moe) * ln2_g[l] + ln2_b[l]
    return x


def setup_inputs(seed: int = 0) -> dict:
    key = jax.random.key(seed)
    ks = jax.random.split(key, 28)
    L, D = DEPTH, D_MODEL

    def nrm(k, shape, s):
        return jax.random.normal(k, shape, jnp.float32) * s

    return {
        "x_prompt": nrm(ks[0], (BATCH, SEQ, D), 1.0),
        "x_sample": nrm(ks[1], (DEC_BATCH, DEC_SEQ, D), 1.0),
        "c_prompt": nrm(ks[2], (BATCH, D), 1.0),
        "c_sample": nrm(ks[3], (DEC_BATCH, D), 1.0),
        "w_ada": nrm(ks[4], (L, D, 6 * D), 0.5 * D ** -0.5),
        "b_ada": nrm(ks[5], (L, 6 * D), 0.02),
        "w_in": nrm(ks[6], (L, D, IN_COLS), D ** -0.5),
        "b_in": nrm(ks[7], (L, IN_COLS), 0.02),
        "ln_v_g": 1.0 + nrm(ks[8], (L, GM_WIDTH), 0.02),
        "ln_v_b": nrm(ks[9], (L, GM_WIDTH), 0.02),
        "w_spatial": nrm(ks[10], (L, GM_GROUPS, CHUNK, CHUNK), CHUNK ** -0.5),
        "b_spatial": 1.0 + nrm(ks[11], (L, GM_GROUPS, CHUNK), 0.02),
        "w_oa": nrm(ks[12], (L, GM_WIDTH, D), BETA * GM_WIDTH ** -0.5),
        "q_norm_g": 1.0 + nrm(ks[13], (L, Q_LORA), 0.02),
        "w_uq": nrm(ks[14], (L, Q_LORA, N_HEADS * (QK_NOPE + QK_ROPE)), Q_LORA ** -0.5),
        "kv_norm_g": 1.0 + nrm(ks[15], (L, KV_LORA), 0.02),
        "w_ukv": nrm(ks[16], (L, KV_LORA, N_HEADS * (QK_NOPE + V_DIM)), KV_LORA ** -0.5),
        "w_ob": nrm(ks[17], (L, N_HEADS * V_DIM, D), BETA * (N_HEADS * V_DIM) ** -0.5),
        "w_out": nrm(ks[18], (L, D, D), BETA * D ** -0.5),
        "ln1_g": 1.0 + nrm(ks[19], (L, D), 0.02),
        "ln1_b": nrm(ks[20], (L, D), 0.02),
        "w_router": nrm(ks[21], (L, D, N_EXPERTS), D ** -0.5),
        "w1": nrm(ks[22], (L, N_EXPERTS, D, D_FF_EXPERT), D ** -0.5),
        "w3": nrm(ks[23], (L, N_EXPERTS, D, D_FF_EXPERT), D ** -0.5),
        "w2": nrm(ks[24], (L, N_EXPERTS, D_FF_EXPERT, D), BETA * D_FF_EXPERT ** -0.5),
        "ln2_g": 1.0 + nrm(ks[25], (L, D), 0.02),
        "ln2_b": nrm(ks[26], (L, D), 0.02),
    }


def reference(x_prompt, x_sample, c_prompt, c_sample, w_ada, b_ada, w_in, b_in, ln_v_g, ln_v_b,
              w_spatial, b_spatial, w_oa, q_norm_g, w_uq, kv_norm_g, w_ukv, w_ob, w_out,
              ln1_g, ln1_b, w_router, w1, w3, w2, ln2_g, ln2_b):
    weights = (w_ada, b_ada, w_in, b_in, ln_v_g, ln_v_b, w_spatial, b_spatial, w_oa,
               q_norm_g, w_uq, kv_norm_g, w_ukv, w_ob, w_out, ln1_g, ln1_b,
               w_router, w1, w3, w2, ln2_g, ln2_b)
    y_prompt = _trunk(x_prompt, c_prompt, weights)
    y_sample = _trunk(x_sample, c_sample, weights)
    return (y_prompt, y_sample)
```

```python
import functools
import math

import jax
import jax.numpy as jnp
from jax import lax
from jax.experimental import pallas as pl
from jax.experimental.pallas import tpu as pltpu

F32 = jnp.float32
BF16 = jnp.bfloat16
I32 = jnp.int32

D_MODEL = 1024
GM_WIDTH = 1024
GM_GROUPS = 8
CHUNK = 128
N_HEADS = 8
QK_NOPE = 128
QK_ROPE = 64
V_DIM = 128
Q_LORA = 384
KV_LORA = 256
ROPE_BASE = 10000.0
N_EXPERTS = 16
EC_CAPACITY_FACTOR = 2
D_FF = 2048
LN_EPS = 1e-5
RMS_EPS = 1e-6

LANES = 128
SUBLANES = 8
HEAD_W = 256
ROW_TILES = D_MODEL // LANES

C_U, C_V, C_CQ, C_CKV, C_KR, C_GA, C_GB, C_END = 0, 1024, 2048, 2432, 2688, 2816, 3840, 4864

TM_IN = 512
TQ = 512
TM = 256
SLOT_TILE = 256
FF_TILE = 512
M_CHUNK = 512
VMEM_LIMIT = 56 * 1024 * 1024


def _cparams(sem):
    return pltpu.CompilerParams(dimension_semantics=sem, vmem_limit_bytes=VMEM_LIMIT)


def _ln(x):
    mu = jnp.mean(x, axis=-1, keepdims=True)
    xc = x - mu
    var = jnp.mean(xc * xc, axis=-1, keepdims=True)
    return xc * lax.rsqrt(var + LN_EPS)


def _rms(x):
    return x * lax.rsqrt(jnp.mean(x * x, axis=-1, keepdims=True) + RMS_EPS)


def _gelu(x):
    return 0.5 * x * (1.0 + lax.erf(x * (2.0 ** -0.5)))


def _dot(a, b):
    return jnp.dot(a, b, preferred_element_type=F32)


def _dot_nt(a, b):
    return lax.dot_general(a, b, (((1,), (1,)), ((), ())), preferred_element_type=F32)


def _mod_kernel(c_ref, w_ref, b_ref, o_ref):
    c = c_ref[...]
    s = c * jax.nn.sigmoid(c)
    o_ref[...] = jnp.dot(s, w_ref[...], preferred_element_type=F32,
                         precision=lax.Precision.HIGHEST) + b_ref[...]


def _mod(c, w_ada, b_ada):
    bsz = c.shape[0]
    bp = -(-bsz // SUBLANES) * SUBLANES
    cp = jnp.zeros((bp, D_MODEL), F32).at[:bsz].set(c)
    n = w_ada.shape[1]
    tn = 1024
    out = pl.pallas_call(
        _mod_kernel,
        grid=(n // tn,),
        in_specs=[pl.BlockSpec((bp, D_MODEL), lambda j: (0, 0)),
                  pl.BlockSpec((D_MODEL, tn), lambda j: (0, j)),
                  pl.BlockSpec((1, tn), lambda j: (0, j))],
        out_specs=pl.BlockSpec((bp, tn), lambda j: (0, j)),
        out_shape=jax.ShapeDtypeStruct((bp, n), F32),
        compiler_params=_cparams(("arbitrary",)),
        name="mod",
    )(cp, w_ada, b_ada.reshape(1, n))
    return out[:bsz].reshape(bsz, 6, D_MODEL)


def _inproj_kernel(x_ref, mod_ref, w_ref, b_ref, lvg_ref, lvb_ref, qg_ref, wuq_ref, kvg_ref,
                   wuk_ref, wuvt_ref, cos_ref, sin_ref,
                   u_ref, vn_ref, q_ref, k_ref, vt_ref, sga_ref, sgb_ref, *, qscale):
    x = x_ref[...]
    sh1 = mod_ref[0, 0:1, :]
    sc1 = mod_ref[0, 1:2, :]
    h = (_ln(x) * (1.0 + sc1) + sh1).astype(BF16)

    def proj(a, b):
        return _dot(h, w_ref[:, a:b]) + b_ref[:, a:b]

    cos = cos_ref[...]
    sin = sin_ref[...]

    def rope(t):
        return t * cos + pltpu.roll(t, 64, 1) * sin

    u_ref[...] = _gelu(proj(C_U, C_V)).astype(BF16)
    v = _gelu(proj(C_V, C_CQ))
    vn_ref[...] = (_ln(v) * lvg_ref[...] + lvb_ref[...]).astype(BF16)

    cqn = (_rms(proj(C_CQ, C_CKV)) * qg_ref[...]).astype(BF16)
    q = _dot(cqn, wuq_ref[...]) * qscale
    for hh in range(N_HEADS):
        o = hh * HEAD_W
        q_ref[0, hh, :, 0:QK_NOPE] = q[:, o:o + QK_NOPE].astype(BF16)
        q_ref[0, hh, :, QK_NOPE:HEAD_W] = rope(q[:, o + QK_NOPE:o + HEAD_W]).astype(BF16)

    ckvn = (_rms(proj(C_CKV, C_KR)) * kvg_ref[...]).astype(BF16)
    kn = _dot(ckvn, wuk_ref[...])
    kr = rope(proj(C_KR, C_GA)).astype(BF16)
    for hh in range(N_HEADS):
        k_ref[0, hh, :, 0:QK_NOPE] = kn[:, hh * QK_NOPE:(hh + 1) * QK_NOPE].astype(BF16)
        k_ref[0, hh, :, QK_NOPE:HEAD_W] = kr
    vt = _dot_nt(wuvt_ref[...], ckvn)
    for hh in range(N_HEADS):
        vt_ref[0, hh, 0] = vt[hh * V_DIM:(hh + 1) * V_DIM, :].astype(BF16)

    sga_ref[...] = jax.nn.sigmoid(proj(C_GA, C_GB)).astype(BF16)
    sgb_ref[...] = jax.nn.sigmoid(proj(C_GB, C_END)).astype(BF16)


def _inproj(x2, mod3, wp, bsz, seq):
    t = bsz * seq
    tm = min(TM_IN, seq)
    spb = seq // tm
    full = lambda a: pl.BlockSpec(a.shape, lambda i: (0,) * a.ndim)
    qscale = (QK_NOPE + QK_ROPE) ** -0.5 * math.log2(math.e)
    row = pl.BlockSpec((tm, D_MODEL), lambda i: (i, 0))
    qk_spec = pl.BlockSpec((1, N_HEADS, tm, HEAD_W), lambda i: (i // spb, 0, i % spb, 0))
    ins = [x2, mod3, wp["w_in"], wp["b_in"], wp["ln_v_g"], wp["ln_v_b"], wp["q_norm_g"], wp["w_uq"],
           wp["kv_norm_g"], wp["w_uk"], wp["w_uvt"], wp["cos"], wp["sin"]]
    in_specs = [row, pl.BlockSpec((1, 6, D_MODEL), lambda i: (i // spb, 0, 0))]
    in_specs += [full(a) for a in ins[2:11]]
    in_specs += [pl.BlockSpec((tm, LANES), lambda i: (i % spb, 0))] * 2
    return pl.pallas_call(
        functools.partial(_inproj_kernel, qscale=qscale),
        grid=(t // tm,),
        in_specs=in_specs,
        out_specs=[row, row, qk_spec, qk_spec,
                   pl.BlockSpec((1, N_HEADS, 1, V_DIM, tm), lambda i: (i // spb, 0, i % spb, 0, 0)),
                   row, row],
        out_shape=[jax.ShapeDtypeStruct((t, D_MODEL), BF16),
                   jax.ShapeDtypeStruct((t, D_MODEL), BF16),
                   jax.ShapeDtypeStruct((bsz, N_HEADS, seq, HEAD_W), BF16),
                   jax.ShapeDtypeStruct((bsz, N_HEADS, seq, HEAD_W), BF16),
                   jax.ShapeDtypeStruct((bsz, N_HEADS, spb, V_DIM, tm), BF16),
                   jax.ShapeDtypeStruct((t, D_MODEL), BF16),
                   jax.ShapeDtypeStruct((t, D_MODEL), BF16)],
        compiler_params=_cparams(("arbitrary",)),
        name="inproj",
    )(*ins)


def _spatial_kernel(u_ref, vn_ref, ws_ref, bs_ref, woa_ref, sga_ref, o_ref, g_ref):
    tm = u_ref.shape[0]
    nc = tm // CHUNK
    for g in range(GM_GROUPS):
        c0 = g * CHUNK
        rhs = jnp.concatenate([vn_ref[n * CHUNK:(n + 1) * CHUNK, c0:c0 + CHUNK] for n in range(nc)], axis=1)
        mixed = _dot(ws_ref[g], rhs)
        bias = bs_ref[g]
        for n in range(nc):
            m = mixed[:, n * CHUNK:(n + 1) * CHUNK] + bias
            uu = u_ref[n * CHUNK:(n + 1) * CHUNK, c0:c0 + CHUNK].astype(F32)
            g_ref[n * CHUNK:(n + 1) * CHUNK, c0:c0 + CHUNK] = (uu * m).astype(BF16)
    ya = _dot(g_ref[...], woa_ref[...])
    o_ref[...] = (sga_ref[...].astype(F32) * ya).astype(BF16)


def _spatial(u, vn, sga, wp):
    t = u.shape[0]
    tm = TM
    row = pl.BlockSpec((tm, D_MODEL), lambda i: (i, 0))
    full = lambda a: pl.BlockSpec(a.shape, lambda i: (0,) * a.ndim)
    return pl.pallas_call(
        _spatial_kernel,
        grid=(t // tm,),
        in_specs=[row, row, full(wp["w_s"]), full(wp["b_s"]), full(wp["w_oa"]), row],
        out_specs=row,
        out_shape=jax.ShapeDtypeStruct((t, D_MODEL), BF16),
        scratch_shapes=[pltpu.VMEM((tm, GM_WIDTH), BF16)],
        compiler_params=_cparams(("arbitrary",)),
        name="spatial",
    )(u, vn, wp["w_s"], wp["b_s"], wp["w_oa"], sga)


def _attn_kernel(q_ref, k_ref, vt_ref, o_ref, acc_ref, *, nk, tk):
    q = q_ref[0, 0]
    tq = q.shape[0]
    acc_ref[...] = jnp.zeros_like(acc_ref)

    def body(j, carry):
        m, l = carry
        k = k_ref[0, 0, pl.ds(pl.multiple_of(j * tk, tk), tk), :]
        s = _dot_nt(k, q)
        m_new = jnp.maximum(m, jnp.max(s, axis=0, keepdims=True))
        p = jnp.exp2(s - m_new)
        alpha = jnp.exp2(m - m_new)
        l = alpha * l + jnp.sum(p, axis=0, keepdims=True)
        acc_ref[...] = alpha * acc_ref[...] + _dot(vt_ref[0, 0, j], p.astype(BF16))
        return m_new, l

    m0 = jnp.full((1, tq), -jnp.inf, F32)
    l0 = jnp.zeros((1, tq), F32)
    _, l = lax.fori_loop(0, nk, body, (m0, l0))
    o = acc_ref[...] / l
    o_ref[0] = o.T.astype(BF16)


def _attention(q, k, vt):
    bsz, nh, seq, _ = q.shape
    nk, tk = vt.shape[2], vt.shape[4]
    tq = min(TQ, seq)
    return pl.pallas_call(
        functools.partial(_attn_kernel, nk=nk, tk=tk),
        grid=(bsz, nh, seq // tq),
        in_specs=[pl.BlockSpec((1, 1, tq, HEAD_W), lambda b, h, i: (b, h, i, 0)),
                  pl.BlockSpec((1, 1, seq, HEAD_W), lambda b, h, i: (b, h, 0, 0)),
                  pl.BlockSpec((1, 1, nk, V_DIM, tk), lambda b, h, i: (b, h, 0, 0, 0))],
        out_specs=pl.BlockSpec((1, tq, V_DIM), lambda b, h, i: (b, i, h)),
        out_shape=jax.ShapeDtypeStruct((bsz, seq, nh * V_DIM), BF16),
        scratch_shapes=[pltpu.VMEM((V_DIM, tq), F32)],
        compiler_params=_cparams(("arbitrary", "arbitrary", "arbitrary")),
        name="attn",
    )(q, k, vt)


def _post_kernel(x_ref, mod_ref, o_ref, ap_ref, sgb_ref, wob_ref, wout_ref, g1_ref, b1_ref, wr_ref,
                 x1_ref, h2_ref, aff_ref, *, alpha):
    yb = _dot(o_ref[...], wob_ref[...])
    mixin = (ap_ref[...].astype(F32) + sgb_ref[...].astype(F32) * yb).astype(BF16)
    mix = _dot(mixin, wout_ref[...])
    gate1 = mod_ref[0, 2:3, :]
    sh2 = mod_ref[0, 3:4, :]
    sc2 = mod_ref[0, 4:5, :]
    x1 = _ln(alpha * x_ref[...] + gate1 * mix) * g1_ref[...] + b1_ref[...]
    x1_ref[...] = x1
    h2 = _ln(x1) * (1.0 + sc2) + sh2
    h2_ref[...] = h2
    logits = jnp.dot(h2, wr_ref[...], preferred_element_type=F32,
                     precision=lax.Precision.HIGHEST)
    lane = lax.broadcasted_iota(I32, logits.shape, 1)
    logits = jnp.where(lane < N_EXPERTS, logits, -jnp.inf)
    ex = jnp.exp(logits - jnp.max(logits, axis=-1, keepdims=True))
    aff = ex / jnp.sum(ex, axis=-1, keepdims=True)
    aff_ref[0] = aff.T[0:N_EXPERTS, :]


def _post(x2, mod3, o, ap, sgb, wp, bsz, seq, alpha):
    t = x2.shape[0]
    tm = TM
    spb = seq // tm
    row = pl.BlockSpec((tm, D_MODEL), lambda i: (i, 0))
    full = lambda a: pl.BlockSpec(a.shape, lambda i: (0,) * a.ndim)
    return pl.pallas_call(
        functools.partial(_post_kernel, alpha=alpha),
        grid=(t // tm,),
        in_specs=[row, pl.BlockSpec((1, 6, D_MODEL), lambda i: (i // spb, 0, 0)), row, row, row,
                  full(wp["w_ob"]), full(wp["w_out"]), full(wp["ln1_g"]), full(wp["ln1_b"]),
                  full(wp["w_router"])],
        out_specs=[row, row, pl.BlockSpec((1, N_EXPERTS, tm), lambda i: (i, 0, 0))],
        out_shape=[jax.ShapeDtypeStruct((t, D_MODEL), F32),
                   jax.ShapeDtypeStruct((t, D_MODEL), F32),
                   jax.ShapeDtypeStruct((t // tm, N_EXPERTS, tm), F32)],
        compiler_params=_cparams(("arbitrary",)),
        name="post",
    )(x2, mod3, o, ap, sgb, wp["w_ob"], wp["w_out"], wp["ln1_g"], wp["ln1_b"], wp["w_router"])


def _route_kernel(aff_ref, spos_ref, flat_ref, cum_ref, tokp_ref, cnt_ref, m_ref, p_ref, *, cap, nb):
    bits = pltpu.bitcast(aff_ref[...], I32)
    tm = bits.shape[2]

    def count(mask):
        c = jnp.sum(jnp.where(mask, 1, 0), axis=0, keepdims=True)
        return jnp.sum(c, axis=2, keepdims=True)

    def bisect(i, thr):
        cand = thr | lax.shift_left(jnp.int32(1), 30 - i)
        return jnp.where(count(bits >= cand) >= cap, cand, thr)

    thr = lax.fori_loop(0, 31, bisect, jnp.zeros((1, N_EXPERTS, 1), I32))
    gt = bits > thr
    eq = bits == thr
    need = cap - count(gt)

    r = lax.broadcasted_iota(I32, (tm, tm), 0)
    c = lax.broadcasted_iota(I32, (tm, tm), 1)
    upper = jnp.where(r <= c, 1.0, 0.0).astype(BF16)
    er = lax.broadcasted_iota(I32, (N_EXPERTS, N_EXPERTS), 0)
    ec = lax.broadcasted_iota(I32, (N_EXPERTS, N_EXPERTS), 1)
    below = jnp.where(ec < er, 1.0, 0.0).astype(F32)

    def prefix(write_cum):
        def body(j, carry):
            blk = m_ref[j]
            incl = _dot(blk.astype(BF16), upper)
            p_ref[j] = incl - blk + carry
            if write_cum:
                cum_ref[j] = jnp.broadcast_to(carry, (N_EXPERTS, LANES)).astype(I32)
            return carry + jnp.sum(blk, axis=1, keepdims=True)
        return lax.fori_loop(0, nb, body, jnp.zeros((N_EXPERTS, 1), F32))

    m_ref[...] = jnp.where(eq, 1.0, 0.0)
    prefix(False)
    sel = gt | (eq & (p_ref[...] < need.astype(F32)))
    self32 = jnp.where(sel, 1.0, 0.0)
    m_ref[...] = self32
    total = prefix(True)
    cum_ref[nb] = jnp.broadcast_to(total, (N_EXPERTS, LANES)).astype(I32)
    spos_ref[...] = jnp.where(sel, p_ref[...].astype(I32), -1)

    cnt = jnp.sum(self32, axis=1, keepdims=True)
    cnt_ref[...] = jnp.broadcast_to(cnt, cnt_ref.shape).astype(I32)
    m_ref[...] = jnp.broadcast_to(cnt, m_ref.shape)
    prefix(False)
    tokp_ref[...] = p_ref[...].astype(I32)

    def within(j, _):
        blk = jnp.where(spos_ref[j] >= 0, 1.0, 0.0)
        w = jnp.dot(below, blk, preferred_element_type=F32)
        flat_ref[j] = (p_ref[j] + w).astype(I32)
        return 0
    lax.fori_loop(0, nb, within, 0)


def _route(aff3, cap):
    nb, ne, tm = aff3.shape
    shp = jax.ShapeDtypeStruct((nb, ne, tm), I32)
    return pl.pallas_call(
        functools.partial(_route_kernel, cap=cap, nb=nb),
        out_shape=[shp, shp, jax.ShapeDtypeStruct((nb + 1, ne, LANES), I32), shp, shp],
        scratch_shapes=[pltpu.VMEM((nb, ne, tm), F32), pltpu.VMEM((nb, ne, tm), F32)],
        compiler_params=pltpu.CompilerParams(vmem_limit_bytes=VMEM_LIMIT),
        name="route",
    )(aff3)


def _slots_kernel(cum_ref, spos_ref, aff_ref, flat_ref, idx_ref, gate_ref, dst_ref, ai_ref, ag_ref, ad_ref,
                  *, nb, tm):
    e = pl.program_id(0)
    j = pl.program_id(1)
    lo = j * SLOT_TILE
    hi = lo + SLOT_TILE
    slot = lo + lax.broadcasted_iota(I32, (SLOT_TILE, 1), 0)
    lane = lax.broadcasted_iota(I32, (1, tm), 1)
    ai_ref[...] = jnp.zeros_like(ai_ref)
    ag_ref[...] = jnp.zeros_like(ag_ref)
    ad_ref[...] = jnp.zeros_like(ad_ref)

    def body(b, _):
        cs = cum_ref[e * (nb + 1) + b]
        ce = cum_ref[e * (nb + 1) + b + 1]

        @pl.when((cs < hi) & (ce > lo))
        def _():
            hit = slot == spos_ref[b, pl.ds(e, 1), :]
            tok = (b * tm + lane).astype(F32)
            flat = flat_ref[b, pl.ds(e, 1), :].astype(F32)
            ai_ref[...] += jnp.sum(jnp.where(hit, tok, 0.0), axis=1, keepdims=True)
            ag_ref[...] += jnp.sum(jnp.where(hit, aff_ref[b, pl.ds(e, 1), :], 0.0), axis=1, keepdims=True)
            ad_ref[...] += jnp.sum(jnp.where(hit, flat, 0.0), axis=1, keepdims=True)
        return 0

    lax.fori_loop(0, nb, body, 0)
    idx_ref[...] = ai_ref[...].astype(I32)
    gate_ref[...] = ag_ref[...]
    dst_ref[...] = ad_ref[...].astype(I32)


def _slots(cum1d, spos, aff3, flat, cap):
    nb, ne, tm = spos.shape
    nt = cap // SLOT_TILE
    full = lambda a: pl.BlockSpec(a.shape, lambda e, j, c: (0,) * a.ndim)
    col = pl.BlockSpec((SLOT_TILE, 1), lambda e, j, c: (e * nt + j, 0))
    return pl.pallas_call(
        functools.partial(_slots_kernel, nb=nb, tm=tm),
        grid_spec=pltpu.PrefetchScalarGridSpec(
            num_scalar_prefetch=1,
            grid=(ne, nt),
            in_specs=[full(spos), full(aff3), full(flat)],
            out_specs=[col, col, col],
            scratch_shapes=[pltpu.VMEM((SLOT_TILE, 1), F32)] * 3),
        out_shape=[jax.ShapeDtypeStruct((ne * cap, 1), I32), jax.ShapeDtypeStruct((ne * cap, 1), F32),
                   jax.ShapeDtypeStruct((ne * cap, 1), I32)],
        compiler_params=_cparams(("arbitrary", "arbitrary")),
        name="slots",
    )(cum1d, spos, aff3, flat)


def _moe_kernel(idx_ref, dst_ref, gate_ref, w1_ref, w3_ref, w2_ref, h2_hbm, y_hbm,
                rows_ref, xe_ref, acc_ref, sem, *, ch, nf, mch):
    f = pl.program_id(2)
    nm = ch // mch

    def row_copy_in(s):
        t = idx_ref[0, 0, s]
        return pltpu.make_async_copy(h2_hbm.at[pl.ds(pl.multiple_of(t * ROW_TILES, ROW_TILES), ROW_TILES), :],
                                     rows_ref.at[pl.ds(pl.multiple_of(s * ROW_TILES, ROW_TILES), ROW_TILES), :],
                                     sem.at[0])

    def row_copy_out(s):
        d = dst_ref[0, 0, s]
        return pltpu.make_async_copy(rows_ref.at[pl.ds(pl.multiple_of(s * ROW_TILES, ROW_TILES), ROW_TILES), :],
                                     y_hbm.at[pl.ds(pl.multiple_of(d * ROW_TILES, ROW_TILES), ROW_TILES), :],
                                     sem.at[1])

    @pl.when(f == 0)
    def _():
        def start(s, _):
            row_copy_in(s).start()
            return 0
        lax.fori_loop(0, ch, start, 0)

        def wait(s, _):
            row_copy_in(s).wait()
            return 0
        lax.fori_loop(0, ch, wait, 0)
        for mc in range(nm):
            base = mc * mch * ROW_TILES
            parts = [rows_ref[pl.ds(base + kk, mch, stride=ROW_TILES), :] for kk in range(ROW_TILES)]
            xe_ref[mc * mch:(mc + 1) * mch, :] = jnp.concatenate(parts, axis=1).astype(BF16)

    w1 = w1_ref[0].astype(BF16)
    w3 = w3_ref[0].astype(BF16)
    w2 = w2_ref[0].astype(BF16)
    for mc in range(nm):
        sl = slice(mc * mch, (mc + 1) * mch)
        xm = xe_ref[sl, :]
        a = _dot(xm, w1)
        hid = (a * jax.nn.sigmoid(a) * _dot(xm, w3)).astype(BF16)
        part = _dot(hid, w2)

        @pl.when(f == 0)
        def _():
            acc_ref[sl, :] = part

        @pl.when(f > 0)
        def _():
            acc_ref[sl, :] += part

    @pl.when(f == nf - 1)
    def _():
        for mc in range(nm):
            sl = slice(mc * mch, (mc + 1) * mch)
            ye = acc_ref[sl, :] * gate_ref[sl, :]
            base = mc * mch * ROW_TILES
            for kk in range(ROW_TILES):
                rows_ref[pl.ds(base + kk, mch, stride=ROW_TILES), :] = ye[:, kk * LANES:(kk + 1) * LANES]

        def start(s, _):
            row_copy_out(s).start()
            return 0
        lax.fori_loop(0, ch, start, 0)

        def wait(s, _):
            row_copy_out(s).wait()
            return 0
        lax.fori_loop(0, ch, wait, 0)


def _moe(idx, dst, gate, h2rows, w1, w3, w2, cap):
    ne = N_EXPERTS
    ch = min(2048, cap)
    nh = cap // ch
    nf = D_FF // FF_TILE
    smem = lambda: pl.BlockSpec((1, 1, ch), lambda e, h, f: (e * nh + h, 0, 0), memory_space=pltpu.SMEM)
    return pl.pallas_call(
        functools.partial(_moe_kernel, ch=ch, nf=nf, mch=min(M_CHUNK, ch)),
        grid=(ne, nh, nf),
        in_specs=[smem(), smem(),
                  pl.BlockSpec((ch, 1), lambda e, h, f: (e * nh + h, 0)),
                  pl.BlockSpec((1, D_MODEL, FF_TILE), lambda e, h, f: (e, 0, f)),
                  pl.BlockSpec((1, D_MODEL, FF_TILE), lambda e, h, f: (e, 0, f)),
                  pl.BlockSpec((1, FF_TILE, D_MODEL), lambda e, h, f: (e, f, 0)),
                  pl.BlockSpec(memory_space=pl.ANY)],
        out_specs=pl.BlockSpec(memory_space=pl.ANY),
        out_shape=jax.ShapeDtypeStruct((ne * cap * ROW_TILES, LANES), F32),
        scratch_shapes=[pltpu.VMEM((ch * ROW_TILES, LANES), F32),
                        pltpu.VMEM((ch, D_MODEL), BF16),
                        pltpu.VMEM((ch, D_MODEL), F32),
                        pltpu.SemaphoreType.DMA((2,))],
        compiler_params=_cparams(("arbitrary", "arbitrary", "arbitrary")),
        name="moe",
    )(idx.reshape(ne * nh, 1, ch), dst.reshape(ne * nh, 1, ch), gate, w1, w3, w2, h2rows)


def _combine_kernel(pb_ref, pc_ref, pv_ref, y_ref, tokp_ref, cnt_ref, x1_ref, mod_ref, g2_ref, b2_ref,
                    o_ref, acc_ref, *, alpha):
    k = pl.program_id(0)
    blk = pb_ref[k]
    prev = pb_ref[jnp.maximum(k - 1, 0)]
    nxt = pb_ref[jnp.minimum(k + 1, pl.num_programs(0) - 1)]
    valid = pv_ref[k] == 1
    first = valid & ((k == 0) | (prev != blk))
    last = valid & ((k == pl.num_programs(0) - 1) | (nxt != blk) | (pv_ref[jnp.minimum(k + 1, pl.num_programs(0) - 1)] == 0))

    @pl.when(first)
    def _():
        acc_ref[...] = jnp.zeros_like(acc_ref)

    @pl.when(valid)
    def _():
        parts = [y_ref[pl.ds(kk, SLOT_TILE, stride=ROW_TILES), :] for kk in range(ROW_TILES)]
        rows = jnp.concatenate(parts, axis=1)
        hi = rows.astype(BF16)
        lo = (rows - hi.astype(F32)).astype(BF16)
        r = pc_ref[k] * SLOT_TILE + lax.broadcasted_iota(I32, (1, SLOT_TILE), 1)
        start = tokp_ref[...]
        own = jnp.where((r >= start) & (r < start + cnt_ref[...]), 1.0, 0.0).astype(BF16)
        acc_ref[...] += _dot(own, hi) + _dot(own, lo)

    @pl.when(last)
    def _():
        gate2 = mod_ref[0, 5:6, :]
        o_ref[...] = _ln(alpha * x1_ref[...] + gate2 * acc_ref[...]) * g2_ref[...] + b2_ref[...]


def _combine(pb, pc, pv, yrows, tokp_col, cnt_col, x1, mod3, wp, seq, alpha):
    t = x1.shape[0]
    tm = TM
    spb = seq // tm
    npairs = pb.shape[0]
    full = lambda a: pl.BlockSpec(a.shape, lambda k, pb, pc, pv: (0,) * a.ndim)
    row = pl.BlockSpec((tm, D_MODEL), lambda k, pb, pc, pv: (pb[k], 0))
    colspec = pl.BlockSpec((tm, 1), lambda k, pb, pc, pv: (pb[k], 0))
    return pl.pallas_call(
        functools.partial(_combine_kernel, alpha=alpha),
        grid_spec=pltpu.PrefetchScalarGridSpec(
            num_scalar_prefetch=3,
            grid=(npairs,),
            in_specs=[pl.BlockSpec((SLOT_TILE * ROW_TILES, LANES), lambda k, pb, pc, pv: (pc[k], 0)),
                      colspec, colspec, row,
                      pl.BlockSpec((1, 6, D_MODEL), lambda k, pb, pc, pv: (pb[k] // spb, 0, 0)),
                      full(wp["ln2_g"]), full(wp["ln2_b"])],
            out_specs=row,
            scratch_shapes=[pltpu.VMEM((tm, D_MODEL), F32)]),
        out_shape=jax.ShapeDtypeStruct((t, D_MODEL), F32),
        compiler_params=_cparams(("arbitrary",)),
        name="combine",
    )(pb, pc, pv, yrows, tokp_col, cnt_col, x1, mod3, wp["ln2_g"], wp["ln2_b"])


def _pair_schedule(row_start, n_rows_total, nb):
    nchunks = n_rows_total // SLOT_TILE
    npairs = nb + nchunks
    rs = row_start
    re = jnp.concatenate([rs[1:], jnp.array([n_rows_total], I32)])
    c_lo = jnp.minimum(rs // SLOT_TILE, nchunks - 1)
    c_hi = jnp.maximum(c_lo, (re - 1) // SLOT_TILE)
    n_b = c_hi - c_lo + 1
    ends = jnp.cumsum(n_b)
    starts = ends - n_b
    k = jnp.arange(npairs, dtype=I32)
    valid = k < ends[-1]
    kk = jnp.minimum(k, ends[-1] - 1)
    b = jnp.searchsorted(ends, kk, side="right").astype(I32)
    c = c_lo[b] + (kk - starts[b])
    return b.astype(I32), c.astype(I32), valid.astype(I32)


def _pack_weights(l, seq, w_in, b_in, ln_v_g, ln_v_b, w_spatial, b_spatial, w_oa, q_norm_g, w_uq, kv_norm_g,
                  w_ukv, w_ob, w_out, ln1_g, ln1_b, w_router, ln2_g, ln2_b):
    half = QK_ROPE // 2
    z32 = lambda rows: jnp.zeros((rows, half), F32)

    def rope_cols(w):
        return jnp.concatenate([w[:, :half], z32(w.shape[0]), w[:, half:], z32(w.shape[0])], axis=1)

    wi, bi = w_in[l], b_in[l][None, :]
    off_cq, off_ckv, off_kr = 2 * GM_WIDTH, 2 * GM_WIDTH + Q_LORA, 2 * GM_WIDTH + Q_LORA + KV_LORA
    off_ga = off_kr + QK_ROPE

    def repack(a):
        return jnp.concatenate([a[:, :off_kr], rope_cols(a[:, off_kr:off_ga]), a[:, off_ga:]], axis=1)

    wq = w_uq[l].reshape(Q_LORA, N_HEADS, QK_NOPE + QK_ROPE)
    wq = jnp.concatenate([wq[:, :, :QK_NOPE],
                          wq[:, :, QK_NOPE:QK_NOPE + half], jnp.zeros((Q_LORA, N_HEADS, half), F32),
                          wq[:, :, QK_NOPE + half:], jnp.zeros((Q_LORA, N_HEADS, half), F32)], axis=2)
    wkv = w_ukv[l].reshape(KV_LORA, N_HEADS, QK_NOPE + V_DIM)
    inv = ROPE_BASE ** (-jnp.arange(half, dtype=F32) / half)
    ang = jnp.arange(seq, dtype=F32)[:, None] * inv[None, :]
    cos, sin, zs = jnp.cos(ang), jnp.sin(ang), jnp.zeros((seq, half), F32)
    wr = jnp.zeros((D_MODEL, LANES), F32).at[:, :N_EXPERTS].set(w_router[l])
    return {
        "w_in": repack(wi).astype(BF16), "b_in": repack(bi),
        "ln_v_g": ln_v_g[l][None, :], "ln_v_b": ln_v_b[l][None, :],
        "w_s": w_spatial[l].astype(BF16),
        "b_s": jnp.broadcast_to(b_spatial[l][:, :, None], (GM_GROUPS, CHUNK, CHUNK)),
        "w_oa": w_oa[l].astype(BF16),
        "q_norm_g": q_norm_g[l][None, :], "w_uq": wq.reshape(Q_LORA, N_HEADS * HEAD_W).astype(BF16),
        "kv_norm_g": kv_norm_g[l][None, :],
        "w_uk": wkv[:, :, :QK_NOPE].reshape(KV_LORA, N_HEADS * QK_NOPE).astype(BF16),
        "w_uvt": wkv[:, :, QK_NOPE:].reshape(KV_LORA, N_HEADS * V_DIM).T.astype(BF16),
        "cos": jnp.concatenate([cos, zs, cos, zs], axis=1), "sin": jnp.concatenate([-sin, zs, sin, zs], axis=1),
        "w_ob": w_ob[l].astype(BF16), "w_out": w_out[l].astype(BF16),
        "ln1_g": ln1_g[l][None, :], "ln1_b": ln1_b[l][None, :], "w_router": wr,
        "ln2_g": ln2_g[l][None, :], "ln2_b": ln2_b[l][None, :],
    }


def _layer(x, c, l, alpha, w_ada, b_ada, w1, w3, w2, packed):
    bsz, seq, _ = x.shape
    t = bsz * seq
    cap = EC_CAPACITY_FACTOR * t // N_EXPERTS
    x2 = x.reshape(t, D_MODEL)
    mod3 = _mod(c, w_ada[l], b_ada[l])
    u, vn, q, k, vt, sga, sgb = _inproj(x2, mod3, packed, bsz, seq)
    ap = _spatial(u, vn, sga, packed)
    o = _attention(q, k, vt).reshape(t, D_MODEL)
    x1, h2, aff3 = _post(x2, mod3, o, ap, sgb, packed, bsz, seq, alpha)
    nb = t // TM
    spos, flat, cum, tokp, cnt = _route(aff3, cap)
    cum1d = cum[:, :, 0].T.reshape(-1)
    idx, gate, dst = _slots(cum1d, spos, aff3, flat, cap)
    yrows = _moe(idx, dst, gate, h2.reshape(t * ROW_TILES, LANES), w1[l], w3[l], w2[l], cap)
    tokp_col = tokp[:, 0, :].reshape(t, 1)
    cnt_col = cnt[:, 0, :].reshape(t, 1)
    pb, pc, pv = _pair_schedule(tokp[:, 0, 0], N_EXPERTS * cap, nb)
    y = _combine(pb, pc, pv, yrows, tokp_col, cnt_col, x1, mod3, packed, seq, alpha)
    return y.reshape(bsz, seq, D_MODEL)


def kernel(x_prompt, x_sample, c_prompt, c_sample, w_ada, b_ada, w_in, b_in, ln_v_g, ln_v_b, w_spatial, b_spatial,
           w_oa, q_norm_g, w_uq, kv_norm_g, w_ukv, w_ob, w_out, ln1_g, ln1_b, w_router, w1, w3, w2, ln2_g, ln2_b):
    depth = w_ada.shape[0]
    alpha = (2.0 * depth) ** 0.25
    outs = []
    for x, c in ((x_prompt, c_prompt), (x_sample, c_sample)):
        for l in range(depth):
            packed = _pack_weights(l, x.shape[1], w_in, b_in, ln_v_g, ln_v_b, w_spatial, b_spatial, w_oa,
                                   q_norm_g, w_uq, kv_norm_g, w_ukv, w_ob, w_out, ln1_g, ln1_b, w_router,
                                   ln2_g, ln2_b)
            x = _layer(x, c, l, alpha, w_ada, b_ada, w1, w3, w2, packed)
        outs.append(x)
    return tuple(outs)
```

```python
import functools
import math

import jax
import jax.numpy as jnp
from jax import lax
from jax.experimental import pallas as pl
from jax.experimental.pallas import tpu as pltpu

F32 = jnp.float32
BF16 = jnp.bfloat16
I32 = jnp.int32

D_MODEL = 1024
GM_WIDTH = 1024
GM_GROUPS = 8
CHUNK = 128
N_HEADS = 8
QK_NOPE = 128
QK_ROPE = 64
V_DIM = 128
Q_LORA = 384
KV_LORA = 256
ROPE_BASE = 10000.0
N_EXPERTS = 16
EC_CAPACITY_FACTOR = 2
D_FF = 2048
LN_EPS = 1e-5
RMS_EPS = 1e-6

LANES = 128
SUBLANES = 8
HEAD_W = 256
ROW_TILES = D_MODEL // LANES

C_U, C_V, C_CQ, C_CKV, C_KR, C_GA, C_GB, C_END = 0, 1024, 2048, 2432, 2688, 2816, 3840, 4864

TM_IN = 512
TQ = 1024
TM = 256
SLOT_TILE = 256
FF_TILE = 512
M_CHUNK = 512
VMEM_LIMIT = 56 * 1024 * 1024


def _cparams(sem):
    return pltpu.CompilerParams(dimension_semantics=sem, vmem_limit_bytes=VMEM_LIMIT)


def _ln(x):
    mu = jnp.mean(x, axis=-1, keepdims=True)
    xc = x - mu
    var = jnp.mean(xc * xc, axis=-1, keepdims=True)
    return xc * lax.rsqrt(var + LN_EPS)


def _rms(x):
    return x * lax.rsqrt(jnp.mean(x * x, axis=-1, keepdims=True) + RMS_EPS)


def _gelu(x):
    return 0.5 * x * (1.0 + lax.erf(x * (2.0 ** -0.5)))


def _dot(a, b):
    return jnp.dot(a, b, preferred_element_type=F32)


def _dot_nt(a, b):
    return lax.dot_general(a, b, (((1,), (1,)), ((), ())), preferred_element_type=F32)


def _mod_kernel(c_ref, w_ref, b_ref, o_ref):
    c = c_ref[...]
    s = c * jax.nn.sigmoid(c)
    o_ref[...] = jnp.dot(s, w_ref[...], preferred_element_type=F32,
                         precision=lax.Precision.HIGHEST) + b_ref[...]


def _mod(c, w_ada, b_ada):
    bsz = c.shape[0]
    bp = -(-bsz // SUBLANES) * SUBLANES
    cp = jnp.zeros((bp, D_MODEL), F32).at[:bsz].set(c)
    n = w_ada.shape[1]
    tn = 1024
    out = pl.pallas_call(
        _mod_kernel,
        grid=(n // tn,),
        in_specs=[pl.BlockSpec((bp, D_MODEL), lambda j: (0, 0)),
                  pl.BlockSpec((D_MODEL, tn), lambda j: (0, j)),
                  pl.BlockSpec((1, tn), lambda j: (0, j))],
        out_specs=pl.BlockSpec((bp, tn), lambda j: (0, j)),
        out_shape=jax.ShapeDtypeStruct((bp, n), F32),
        compiler_params=_cparams(("arbitrary",)),
        name="mod",
    )(cp, w_ada, b_ada.reshape(1, n))
    return out[:bsz].reshape(bsz, 6, D_MODEL)


def _inproj_kernel(x_ref, mod_ref, w_ref, b_ref, lvg_ref, lvb_ref, qg_ref, wuq_ref, kvg_ref,
                   wuk_ref, wuvt_ref, cos_ref, sin_ref,
                   u_ref, vn_ref, q_ref, k_ref, vt_ref, sga_ref, sgb_ref, *, qscale):
    x = x_ref[...]
    sh1 = mod_ref[0, 0:1, :]
    sc1 = mod_ref[0, 1:2, :]
    h = (_ln(x) * (1.0 + sc1) + sh1).astype(BF16)

    def proj(a, b):
        return _dot(h, w_ref[:, a:b]) + b_ref[:, a:b]

    cos = cos_ref[...]
    sin = sin_ref[...]

    def rope(t):
        return t * cos + pltpu.roll(t, 64, 1) * sin

    u_ref[...] = _gelu(proj(C_U, C_V)).astype(BF16)
    v = _gelu(proj(C_V, C_CQ))
    vn_ref[...] = (_ln(v) * lvg_ref[...] + lvb_ref[...]).astype(BF16)

    cqn = (_rms(proj(C_CQ, C_CKV)) * qg_ref[...]).astype(BF16)
    q = _dot(cqn, wuq_ref[...]) * qscale
    for hh in range(N_HEADS):
        o = hh * HEAD_W
        q_ref[0, hh, :, 0:QK_NOPE] = q[:, o:o + QK_NOPE].astype(BF16)
        q_ref[0, hh, :, QK_NOPE:HEAD_W] = rope(q[:, o + QK_NOPE:o + HEAD_W]).astype(BF16)

    ckvn = (_rms(proj(C_CKV, C_KR)) * kvg_ref[...]).astype(BF16)
    kn = _dot(ckvn, wuk_ref[...])
    kr = rope(proj(C_KR, C_GA)).astype(BF16)
    for hh in range(N_HEADS):
        k_ref[0, hh, :, 0:QK_NOPE] = kn[:, hh * QK_NOPE:(hh + 1) * QK_NOPE].astype(BF16)
        k_ref[0, hh, :, QK_NOPE:HEAD_W] = kr
    vt = _dot_nt(wuvt_ref[...], ckvn)
    for hh in range(N_HEADS):
        vt_ref[0, hh, 0] = vt[hh * V_DIM:(hh + 1) * V_DIM, :].astype(BF16)

    sga_ref[...] = jax.nn.sigmoid(proj(C_GA, C_GB)).astype(BF16)
    sgb_ref[...] = jax.nn.sigmoid(proj(C_GB, C_END)).astype(BF16)


def _inproj(x2, mod3, wp, bsz, seq):
    t = bsz * seq
    tm = min(TM_IN, seq)
    spb = seq // tm
    full = lambda a: pl.BlockSpec(a.shape, lambda i: (0,) * a.ndim)
    qscale = (QK_NOPE + QK_ROPE) ** -0.5 * math.log2(math.e)
    row = pl.BlockSpec((tm, D_MODEL), lambda i: (i, 0))
    qk_spec = pl.BlockSpec((1, N_HEADS, tm, HEAD_W), lambda i: (i // spb, 0, i % spb, 0))
    ins = [x2, mod3, wp["w_in"], wp["b_in"], wp["ln_v_g"], wp["ln_v_b"], wp["q_norm_g"], wp["w_uq"],
           wp["kv_norm_g"], wp["w_uk"], wp["w_uvt"], wp["cos"], wp["sin"]]
    in_specs = [row, pl.BlockSpec((1, 6, D_MODEL), lambda i: (i // spb, 0, 0))]
    in_specs += [full(a) for a in ins[2:11]]
    in_specs += [pl.BlockSpec((tm, LANES), lambda i: (i % spb, 0))] * 2
    return pl.pallas_call(
        functools.partial(_inproj_kernel, qscale=qscale),
        grid=(t // tm,),
        in_specs=in_specs,
        out_specs=[row, row, qk_spec, qk_spec,
                   pl.BlockSpec((1, N_HEADS, 1, V_DIM, tm), lambda i: (i // spb, 0, i % spb, 0, 0)),
                   row, row],
        out_shape=[jax.ShapeDtypeStruct((t, D_MODEL), BF16),
                   jax.ShapeDtypeStruct((t, D_MODEL), BF16),
                   jax.ShapeDtypeStruct((bsz, N_HEADS, seq, HEAD_W), BF16),
                   jax.ShapeDtypeStruct((bsz, N_HEADS, seq, HEAD_W), BF16),
                   jax.ShapeDtypeStruct((bsz, N_HEADS, spb, V_DIM, tm), BF16),
                   jax.ShapeDtypeStruct((t, D_MODEL), BF16),
                   jax.ShapeDtypeStruct((t, D_MODEL), BF16)],
        compiler_params=_cparams(("arbitrary",)),
        name="inproj",
    )(*ins)


def _spatial_kernel(u_ref, vn_ref, ws_ref, bs_ref, woa_ref, sga_ref, o_ref, g_ref):
    tm = u_ref.shape[0]
    nc = tm // CHUNK
    for g in range(GM_GROUPS):
        c0 = g * CHUNK
        rhs = jnp.concatenate([vn_ref[n * CHUNK:(n + 1) * CHUNK, c0:c0 + CHUNK] for n in range(nc)], axis=1)
        mixed = _dot(ws_ref[g], rhs)
        bias = bs_ref[g]
        for n in range(nc):
            m = mixed[:, n * CHUNK:(n + 1) * CHUNK] + bias
            uu = u_ref[n * CHUNK:(n + 1) * CHUNK, c0:c0 + CHUNK].astype(F32)
            g_ref[n * CHUNK:(n + 1) * CHUNK, c0:c0 + CHUNK] = (uu * m).astype(BF16)
    ya = _dot(g_ref[...], woa_ref[...])
    o_ref[...] = (sga_ref[...].astype(F32) * ya).astype(BF16)


def _spatial(u, vn, sga, wp):
    t = u.shape[0]
    tm = TM
    row = pl.BlockSpec((tm, D_MODEL), lambda i: (i, 0))
    full = lambda a: pl.BlockSpec(a.shape, lambda i: (0,) * a.ndim)
    return pl.pallas_call(
        _spatial_kernel,
        grid=(t // tm,),
        in_specs=[row, row, full(wp["w_s"]), full(wp["b_s"]), full(wp["w_oa"]), row],
        out_specs=row,
        out_shape=jax.ShapeDtypeStruct((t, D_MODEL), BF16),
        scratch_shapes=[pltpu.VMEM((tm, GM_WIDTH), BF16)],
        compiler_params=_cparams(("arbitrary",)),
        name="spatial",
    )(u, vn, wp["w_s"], wp["b_s"], wp["w_oa"], sga)


def _attn_kernel(q_ref, k_ref, vt_ref, o_ref, acc_ref, sa_ref, sb_ref, pa_ref, pb_ref, *, nk, tk):
    q = q_ref[0, 0]
    tq = q.shape[0]
    acc_ref[...] = jnp.zeros_like(acc_ref)
    pb_ref[...] = jnp.zeros_like(pb_ref)

    def scores(c):
        return _dot_nt(k_ref[0, 0, pl.ds(pl.multiple_of(c * tk, tk), tk), :], q)

    def softmax(s_ref, p_ref, m, l):
        s = s_ref[...]
        m_new = jnp.maximum(m, jnp.max(s, axis=0, keepdims=True))
        p = jnp.exp2(s - m_new)
        p_ref[...] = p.astype(BF16)
        alpha = jnp.exp2(m - m_new)
        return m_new, alpha * l + jnp.sum(p, axis=0, keepdims=True), alpha

    def accumulate(alpha, c, p_ref):
        return alpha * acc_ref[...] + _dot(vt_ref[0, 0, c], p_ref[...])

    sa_ref[...] = scores(0)

    def body(i, carry):
        m, l, alpha = carry
        c0 = 2 * i
        sb_ref[...] = scores(c0 + 1)
        acc_ref[...] = accumulate(alpha, jnp.maximum(c0 - 1, 0), pb_ref)
        m, l, alpha = softmax(sa_ref, pa_ref, m, l)
        sa_ref[...] = scores(jnp.minimum(c0 + 2, nk - 1))
        acc_ref[...] = accumulate(alpha, c0, pa_ref)
        return softmax(sb_ref, pb_ref, m, l)

    init = (jnp.full((1, tq), -jnp.inf, F32), jnp.zeros((1, tq), F32), jnp.ones((1, tq), F32))
    _, l, alpha = lax.fori_loop(0, nk // 2, body, init)
    o = accumulate(alpha, nk - 1, pb_ref) / l
    o_ref[0] = o.T.astype(BF16)


def _attention(q, k, vt):
    bsz, nh, seq, _ = q.shape
    nk, tk = vt.shape[2], vt.shape[4]
    assert nk % 2 == 0
    tq = min(TQ, seq)
    return pl.pallas_call(
        functools.partial(_attn_kernel, nk=nk, tk=tk),
        grid=(bsz, nh, seq // tq),
        in_specs=[pl.BlockSpec((1, 1, tq, HEAD_W), lambda b, h, i: (b, h, i, 0)),
                  pl.BlockSpec((1, 1, seq, HEAD_W), lambda b, h, i: (b, h, 0, 0)),
                  pl.BlockSpec((1, 1, nk, V_DIM, tk), lambda b, h, i: (b, h, 0, 0, 0))],
        out_specs=pl.BlockSpec((1, tq, V_DIM), lambda b, h, i: (b, i, h)),
        out_shape=jax.ShapeDtypeStruct((bsz, seq, nh * V_DIM), BF16),
        scratch_shapes=[pltpu.VMEM((V_DIM, tq), F32), pltpu.VMEM((tk, tq), F32), pltpu.VMEM((tk, tq), F32),
                        pltpu.VMEM((tk, tq), BF16), pltpu.VMEM((tk, tq), BF16)],
        compiler_params=_cparams(("arbitrary", "arbitrary", "arbitrary")),
        name="attn",
    )(q, k, vt)


def _post_kernel(x_ref, mod_ref, o_ref, ap_ref, sgb_ref, wob_ref, wout_ref, g1_ref, b1_ref, wr_ref,
                 x1_ref, h2_ref, aff_ref, *, alpha):
    yb = _dot(o_ref[...], wob_ref[...])
    mixin = (ap_ref[...].astype(F32) + sgb_ref[...].astype(F32) * yb).astype(BF16)
    mix = _dot(mixin, wout_ref[...])
    gate1 = mod_ref[0, 2:3, :]
    sh2 = mod_ref[0, 3:4, :]
    sc2 = mod_ref[0, 4:5, :]
    x1 = _ln(alpha * x_ref[...] + gate1 * mix) * g1_ref[...] + b1_ref[...]
    x1_ref[...] = x1
    h2 = _ln(x1) * (1.0 + sc2) + sh2
    h2_ref[...] = h2
    logits = jnp.dot(h2, wr_ref[...], preferred_element_type=F32,
                     precision=lax.Precision.HIGHEST)
    lane = lax.broadcasted_iota(I32, logits.shape, 1)
    logits = jnp.where(lane < N_EXPERTS, logits, -jnp.inf)
    ex = jnp.exp(logits - jnp.max(logits, axis=-1, keepdims=True))
    aff = ex / jnp.sum(ex, axis=-1, keepdims=True)
    aff_ref[0] = aff.T[0:N_EXPERTS, :]


def _post(x2, mod3, o, ap, sgb, wp, bsz, seq, alpha):
    t = x2.shape[0]
    tm = TM
    spb = seq // tm
    row = pl.BlockSpec((tm, D_MODEL), lambda i: (i, 0))
    full = lambda a: pl.BlockSpec(a.shape, lambda i: (0,) * a.ndim)
    return pl.pallas_call(
        functools.partial(_post_kernel, alpha=alpha),
        grid=(t // tm,),
        in_specs=[row, pl.BlockSpec((1, 6, D_MODEL), lambda i: (i // spb, 0, 0)), row, row, row,
                  full(wp["w_ob"]), full(wp["w_out"]), full(wp["ln1_g"]), full(wp["ln1_b"]),
                  full(wp["w_router"])],
        out_specs=[row, row, pl.BlockSpec((1, N_EXPERTS, tm), lambda i: (i, 0, 0))],
        out_shape=[jax.ShapeDtypeStruct((t, D_MODEL), F32),
                   jax.ShapeDtypeStruct((t, D_MODEL), F32),
                   jax.ShapeDtypeStruct((t // tm, N_EXPERTS, tm), F32)],
        compiler_params=_cparams(("arbitrary",)),
        name="post",
    )(x2, mod3, o, ap, sgb, wp["w_ob"], wp["w_out"], wp["ln1_g"], wp["ln1_b"], wp["w_router"])


def _route_kernel(aff_ref, spos_ref, flat_ref, cum_ref, tokp_ref, cnt_ref, m_ref, p_ref, *, cap, nb):
    bits = pltpu.bitcast(aff_ref[...], I32)
    tm = bits.shape[2]

    def count(mask):
        c = jnp.sum(jnp.where(mask, 1, 0), axis=0, keepdims=True)
        return jnp.sum(c, axis=2, keepdims=True)

    def bisect(i, thr):
        cand = thr | lax.shift_left(jnp.int32(1), 30 - i)
        return jnp.where(count(bits >= cand) >= cap, cand, thr)

    thr = lax.fori_loop(0, 31, bisect, jnp.zeros((1, N_EXPERTS, 1), I32))
    gt = bits > thr
    eq = bits == thr
    need = cap - count(gt)

    r = lax.broadcasted_iota(I32, (tm, tm), 0)
    c = lax.broadcasted_iota(I32, (tm, tm), 1)
    upper = jnp.where(r <= c, 1.0, 0.0).astype(BF16)
    er = lax.broadcasted_iota(I32, (N_EXPERTS, N_EXPERTS), 0)
    ec = lax.broadcasted_iota(I32, (N_EXPERTS, N_EXPERTS), 1)
    below = jnp.where(ec < er, 1.0, 0.0).astype(F32)

    def prefix(write_cum):
        def body(j, carry):
            blk = m_ref[j]
            incl = _dot(blk.astype(BF16), upper)
            p_ref[j] = incl - blk + carry
            if write_cum:
                cum_ref[j] = jnp.broadcast_to(carry, (N_EXPERTS, LANES)).astype(I32)
            return carry + jnp.sum(blk, axis=1, keepdims=True)
        return lax.fori_loop(0, nb, body, jnp.zeros((N_EXPERTS, 1), F32))

    m_ref[...] = jnp.where(eq, 1.0, 0.0)
    prefix(False)
    sel = gt | (eq & (p_ref[...] < need.astype(F32)))
    self32 = jnp.where(sel, 1.0, 0.0)
    m_ref[...] = self32
    total = prefix(True)
    cum_ref[nb] = jnp.broadcast_to(total, (N_EXPERTS, LANES)).astype(I32)
    spos_ref[...] = jnp.where(sel, p_ref[...].astype(I32), -1)

    cnt = jnp.sum(self32, axis=1, keepdims=True)
    cnt_ref[...] = jnp.broadcast_to(cnt, cnt_ref.shape).astype(I32)
    m_ref[...] = jnp.broadcast_to(cnt, m_ref.shape)
    prefix(False)
    tokp_ref[...] = p_ref[...].astype(I32)

    def within(j, _):
        blk = jnp.where(spos_ref[j] >= 0, 1.0, 0.0)
        w = jnp.dot(below, blk, preferred_element_type=F32)
        flat_ref[j] = (p_ref[j] + w).astype(I32)
        return 0
    lax.fori_loop(0, nb, within, 0)


def _route(aff3, cap):
    nb, ne, tm = aff3.shape
    shp = jax.ShapeDtypeStruct((nb, ne, tm), I32)
    return pl.pallas_call(
        functools.partial(_route_kernel, cap=cap, nb=nb),
        out_shape=[shp, shp, jax.ShapeDtypeStruct((nb + 1, ne, LANES), I32), shp, shp],
        scratch_shapes=[pltpu.VMEM((nb, ne, tm), F32), pltpu.VMEM((nb, ne, tm), F32)],
        compiler_params=pltpu.CompilerParams(vmem_limit_bytes=VMEM_LIMIT),
        name="route",
    )(aff3)


def _slots_kernel(blo_ref, bhi_ref, spos_ref, aff_ref, flat_ref, idx_ref, gate_ref, dst_ref, acc_ref, *, tm, nt):
    e = pl.program_id(0)
    j = pl.program_id(1)
    slot = j * SLOT_TILE + lax.broadcasted_iota(I32, (SLOT_TILE, 1), 0)
    lane = lax.broadcasted_iota(I32, (1, tm), 1).astype(F32)
    zero = jnp.zeros((1, tm), F32)
    acc_ref[...] = jnp.zeros_like(acc_ref)

    def body(b, _):
        hit = slot == spos_ref[b, pl.ds(e, 1), :]
        onehot = jnp.where(hit, 1.0, 0.0).astype(BF16)
        g = aff_ref[b, pl.ds(e, 1), :]
        g0 = g.astype(BF16).astype(F32)
        g1 = (g - g0).astype(BF16).astype(F32)
        g2 = g - g0 - g1
        fl = flat_ref[b, pl.ds(e, 1), :]
        rows = [lane, (jnp.zeros((1, tm), I32) + b).astype(F32), g0, g1, g2,
                lax.shift_right_logical(fl, 8).astype(F32), (fl & 255).astype(F32), zero]
        vals = jnp.concatenate(rows + [zero] * 8, axis=0).astype(BF16)
        acc_ref[...] += _dot_nt(onehot, vals)
        return 0

    lax.fori_loop(blo_ref[e * nt + j], bhi_ref[e * nt + j], body, 0)
    a = acc_ref[...]
    idx_ref[...] = (a[:, 1:2] * tm + a[:, 0:1]).astype(I32)
    gate_ref[...] = a[:, 2:3] + a[:, 3:4] + a[:, 4:5]
    dst_ref[...] = (a[:, 5:6] * 256.0 + a[:, 6:7]).astype(I32)


def _slots(cum, spos, aff3, flat, cap):
    nb, ne, tm = spos.shape
    nt = cap // SLOT_TILE
    edges = jnp.arange(nt + 1, dtype=I32) * SLOT_TILE
    cs, ce = cum[:-1].T, cum[1:].T
    blo = jnp.sum(ce[:, None, :] <= edges[None, :-1, None], axis=2).astype(I32).reshape(-1)
    bhi = jnp.sum(cs[:, None, :] < edges[None, 1:, None], axis=2).astype(I32).reshape(-1)
    full = lambda a: pl.BlockSpec(a.shape, lambda e, j, lo, hi: (0,) * a.ndim)
    col = pl.BlockSpec((SLOT_TILE, 1), lambda e, j, lo, hi: (e * nt + j, 0))
    return pl.pallas_call(
        functools.partial(_slots_kernel, tm=tm, nt=nt),
        grid_spec=pltpu.PrefetchScalarGridSpec(
            num_scalar_prefetch=2,
            grid=(ne, nt),
            in_specs=[full(spos), full(aff3), full(flat)],
            out_specs=[col, col, col],
            scratch_shapes=[pltpu.VMEM((SLOT_TILE, 16), F32)]),
        out_shape=[jax.ShapeDtypeStruct((ne * cap, 1), I32), jax.ShapeDtypeStruct((ne * cap, 1), F32),
                   jax.ShapeDtypeStruct((ne * cap, 1), I32)],
        compiler_params=_cparams(("arbitrary", "arbitrary")),
        name="slots",
    )(blo, bhi, spos, aff3, flat)


def _moe_kernel(idx_ref, dst_ref, gate_ref, w1_ref, w3_ref, w2_ref, h2_hbm, y_hbm,
                rows_ref, xe_ref, acc_ref, sem, *, ch, nf, mch):
    f = pl.program_id(2)
    nm = ch // mch

    def row_copy_in(s):
        t = idx_ref[0, 0, s]
        return pltpu.make_async_copy(h2_hbm.at[pl.ds(pl.multiple_of(t * ROW_TILES, ROW_TILES), ROW_TILES), :],
                                     rows_ref.at[pl.ds(pl.multiple_of(s * ROW_TILES, ROW_TILES), ROW_TILES), :],
                                     sem.at[0])

    def row_copy_out(s):
        d = dst_ref[0, 0, s]
        return pltpu.make_async_copy(rows_ref.at[pl.ds(pl.multiple_of(s * ROW_TILES, ROW_TILES), ROW_TILES), :],
                                     y_hbm.at[pl.ds(pl.multiple_of(d * ROW_TILES, ROW_TILES), ROW_TILES), :],
                                     sem.at[1])

    @pl.when(f == 0)
    def _():
        def start(s, _):
            row_copy_in(s).start()
            return 0
        lax.fori_loop(0, ch, start, 0)

        def wait(s, _):
            row_copy_in(s).wait()
            return 0
        lax.fori_loop(0, ch, wait, 0)
        for mc in range(nm):
            base = mc * mch * ROW_TILES
            parts = [rows_ref[pl.ds(base + kk, mch, stride=ROW_TILES), :] for kk in range(ROW_TILES)]
            xe_ref[mc * mch:(mc + 1) * mch, :] = jnp.concatenate(parts, axis=1).astype(BF16)

    w1 = w1_ref[0].astype(BF16)
    w3 = w3_ref[0].astype(BF16)
    w2 = w2_ref[0].astype(BF16)
    for mc in range(nm):
        sl = slice(mc * mch, (mc + 1) * mch)
        xm = xe_ref[sl, :]
        a = _dot(xm, w1)
        hid = (a * jax.nn.sigmoid(a) * _dot(xm, w3)).astype(BF16)
        part = _dot(hid, w2)

        @pl.when(f == 0)
        def _():
            acc_ref[sl, :] = part

        @pl.when(f > 0)
        def _():
            acc_ref[sl, :] += part

    @pl.when(f == nf - 1)
    def _():
        for mc in range(nm):
            sl = slice(mc * mch, (mc + 1) * mch)
            ye = acc_ref[sl, :] * gate_ref[sl, :]
            base = mc * mch * ROW_TILES
            for kk in range(ROW_TILES):
                rows_ref[pl.ds(base + kk, mch, stride=ROW_TILES), :] = ye[:, kk * LANES:(kk + 1) * LANES]

        def start(s, _):
            row_copy_out(s).start()
            return 0
        lax.fori_loop(0, ch, start, 0)

        def wait(s, _):
            row_copy_out(s).wait()
            return 0
        lax.fori_loop(0, ch, wait, 0)


def _moe(idx, dst, gate, h2rows, w1, w3, w2, cap):
    ne = N_EXPERTS
    ch = min(2048, cap)
    nh = cap // ch
    nf = D_FF // FF_TILE
    smem = lambda: pl.BlockSpec((1, 1, ch), lambda e, h, f: (e * nh + h, 0, 0), memory_space=pltpu.SMEM)
    return pl.pallas_call(
        functools.partial(_moe_kernel, ch=ch, nf=nf, mch=min(M_CHUNK, ch)),
        grid=(ne, nh, nf),
        in_specs=[smem(), smem(),
                  pl.BlockSpec((ch, 1), lambda e, h, f: (e * nh + h, 0)),
                  pl.BlockSpec((1, D_MODEL, FF_TILE), lambda e, h, f: (e, 0, f)),
                  pl.BlockSpec((1, D_MODEL, FF_TILE), lambda e, h, f: (e, 0, f)),
                  pl.BlockSpec((1, FF_TILE, D_MODEL), lambda e, h, f: (e, f, 0)),
                  pl.BlockSpec(memory_space=pl.ANY)],
        out_specs=pl.BlockSpec(memory_space=pl.ANY),
        out_shape=jax.ShapeDtypeStruct((ne * cap * ROW_TILES, LANES), F32),
        scratch_shapes=[pltpu.VMEM((ch * ROW_TILES, LANES), F32),
                        pltpu.VMEM((ch, D_MODEL), BF16),
                        pltpu.VMEM((ch, D_MODEL), F32),
                        pltpu.SemaphoreType.DMA((2,))],
        compiler_params=_cparams(("arbitrary", "arbitrary", "arbitrary")),
        name="moe",
    )(idx.reshape(ne * nh, 1, ch), dst.reshape(ne * nh, 1, ch), gate, w1, w3, w2, h2rows)


def _combine_kernel(pb_ref, pc_ref, pv_ref, y_ref, tokp_ref, cnt_ref, x1_ref, mod_ref, g2_ref, b2_ref,
                    o_ref, acc_ref, *, alpha):
    k = pl.program_id(0)
    blk = pb_ref[k]
    prev = pb_ref[jnp.maximum(k - 1, 0)]
    nxt = pb_ref[jnp.minimum(k + 1, pl.num_programs(0) - 1)]
    valid = pv_ref[k] == 1
    first = valid & ((k == 0) | (prev != blk))
    last = valid & ((k == pl.num_programs(0) - 1) | (nxt != blk) | (pv_ref[jnp.minimum(k + 1, pl.num_programs(0) - 1)] == 0))

    @pl.when(first)
    def _():
        acc_ref[...] = jnp.zeros_like(acc_ref)

    @pl.when(valid)
    def _():
        parts = [y_ref[pl.ds(kk, SLOT_TILE, stride=ROW_TILES), :] for kk in range(ROW_TILES)]
        rows = jnp.concatenate(parts, axis=1)
        hi = rows.astype(BF16)
        lo = (rows - hi.astype(F32)).astype(BF16)
        r = pc_ref[k] * SLOT_TILE + lax.broadcasted_iota(I32, (1, SLOT_TILE), 1)
        start = tokp_ref[...]
        own = jnp.where((r >= start) & (r < start + cnt_ref[...]), 1.0, 0.0).astype(BF16)
        acc_ref[...] += _dot(own, hi) + _dot(own, lo)

    @pl.when(last)
    def _():
        gate2 = mod_ref[0, 5:6, :]
        o_ref[...] = _ln(alpha * x1_ref[...] + gate2 * acc_ref[...]) * g2_ref[...] + b2_ref[...]


def _combine(pb, pc, pv, yrows, tokp_col, cnt_col, x1, mod3, wp, seq, alpha):
    t = x1.shape[0]
    tm = TM
    spb = seq // tm
    npairs = pb.shape[0]
    full = lambda a: pl.BlockSpec(a.shape, lambda k, pb, pc, pv: (0,) * a.ndim)
    row = pl.BlockSpec((tm, D_MODEL), lambda k, pb, pc, pv: (pb[k], 0))
    colspec = pl.BlockSpec((tm, 1), lambda k, pb, pc, pv: (pb[k], 0))
    return pl.pallas_call(
        functools.partial(_combine_kernel, alpha=alpha),
        grid_spec=pltpu.PrefetchScalarGridSpec(
            num_scalar_prefetch=3,
            grid=(npairs,),
            in_specs=[pl.BlockSpec((SLOT_TILE * ROW_TILES, LANES), lambda k, pb, pc, pv: (pc[k], 0)),
                      colspec, colspec, row,
                      pl.BlockSpec((1, 6, D_MODEL), lambda k, pb, pc, pv: (pb[k] // spb, 0, 0)),
                      full(wp["ln2_g"]), full(wp["ln2_b"])],
            out_specs=row,
            scratch_shapes=[pltpu.VMEM((tm, D_MODEL), F32)]),
        out_shape=jax.ShapeDtypeStruct((t, D_MODEL), F32),
        compiler_params=_cparams(("arbitrary",)),
        name="combine",
    )(pb, pc, pv, yrows, tokp_col, cnt_col, x1, mod3, wp["ln2_g"], wp["ln2_b"])


def _pair_schedule(row_start, n_rows_total, nb):
    nchunks = n_rows_total // SLOT_TILE
    npairs = nb + nchunks
    rs = row_start
    re = jnp.concatenate([rs[1:], jnp.array([n_rows_total], I32)])
    c_lo = jnp.minimum(rs // SLOT_TILE, nchunks - 1)
    c_hi = jnp.maximum(c_lo, (re - 1) // SLOT_TILE)
    n_b = c_hi - c_lo + 1
    ends = jnp.cumsum(n_b)
    starts = ends - n_b
    k = jnp.arange(npairs, dtype=I32)
    valid = k < ends[-1]
    kk = jnp.minimum(k, ends[-1] - 1)
    b = jnp.sum(ends[None, :] <= kk[:, None], axis=1).astype(I32)
    c = c_lo[b] + (kk - starts[b])
    return b.astype(I32), c.astype(I32), valid.astype(I32)


def _pack_weights(l, seq, w_in, b_in, ln_v_g, ln_v_b, w_spatial, b_spatial, w_oa, q_norm_g, w_uq, kv_norm_g,
                  w_ukv, w_ob, w_out, ln1_g, ln1_b, w_router, ln2_g, ln2_b):
    half = QK_ROPE // 2
    z32 = lambda rows: jnp.zeros((rows, half), F32)

    def rope_cols(w):
        return jnp.concatenate([w[:, :half], z32(w.shape[0]), w[:, half:], z32(w.shape[0])], axis=1)

    wi, bi = w_in[l], b_in[l][None, :]
    off_cq, off_ckv, off_kr = 2 * GM_WIDTH, 2 * GM_WIDTH + Q_LORA, 2 * GM_WIDTH + Q_LORA + KV_LORA
    off_ga = off_kr + QK_ROPE

    def repack(a):
        return jnp.concatenate([a[:, :off_kr], rope_cols(a[:, off_kr:off_ga]), a[:, off_ga:]], axis=1)

    wq = w_uq[l].reshape(Q_LORA, N_HEADS, QK_NOPE + QK_ROPE)
    wq = jnp.concatenate([wq[:, :, :QK_NOPE],
                          wq[:, :, QK_NOPE:QK_NOPE + half], jnp.zeros((Q_LORA, N_HEADS, half), F32),
                          wq[:, :, QK_NOPE + half:], jnp.zeros((Q_LORA, N_HEADS, half), F32)], axis=2)
    wkv = w_ukv[l].reshape(KV_LORA, N_HEADS, QK_NOPE + V_DIM)
    inv = ROPE_BASE ** (-jnp.arange(half, dtype=F32) / half)
    ang = jnp.arange(seq, dtype=F32)[:, None] * inv[None, :]
    cos, sin, zs = jnp.cos(ang), jnp.sin(ang), jnp.zeros((seq, half), F32)
    wr = jnp.zeros((D_MODEL, LANES), F32).at[:, :N_EXPERTS].set(w_router[l])
    return {
        "w_in": repack(wi).astype(BF16), "b_in": repack(bi),
        "ln_v_g": ln_v_g[l][None, :], "ln_v_b": ln_v_b[l][None, :],
        "w_s": w_spatial[l].astype(BF16),
        "b_s": jnp.broadcast_to(b_spatial[l][:, :, None], (GM_GROUPS, CHUNK, CHUNK)),
        "w_oa": w_oa[l].astype(BF16),
        "q_norm_g": q_norm_g[l][None, :], "w_uq": wq.reshape(Q_LORA, N_HEADS * HEAD_W).astype(BF16),
        "kv_norm_g": kv_norm_g[l][None, :],
        "w_uk": wkv[:, :, :QK_NOPE].reshape(KV_LORA, N_HEADS * QK_NOPE).astype(BF16),
        "w_uvt": wkv[:, :, QK_NOPE:].reshape(KV_LORA, N_HEADS * V_DIM).T.astype(BF16),
        "cos": jnp.concatenate([cos, zs, cos, zs], axis=1), "sin": jnp.concatenate([-sin, zs, sin, zs], axis=1),
        "w_ob": w_ob[l].astype(BF16), "w_out": w_out[l].astype(BF16),
        "ln1_g": ln1_g[l][None, :], "ln1_b": ln1_b[l][None, :], "w_router": wr,
        "ln2_g": ln2_g[l][None, :], "ln2_b": ln2_b[l][None, :],
    }


def _layer(x, c, l, alpha, w_ada, b_ada, w1, w3, w2, packed):
    bsz, seq, _ = x.shape
    t = bsz * seq
    cap = EC_CAPACITY_FACTOR * t // N_EXPERTS
    x2 = x.reshape(t, D_MODEL)
    mod3 = _mod(c, w_ada[l], b_ada[l])
    u, vn, q, k, vt, sga, sgb = _inproj(x2, mod3, packed, bsz, seq)
    ap = _spatial(u, vn, sga, packed)
    o = _attention(q, k, vt).reshape(t, D_MODEL)
    x1, h2, aff3 = _post(x2, mod3, o, ap, sgb, packed, bsz, seq, alpha)
    nb = t // TM
    spos, flat, cum, tokp, cnt = _route(aff3, cap)
    idx, gate, dst = _slots(cum[:, :, 0], spos, aff3, flat, cap)
    yrows = _moe(idx, dst, gate, h2.reshape(t * ROW_TILES, LANES), w1[l], w3[l], w2[l], cap)
    tokp_col = tokp[:, 0, :].reshape(t, 1)
    cnt_col = cnt[:, 0, :].reshape(t, 1)
    pb, pc, pv = _pair_schedule(tokp[:, 0, 0], N_EXPERTS * cap, nb)
    y = _combine(pb, pc, pv, yrows, tokp_col, cnt_col, x1, mod3, packed, seq, alpha)
    return y.reshape(bsz, seq, D_MODEL)


def kernel(x_prompt, x_sample, c_prompt, c_sample, w_ada, b_ada, w_in, b_in, ln_v_g, ln_v_b, w_spatial, b_spatial,
           w_oa, q_norm_g, w_uq, kv_norm_g, w_ukv, w_ob, w_out, ln1_g, ln1_b, w_router, w1, w3, w2, ln2_g, ln2_b):
    depth = w_ada.shape[0]
    alpha = (2.0 * depth) ** 0.25
    outs = []
    for x, c in ((x_prompt, c_prompt), (x_sample, c_sample)):
        for l in range(depth):
            packed = _pack_weights(l, x.shape[1], w_in, b_in, ln_v_g, ln_v_b, w_spatial, b_spatial, w_oa,
                                   q_norm_g, w_uq, kv_norm_g, w_ukv, w_ob, w_out, ln1_g, ln1_b, w_router,
                                   ln2_g, ln2_b)
            x = _layer(x, c, l, alpha, w_ada, b_ada, w1, w3, w2, packed)
        outs.append(x)
    return tuple(outs)
```

```python
import functools
import math

import jax
import jax.numpy as jnp
from jax import lax
from jax.experimental import pallas as pl
from jax.experimental.pallas import tpu as pltpu

F32 = jnp.float32
BF16 = jnp.bfloat16
I32 = jnp.int32

D_MODEL = 1024
GM_WIDTH = 1024
GM_GROUPS = 8
CHUNK = 128
N_HEADS = 8
QK_NOPE = 128
QK_ROPE = 64
V_DIM = 128
Q_LORA = 384
KV_LORA = 256
ROPE_BASE = 10000.0
N_EXPERTS = 16
EC_CAPACITY_FACTOR = 2
D_FF = 2048
LN_EPS = 1e-5
RMS_EPS = 1e-6

LANES = 128
SUBLANES = 8
HEAD_W = 256
ROW_TILES = D_MODEL // LANES

C_U, C_V, C_CQ, C_CKV, C_KR, C_GA, C_GB, C_END = 0, 1024, 2048, 2432, 2688, 2816, 3840, 4864

TM_IN = 512
TQ = 1024
TM = 256
SLOT_TILE = 256
FF_TILE = 512
M_CHUNK = 512
MOE_ROWS = 1024
VMEM_LIMIT = 56 * 1024 * 1024


def _cparams(sem):
    return pltpu.CompilerParams(dimension_semantics=sem, vmem_limit_bytes=VMEM_LIMIT)


def _ln(x):
    mu = jnp.mean(x, axis=-1, keepdims=True)
    xc = x - mu
    var = jnp.mean(xc * xc, axis=-1, keepdims=True)
    return xc * lax.rsqrt(var + LN_EPS)


def _rms(x):
    return x * lax.rsqrt(jnp.mean(x * x, axis=-1, keepdims=True) + RMS_EPS)


def _gelu(x):
    return 0.5 * x * (1.0 + lax.erf(x * (2.0 ** -0.5)))


def _dot(a, b):
    return jnp.dot(a, b, preferred_element_type=F32)


def _dot_nt(a, b):
    return lax.dot_general(a, b, (((1,), (1,)), ((), ())), preferred_element_type=F32)


def _mod_kernel(c_ref, w_ref, b_ref, o_ref):
    c = c_ref[...]
    s = c * jax.nn.sigmoid(c)
    o_ref[...] = jnp.dot(s, w_ref[...], preferred_element_type=F32,
                         precision=lax.Precision.HIGHEST) + b_ref[...]


def _mod(c, w_ada, b_ada):
    bsz = c.shape[0]
    bp = -(-bsz // SUBLANES) * SUBLANES
    cp = jnp.zeros((bp, D_MODEL), F32).at[:bsz].set(c)
    n = w_ada.shape[1]
    tn = 1024
    out = pl.pallas_call(
        _mod_kernel,
        grid=(n // tn,),
        in_specs=[pl.BlockSpec((bp, D_MODEL), lambda j: (0, 0)),
                  pl.BlockSpec((D_MODEL, tn), lambda j: (0, j)),
                  pl.BlockSpec((1, tn), lambda j: (0, j))],
        out_specs=pl.BlockSpec((bp, tn), lambda j: (0, j)),
        out_shape=jax.ShapeDtypeStruct((bp, n), F32),
        compiler_params=_cparams(("arbitrary",)),
        name="mod",
    )(cp, w_ada, b_ada.reshape(1, n))
    return out[:bsz].reshape(bsz, 6, D_MODEL)


def _inproj_kernel(x_ref, mod_ref, w_ref, b_ref, lvg_ref, lvb_ref, qg_ref, wuq_ref, kvg_ref,
                   wuk_ref, wuvt_ref, cos_ref, sin_ref,
                   u_ref, vn_ref, q_ref, k_ref, vt_ref, sga_ref, sgb_ref, *, qscale):
    x = x_ref[...]
    sh1 = mod_ref[0, 0:1, :]
    sc1 = mod_ref[0, 1:2, :]
    h = (_ln(x) * (1.0 + sc1) + sh1).astype(BF16)

    def proj(a, b):
        return _dot(h, w_ref[:, a:b]) + b_ref[:, a:b]

    cos = cos_ref[...]
    sin = sin_ref[...]

    def rope(t):
        return t * cos + pltpu.roll(t, 64, 1) * sin

    u_ref[...] = _gelu(proj(C_U, C_V)).astype(BF16)
    v = _gelu(proj(C_V, C_CQ))
    vn_ref[...] = (_ln(v) * lvg_ref[...] + lvb_ref[...]).astype(BF16)

    cqn = (_rms(proj(C_CQ, C_CKV)) * qg_ref[...]).astype(BF16)
    q = _dot(cqn, wuq_ref[...]) * qscale
    for hh in range(N_HEADS):
        o = hh * HEAD_W
        q_ref[0, hh, :, 0:QK_NOPE] = q[:, o:o + QK_NOPE].astype(BF16)
        q_ref[0, hh, :, QK_NOPE:HEAD_W] = rope(q[:, o + QK_NOPE:o + HEAD_W]).astype(BF16)

    ckvn = (_rms(proj(C_CKV, C_KR)) * kvg_ref[...]).astype(BF16)
    kn = _dot(ckvn, wuk_ref[...])
    kr = rope(proj(C_KR, C_GA)).astype(BF16)
    for hh in range(N_HEADS):
        k_ref[0, hh, :, 0:QK_NOPE] = kn[:, hh * QK_NOPE:(hh + 1) * QK_NOPE].astype(BF16)
        k_ref[0, hh, :, QK_NOPE:HEAD_W] = kr
    vt = _dot_nt(wuvt_ref[...], ckvn)
    for hh in range(N_HEADS):
        vt_ref[0, hh, 0] = vt[hh * V_DIM:(hh + 1) * V_DIM, :].astype(BF16)

    sga_ref[...] = jax.nn.sigmoid(proj(C_GA, C_GB)).astype(BF16)
    sgb_ref[...] = jax.nn.sigmoid(proj(C_GB, C_END)).astype(BF16)


def _inproj(x2, mod3, wp, bsz, seq):
    t = bsz * seq
    tm = min(TM_IN, seq)
    spb = seq // tm
    full = lambda a: pl.BlockSpec(a.shape, lambda i: (0,) * a.ndim)
    qscale = (QK_NOPE + QK_ROPE) ** -0.5 * math.log2(math.e)
    row = pl.BlockSpec((tm, D_MODEL), lambda i: (i, 0))
    qk_spec = pl.BlockSpec((1, N_HEADS, tm, HEAD_W), lambda i: (i // spb, 0, i % spb, 0))
    ins = [x2, mod3, wp["w_in"], wp["b_in"], wp["ln_v_g"], wp["ln_v_b"], wp["q_norm_g"], wp["w_uq"],
           wp["kv_norm_g"], wp["w_uk"], wp["w_uvt"], wp["cos"], wp["sin"]]
    in_specs = [row, pl.BlockSpec((1, 6, D_MODEL), lambda i: (i // spb, 0, 0))]
    in_specs += [full(a) for a in ins[2:11]]
    in_specs += [pl.BlockSpec((tm, LANES), lambda i: (i % spb, 0))] * 2
    return pl.pallas_call(
        functools.partial(_inproj_kernel, qscale=qscale),
        grid=(t // tm,),
        in_specs=in_specs,
        out_specs=[row, row, qk_spec, qk_spec,
                   pl.BlockSpec((1, N_HEADS, 1, V_DIM, tm), lambda i: (i // spb, 0, i % spb, 0, 0)),
                   row, row],
        out_shape=[jax.ShapeDtypeStruct((t, D_MODEL), BF16),
                   jax.ShapeDtypeStruct((t, D_MODEL), BF16),
                   jax.ShapeDtypeStruct((bsz, N_HEADS, seq, HEAD_W), BF16),
                   jax.ShapeDtypeStruct((bsz, N_HEADS, seq, HEAD_W), BF16),
                   jax.ShapeDtypeStruct((bsz, N_HEADS, spb, V_DIM, tm), BF16),
                   jax.ShapeDtypeStruct((t, D_MODEL), BF16),
                   jax.ShapeDtypeStruct((t, D_MODEL), BF16)],
        compiler_params=_cparams(("arbitrary",)),
        name="inproj",
    )(*ins)


def _spatial_kernel(u_ref, vn_ref, ws_ref, bs_ref, woa_ref, sga_ref, o_ref, g_ref):
    tm = u_ref.shape[0]
    nc = tm // CHUNK
    for g in range(GM_GROUPS):
        c0 = g * CHUNK
        rhs = jnp.concatenate([vn_ref[n * CHUNK:(n + 1) * CHUNK, c0:c0 + CHUNK] for n in range(nc)], axis=1)
        mixed = _dot(ws_ref[g], rhs)
        bias = bs_ref[g]
        for n in range(nc):
            m = mixed[:, n * CHUNK:(n + 1) * CHUNK] + bias
            uu = u_ref[n * CHUNK:(n + 1) * CHUNK, c0:c0 + CHUNK].astype(F32)
            g_ref[n * CHUNK:(n + 1) * CHUNK, c0:c0 + CHUNK] = (uu * m).astype(BF16)
    ya = _dot(g_ref[...], woa_ref[...])
    o_ref[...] = (sga_ref[...].astype(F32) * ya).astype(BF16)


def _spatial(u, vn, sga, wp):
    t = u.shape[0]
    tm = TM
    row = pl.BlockSpec((tm, D_MODEL), lambda i: (i, 0))
    full = lambda a: pl.BlockSpec(a.shape, lambda i: (0,) * a.ndim)
    return pl.pallas_call(
        _spatial_kernel,
        grid=(t // tm,),
        in_specs=[row, row, full(wp["w_s"]), full(wp["b_s"]), full(wp["w_oa"]), row],
        out_specs=row,
        out_shape=jax.ShapeDtypeStruct((t, D_MODEL), BF16),
        scratch_shapes=[pltpu.VMEM((tm, GM_WIDTH), BF16)],
        compiler_params=_cparams(("arbitrary",)),
        name="spatial",
    )(u, vn, wp["w_s"], wp["b_s"], wp["w_oa"], sga)


def _attn_kernel(q_ref, k_ref, vt_ref, o_ref, acc_ref, sa_ref, sb_ref, pa_ref, pb_ref, *, nk, tk):
    q = q_ref[0, 0]
    tq = q.shape[0]
    acc_ref[...] = jnp.zeros_like(acc_ref)
    pb_ref[...] = jnp.zeros_like(pb_ref)

    def scores(c):
        return _dot_nt(k_ref[0, 0, pl.ds(pl.multiple_of(c * tk, tk), tk), :], q)

    def softmax(s_ref, p_ref, m, l):
        s = s_ref[...]
        m_new = jnp.maximum(m, jnp.max(s, axis=0, keepdims=True))
        p = jnp.exp2(s - m_new)
        p_ref[...] = p.astype(BF16)
        alpha = jnp.exp2(m - m_new)
        return m_new, alpha * l + jnp.sum(p, axis=0, keepdims=True), alpha

    def accumulate(alpha, c, p_ref):
        return alpha * acc_ref[...] + _dot(vt_ref[0, 0, c], p_ref[...])

    sa_ref[...] = scores(0)

    def body(i, carry):
        m, l, alpha = carry
        c0 = 2 * i
        sb_ref[...] = scores(c0 + 1)
        acc_ref[...] = accumulate(alpha, jnp.maximum(c0 - 1, 0), pb_ref)
        m, l, alpha = softmax(sa_ref, pa_ref, m, l)
        sa_ref[...] = scores(jnp.minimum(c0 + 2, nk - 1))
        acc_ref[...] = accumulate(alpha, c0, pa_ref)
        return softmax(sb_ref, pb_ref, m, l)

    init = (jnp.full((1, tq), -jnp.inf, F32), jnp.zeros((1, tq), F32), jnp.ones((1, tq), F32))
    _, l, alpha = lax.fori_loop(0, nk // 2, body, init)
    o = accumulate(alpha, nk - 1, pb_ref) / l
    o_ref[0] = o.T.astype(BF16)


def _attention(q, k, vt):
    bsz, nh, seq, _ = q.shape
    nk, tk = vt.shape[2], vt.shape[4]
    assert nk % 2 == 0
    tq = min(TQ, seq)
    return pl.pallas_call(
        functools.partial(_attn_kernel, nk=nk, tk=tk),
        grid=(bsz, nh, seq // tq),
        in_specs=[pl.BlockSpec((1, 1, tq, HEAD_W), lambda b, h, i: (b, h, i, 0)),
                  pl.BlockSpec((1, 1, seq, HEAD_W), lambda b, h, i: (b, h, 0, 0)),
                  pl.BlockSpec((1, 1, nk, V_DIM, tk), lambda b, h, i: (b, h, 0, 0, 0))],
        out_specs=pl.BlockSpec((1, tq, V_DIM), lambda b, h, i: (b, i, h)),
        out_shape=jax.ShapeDtypeStruct((bsz, seq, nh * V_DIM), BF16),
        scratch_shapes=[pltpu.VMEM((V_DIM, tq), F32), pltpu.VMEM((tk, tq), F32), pltpu.VMEM((tk, tq), F32),
                        pltpu.VMEM((tk, tq), BF16), pltpu.VMEM((tk, tq), BF16)],
        compiler_params=_cparams(("arbitrary", "arbitrary", "arbitrary")),
        name="attn",
    )(q, k, vt)


def _post_kernel(x_ref, mod_ref, o_ref, ap_ref, sgb_ref, wob_ref, wout_ref, g1_ref, b1_ref, wr_ref,
                 x1_ref, h2_ref, aff_ref, *, alpha):
    yb = _dot(o_ref[...], wob_ref[...])
    mixin = (ap_ref[...].astype(F32) + sgb_ref[...].astype(F32) * yb).astype(BF16)
    mix = _dot(mixin, wout_ref[...])
    gate1 = mod_ref[0, 2:3, :]
    sh2 = mod_ref[0, 3:4, :]
    sc2 = mod_ref[0, 4:5, :]
    x1 = _ln(alpha * x_ref[...] + gate1 * mix) * g1_ref[...] + b1_ref[...]
    x1_ref[...] = x1
    h2 = _ln(x1) * (1.0 + sc2) + sh2
    h2_ref[...] = h2
    logits = jnp.dot(h2, wr_ref[...], preferred_element_type=F32,
                     precision=lax.Precision.HIGHEST)
    lane = lax.broadcasted_iota(I32, logits.shape, 1)
    logits = jnp.where(lane < N_EXPERTS, logits, -jnp.inf)
    ex = jnp.exp(logits - jnp.max(logits, axis=-1, keepdims=True))
    aff = ex / jnp.sum(ex, axis=-1, keepdims=True)
    aff_ref[0] = aff.T[0:N_EXPERTS, :]


def _post(x2, mod3, o, ap, sgb, wp, bsz, seq, alpha):
    t = x2.shape[0]
    tm = TM
    spb = seq // tm
    row = pl.BlockSpec((tm, D_MODEL), lambda i: (i, 0))
    full = lambda a: pl.BlockSpec(a.shape, lambda i: (0,) * a.ndim)
    return pl.pallas_call(
        functools.partial(_post_kernel, alpha=alpha),
        grid=(t // tm,),
        in_specs=[row, pl.BlockSpec((1, 6, D_MODEL), lambda i: (i // spb, 0, 0)), row, row, row,
                  full(wp["w_ob"]), full(wp["w_out"]), full(wp["ln1_g"]), full(wp["ln1_b"]),
                  full(wp["w_router"])],
        out_specs=[row, row, pl.BlockSpec((1, N_EXPERTS, tm), lambda i: (i, 0, 0))],
        out_shape=[jax.ShapeDtypeStruct((t, D_MODEL), F32),
                   jax.ShapeDtypeStruct((t, D_MODEL), F32),
                   jax.ShapeDtypeStruct((t // tm, N_EXPERTS, tm), F32)],
        compiler_params=_cparams(("arbitrary",)),
        name="post",
    )(x2, mod3, o, ap, sgb, wp["w_ob"], wp["w_out"], wp["ln1_g"], wp["ln1_b"], wp["w_router"])


def _route_kernel(aff_ref, spos_ref, flat_ref, cum_ref, tokp_ref, cnt_ref, m_ref, p_ref, *, cap, nb):
    bits = pltpu.bitcast(aff_ref[...], I32)
    tm = bits.shape[2]

    def count(mask):
        c = jnp.sum(jnp.where(mask, 1, 0), axis=0, keepdims=True)
        return jnp.sum(c, axis=2, keepdims=True)

    def bisect(i, thr):
        cand = thr | lax.shift_left(jnp.int32(1), 30 - i)
        return jnp.where(count(bits >= cand) >= cap, cand, thr)

    thr = lax.fori_loop(0, 31, bisect, jnp.zeros((1, N_EXPERTS, 1), I32))
    gt = bits > thr
    eq = bits == thr
    need = cap - count(gt)

    r = lax.broadcasted_iota(I32, (tm, tm), 0)
    c = lax.broadcasted_iota(I32, (tm, tm), 1)
    upper = jnp.where(r <= c, 1.0, 0.0).astype(BF16)
    er = lax.broadcasted_iota(I32, (N_EXPERTS, N_EXPERTS), 0)
    ec = lax.broadcasted_iota(I32, (N_EXPERTS, N_EXPERTS), 1)
    below = jnp.where(ec < er, 1.0, 0.0).astype(F32)

    def prefix(write_cum):
        def body(j, carry):
            blk = m_ref[j]
            incl = _dot(blk.astype(BF16), upper)
            p_ref[j] = incl - blk + carry
            if write_cum:
                cum_ref[j] = jnp.broadcast_to(carry, (N_EXPERTS, LANES)).astype(I32)
            return carry + jnp.sum(blk, axis=1, keepdims=True)
        return lax.fori_loop(0, nb, body, jnp.zeros((N_EXPERTS, 1), F32))

    m_ref[...] = jnp.where(eq, 1.0, 0.0)
    prefix(False)
    sel = gt | (eq & (p_ref[...] < need.astype(F32)))
    self32 = jnp.where(sel, 1.0, 0.0)
    m_ref[...] = self32
    total = prefix(True)
    cum_ref[nb] = jnp.broadcast_to(total, (N_EXPERTS, LANES)).astype(I32)
    spos_ref[...] = jnp.where(sel, p_ref[...].astype(I32), -1)

    cnt = jnp.sum(self32, axis=1, keepdims=True)
    cnt_ref[...] = jnp.broadcast_to(cnt, cnt_ref.shape).astype(I32)
    m_ref[...] = jnp.broadcast_to(cnt, m_ref.shape)
    prefix(False)
    tokp_ref[...] = p_ref[...].astype(I32)

    def within(j, _):
        blk = jnp.where(spos_ref[j] >= 0, 1.0, 0.0)
        w = jnp.dot(below, blk, preferred_element_type=F32)
        flat_ref[j] = (p_ref[j] + w).astype(I32)
        return 0
    lax.fori_loop(0, nb, within, 0)


def _route(aff3, cap):
    nb, ne, tm = aff3.shape
    shp = jax.ShapeDtypeStruct((nb, ne, tm), I32)
    return pl.pallas_call(
        functools.partial(_route_kernel, cap=cap, nb=nb),
        out_shape=[shp, shp, jax.ShapeDtypeStruct((nb + 1, ne, LANES), I32), shp, shp],
        scratch_shapes=[pltpu.VMEM((nb, ne, tm), F32), pltpu.VMEM((nb, ne, tm), F32)],
        compiler_params=pltpu.CompilerParams(vmem_limit_bytes=VMEM_LIMIT),
        name="route",
    )(aff3)


def _slots_kernel(blo_ref, bhi_ref, spos_ref, aff_ref, flat_ref, idx_ref, gate_ref, dst_ref, acc_ref, *, tm, nt):
    e = pl.program_id(0)
    j = pl.program_id(1)
    slot = j * SLOT_TILE + lax.broadcasted_iota(I32, (SLOT_TILE, 1), 0)
    lane = lax.broadcasted_iota(I32, (1, tm), 1).astype(F32)
    zero = jnp.zeros((1, tm), F32)
    acc_ref[...] = jnp.zeros_like(acc_ref)

    def body(b, _):
        hit = slot == spos_ref[b, pl.ds(e, 1), :]
        onehot = jnp.where(hit, 1.0, 0.0).astype(BF16)
        g = aff_ref[b, pl.ds(e, 1), :]
        g0 = g.astype(BF16).astype(F32)
        g1 = (g - g0).astype(BF16).astype(F32)
        g2 = g - g0 - g1
        fl = flat_ref[b, pl.ds(e, 1), :]
        rows = [lane, (jnp.zeros((1, tm), I32) + b).astype(F32), g0, g1, g2,
                lax.shift_right_logical(fl, 8).astype(F32), (fl & 255).astype(F32), zero]
        vals = jnp.concatenate(rows + [zero] * 8, axis=0).astype(BF16)
        acc_ref[...] += _dot_nt(onehot, vals)
        return 0

    lax.fori_loop(blo_ref[e * nt + j], bhi_ref[e * nt + j], body, 0)
    a = acc_ref[...]
    idx_ref[...] = (a[:, 1:2] * tm + a[:, 0:1]).astype(I32)
    gate_ref[...] = a[:, 2:3] + a[:, 3:4] + a[:, 4:5]
    dst_ref[...] = (a[:, 5:6] * 256.0 + a[:, 6:7]).astype(I32)


def _slots(cum, spos, aff3, flat, cap):
    nb, ne, tm = spos.shape
    nt = cap // SLOT_TILE
    edges = jnp.arange(nt + 1, dtype=I32) * SLOT_TILE
    cs, ce = cum[:-1].T, cum[1:].T
    blo = jnp.sum(ce[:, None, :] <= edges[None, :-1, None], axis=2).astype(I32).reshape(-1)
    bhi = jnp.sum(cs[:, None, :] < edges[None, 1:, None], axis=2).astype(I32).reshape(-1)
    full = lambda a: pl.BlockSpec(a.shape, lambda e, j, lo, hi: (0,) * a.ndim)
    col = pl.BlockSpec((SLOT_TILE, 1), lambda e, j, lo, hi: (e * nt + j, 0))
    return pl.pallas_call(
        functools.partial(_slots_kernel, tm=tm, nt=nt),
        grid_spec=pltpu.PrefetchScalarGridSpec(
            num_scalar_prefetch=2,
            grid=(ne, nt),
            in_specs=[full(spos), full(aff3), full(flat)],
            out_specs=[col, col, col],
            scratch_shapes=[pltpu.VMEM((SLOT_TILE, 16), F32)]),
        out_shape=[jax.ShapeDtypeStruct((ne * cap, 1), I32), jax.ShapeDtypeStruct((ne * cap, 1), F32),
                   jax.ShapeDtypeStruct((ne * cap, 1), I32)],
        compiler_params=_cparams(("arbitrary", "arbitrary")),
        name="slots",
    )(blo, bhi, spos, aff3, flat)


def _moe_kernel(idx_ref, idxn_ref, dst_ref, gate_ref, w1_ref, w3_ref, w2_ref, h2_hbm, y_hbm,
                rin_ref, rout_ref, xe_ref, acc_ref, sem, *, ch, nf, mch):
    f = pl.program_id(2)
    blk = pl.program_id(0) * pl.num_programs(1) + pl.program_id(1)
    nblk = pl.num_programs(0) * pl.num_programs(1)
    par = blk & 1
    nm = ch // mch
    per_chunk = ch // (nf * nm)
    rows = ch * ROW_TILES

    def gather_row(ids_ref, s, buf):
        t = ids_ref[0, 0, s]
        return pltpu.make_async_copy(h2_hbm.at[pl.ds(pl.multiple_of(t * ROW_TILES, ROW_TILES), ROW_TILES), :],
                                     rin_ref.at[buf, pl.ds(pl.multiple_of(s * ROW_TILES, ROW_TILES), ROW_TILES), :],
                                     sem.at[buf])

    def gather_all(buf):
        return pltpu.make_async_copy(h2_hbm.at[pl.ds(0, rows), :], rin_ref.at[buf], sem.at[buf])

    def scatter_row(s):
        d = dst_ref[0, 0, s]
        return pltpu.make_async_copy(rout_ref.at[pl.ds(pl.multiple_of(s * ROW_TILES, ROW_TILES), ROW_TILES), :],
                                     y_hbm.at[pl.ds(pl.multiple_of(d * ROW_TILES, ROW_TILES), ROW_TILES), :],
                                     sem.at[2])

    def scatter_all():
        return pltpu.make_async_copy(rout_ref, y_hbm.at[pl.ds(0, rows), :], sem.at[2])

    @pl.when((blk == 0) & (f == 0))
    def _():
        def start(s, _):
            gather_row(idx_ref, s, 0).start()
            return 0
        lax.fori_loop(0, ch, start, 0)

    @pl.when(f == 0)
    def _():
        gather_all(par).wait()
        for mc in range(nm):
            base = mc * mch * ROW_TILES
            parts = [rin_ref[par, pl.ds(base + kk, mch, stride=ROW_TILES), :] for kk in range(ROW_TILES)]
            xe_ref[mc * mch:(mc + 1) * mch, :] = jnp.concatenate(parts, axis=1).astype(BF16)

    w1 = w1_ref[0].astype(BF16)
    w3 = w3_ref[0].astype(BF16)
    w2 = w2_ref[0].astype(BF16)
    for mc in range(nm):
        sl = slice(mc * mch, (mc + 1) * mch)
        xm = xe_ref[sl, :]
        a = _dot(xm, w1)
        hid = (a * jax.nn.sigmoid(a) * _dot(xm, w3)).astype(BF16)
        part = _dot(hid, w2)
        first = (f * nm + mc) * per_chunk
        for i in range(per_chunk):
            gather_row(idxn_ref, first + i, 1 - par).start()

        @pl.when(f == 0)
        def _():
            acc_ref[sl, :] = part

        @pl.when(f > 0)
        def _():
            acc_ref[sl, :] += part

    @pl.when(f == nf - 1)
    def _():
        @pl.when(blk > 0)
        def _():
            scatter_all().wait()
        for mc in range(nm):
            sl = slice(mc * mch, (mc + 1) * mch)
            ye = acc_ref[sl, :] * gate_ref[sl, :]
            base = mc * mch * ROW_TILES
            for kk in range(ROW_TILES):
                rout_ref[pl.ds(base + kk, mch, stride=ROW_TILES), :] = ye[:, kk * LANES:(kk + 1) * LANES]

        def start(s, _):
            scatter_row(s).start()
            return 0
        lax.fori_loop(0, ch, start, 0)

        @pl.when(blk == nblk - 1)
        def _():
            scatter_all().wait()
            gather_all(1 - par).wait()


def _moe(idx, dst, gate, h2rows, w1, w3, w2, cap):
    ne = N_EXPERTS
    ch = min(MOE_ROWS, cap)
    nh = cap // ch
    nf = D_FF // FF_TILE
    mch = min(M_CHUNK, ch)
    assert ch % (nf * (ch // mch)) == 0
    smem = lambda: pl.BlockSpec((1, 1, ch), lambda e, h, f: (e * nh + h, 0, 0), memory_space=pltpu.SMEM)
    smem_next = pl.BlockSpec((1, 1, ch), lambda e, h, f: (jnp.minimum(e * nh + h + 1, ne * nh - 1), 0, 0),
                             memory_space=pltpu.SMEM)
    return pl.pallas_call(
        functools.partial(_moe_kernel, ch=ch, nf=nf, mch=mch),
        grid=(ne, nh, nf),
        in_specs=[smem(), smem_next, smem(),
                  pl.BlockSpec((ch, 1), lambda e, h, f: (e * nh + h, 0)),
                  pl.BlockSpec((1, D_MODEL, FF_TILE), lambda e, h, f: (e, 0, f)),
                  pl.BlockSpec((1, D_MODEL, FF_TILE), lambda e, h, f: (e, 0, f)),
                  pl.BlockSpec((1, FF_TILE, D_MODEL), lambda e, h, f: (e, f, 0)),
                  pl.BlockSpec(memory_space=pl.ANY)],
        out_specs=pl.BlockSpec(memory_space=pl.ANY),
        out_shape=jax.ShapeDtypeStruct((ne * cap * ROW_TILES, LANES), F32),
        scratch_shapes=[pltpu.VMEM((2, ch * ROW_TILES, LANES), F32),
                        pltpu.VMEM((ch * ROW_TILES, LANES), F32),
                        pltpu.VMEM((ch, D_MODEL), BF16),
                        pltpu.VMEM((ch, D_MODEL), F32),
                        pltpu.SemaphoreType.DMA((3,))],
        compiler_params=_cparams(("arbitrary", "arbitrary", "arbitrary")),
        name="moe",
    )(idx.reshape(ne * nh, 1, ch), idx.reshape(ne * nh, 1, ch), dst.reshape(ne * nh, 1, ch), gate, w1, w3, w2,
      h2rows)


def _combine_kernel(pb_ref, pc_ref, pv_ref, y_ref, tokp_ref, cnt_ref, x1_ref, mod_ref, g2_ref, b2_ref,
                    o_ref, acc_ref, *, alpha):
    k = pl.program_id(0)
    blk = pb_ref[k]
    prev = pb_ref[jnp.maximum(k - 1, 0)]
    nxt = pb_ref[jnp.minimum(k + 1, pl.num_programs(0) - 1)]
    valid = pv_ref[k] == 1
    first = valid & ((k == 0) | (prev != blk))
    last = valid & ((k == pl.num_programs(0) - 1) | (nxt != blk) | (pv_ref[jnp.minimum(k + 1, pl.num_programs(0) - 1)] == 0))

    @pl.when(first)
    def _():
        acc_ref[...] = jnp.zeros_like(acc_ref)

    @pl.when(valid)
    def _():
        parts = [y_ref[pl.ds(kk, SLOT_TILE, stride=ROW_TILES), :] for kk in range(ROW_TILES)]
        rows = jnp.concatenate(parts, axis=1)
        hi = rows.astype(BF16)
        lo = (rows - hi.astype(F32)).astype(BF16)
        r = pc_ref[k] * SLOT_TILE + lax.broadcasted_iota(I32, (1, SLOT_TILE), 1)
        start = tokp_ref[...]
        own = jnp.where((r >= start) & (r < start + cnt_ref[...]), 1.0, 0.0).astype(BF16)
        acc_ref[...] += _dot(own, hi) + _dot(own, lo)

    @pl.when(last)
    def _():
        gate2 = mod_ref[0, 5:6, :]
        o_ref[...] = _ln(alpha * x1_ref[...] + gate2 * acc_ref[...]) * g2_ref[...] + b2_ref[...]


def _combine(pb, pc, pv, yrows, tokp_col, cnt_col, x1, mod3, wp, seq, alpha):
    t = x1.shape[0]
    tm = TM
    spb = seq // tm
    npairs = pb.shape[0]
    full = lambda a: pl.BlockSpec(a.shape, lambda k, pb, pc, pv: (0,) * a.ndim)
    row = pl.BlockSpec((tm, D_MODEL), lambda k, pb, pc, pv: (pb[k], 0))
    colspec = pl.BlockSpec((tm, 1), lambda k, pb, pc, pv: (pb[k], 0))
    return pl.pallas_call(
        functools.partial(_combine_kernel, alpha=alpha),
        grid_spec=pltpu.PrefetchScalarGridSpec(
            num_scalar_prefetch=3,
            grid=(npairs,),
            in_specs=[pl.BlockSpec((SLOT_TILE * ROW_TILES, LANES), lambda k, pb, pc, pv: (pc[k], 0)),
                      colspec, colspec, row,
                      pl.BlockSpec((1, 6, D_MODEL), lambda k, pb, pc, pv: (pb[k] // spb, 0, 0)),
                      full(wp["ln2_g"]), full(wp["ln2_b"])],
            out_specs=row,
            scratch_shapes=[pltpu.VMEM((tm, D_MODEL), F32)]),
        out_shape=jax.ShapeDtypeStruct((t, D_MODEL), F32),
        compiler_params=_cparams(("arbitrary",)),
        name="combine",
    )(pb, pc, pv, yrows, tokp_col, cnt_col, x1, mod3, wp["ln2_g"], wp["ln2_b"])


def _pair_schedule(row_start, n_rows_total, nb):
    nchunks = n_rows_total // SLOT_TILE
    npairs = nb + nchunks
    rs = row_start
    re = jnp.concatenate([rs[1:], jnp.array([n_rows_total], I32)])
    c_lo = jnp.minimum(rs // SLOT_TILE, nchunks - 1)
    c_hi = jnp.maximum(c_lo, (re - 1) // SLOT_TILE)
    n_b = c_hi - c_lo + 1
    ends = jnp.cumsum(n_b)
    starts = ends - n_b
    k = jnp.arange(npairs, dtype=I32)
    valid = k < ends[-1]
    kk = jnp.minimum(k, ends[-1] - 1)
    b = jnp.sum(ends[None, :] <= kk[:, None], axis=1).astype(I32)
    c = c_lo[b] + (kk - starts[b])
    return b.astype(I32), c.astype(I32), valid.astype(I32)


def _pack_weights(l, seq, w_in, b_in, ln_v_g, ln_v_b, w_spatial, b_spatial, w_oa, q_norm_g, w_uq, kv_norm_g,
                  w_ukv, w_ob, w_out, ln1_g, ln1_b, w_router, ln2_g, ln2_b):
    half = QK_ROPE // 2
    z32 = lambda rows: jnp.zeros((rows, half), F32)

    def rope_cols(w):
        return jnp.concatenate([w[:, :half], z32(w.shape[0]), w[:, half:], z32(w.shape[0])], axis=1)

    wi, bi = w_in[l], b_in[l][None, :]
    off_cq, off_ckv, off_kr = 2 * GM_WIDTH, 2 * GM_WIDTH + Q_LORA, 2 * GM_WIDTH + Q_LORA + KV_LORA
    off_ga = off_kr + QK_ROPE

    def repack(a):
        return jnp.concatenate([a[:, :off_kr], rope_cols(a[:, off_kr:off_ga]), a[:, off_ga:]], axis=1)

    wq = w_uq[l].reshape(Q_LORA, N_HEADS, QK_NOPE + QK_ROPE)
    wq = jnp.concatenate([wq[:, :, :QK_NOPE],
                          wq[:, :, QK_NOPE:QK_NOPE + half], jnp.zeros((Q_LORA, N_HEADS, half), F32),
                          wq[:, :, QK_NOPE + half:], jnp.zeros((Q_LORA, N_HEADS, half), F32)], axis=2)
    wkv = w_ukv[l].reshape(KV_LORA, N_HEADS, QK_NOPE + V_DIM)
    inv = ROPE_BASE ** (-jnp.arange(half, dtype=F32) / half)
    ang = jnp.arange(seq, dtype=F32)[:, None] * inv[None, :]
    cos, sin, zs = jnp.cos(ang), jnp.sin(ang), jnp.zeros((seq, half), F32)
    wr = jnp.zeros((D_MODEL, LANES), F32).at[:, :N_EXPERTS].set(w_router[l])
    return {
        "w_in": repack(wi).astype(BF16), "b_in": repack(bi),
        "ln_v_g": ln_v_g[l][None, :], "ln_v_b": ln_v_b[l][None, :],
        "w_s": w_spatial[l].astype(BF16),
        "b_s": jnp.broadcast_to(b_spatial[l][:, :, None], (GM_GROUPS, CHUNK, CHUNK)),
        "w_oa": w_oa[l].astype(BF16),
        "q_norm_g": q_norm_g[l][None, :], "w_uq": wq.reshape(Q_LORA, N_HEADS * HEAD_W).astype(BF16),
        "kv_norm_g": kv_norm_g[l][None, :],
        "w_uk": wkv[:, :, :QK_NOPE].reshape(KV_LORA, N_HEADS * QK_NOPE).astype(BF16),
        "w_uvt": wkv[:, :, QK_NOPE:].reshape(KV_LORA, N_HEADS * V_DIM).T.astype(BF16),
        "cos": jnp.concatenate([cos, zs, cos, zs], axis=1), "sin": jnp.concatenate([-sin, zs, sin, zs], axis=1),
        "w_ob": w_ob[l].astype(BF16), "w_out": w_out[l].astype(BF16),
        "ln1_g": ln1_g[l][None, :], "ln1_b": ln1_b[l][None, :], "w_router": wr,
        "ln2_g": ln2_g[l][None, :], "ln2_b": ln2_b[l][None, :],
    }


def _layer(x, c, l, alpha, w_ada, b_ada, w1, w3, w2, packed):
    bsz, seq, _ = x.shape
    t = bsz * seq
    cap = EC_CAPACITY_FACTOR * t // N_EXPERTS
    x2 = x.reshape(t, D_MODEL)
    mod3 = _mod(c, w_ada[l], b_ada[l])
    u, vn, q, k, vt, sga, sgb = _inproj(x2, mod3, packed, bsz, seq)
    ap = _spatial(u, vn, sga, packed)
    o = _attention(q, k, vt).reshape(t, D_MODEL)
    x1, h2, aff3 = _post(x2, mod3, o, ap, sgb, packed, bsz, seq, alpha)
    nb = t // TM
    spos, flat, cum, tokp, cnt = _route(aff3, cap)
    idx, gate, dst = _slots(cum[:, :, 0], spos, aff3, flat, cap)
    yrows = _moe(idx, dst, gate, h2.reshape(t * ROW_TILES, LANES), w1[l], w3[l], w2[l], cap)
    tokp_col = tokp[:, 0, :].reshape(t, 1)
    cnt_col = cnt[:, 0, :].reshape(t, 1)
    pb, pc, pv = _pair_schedule(tokp[:, 0, 0], N_EXPERTS * cap, nb)
    y = _combine(pb, pc, pv, yrows, tokp_col, cnt_col, x1, mod3, packed, seq, alpha)
    return y.reshape(bsz, seq, D_MODEL)


def kernel(x_prompt, x_sample, c_prompt, c_sample, w_ada, b_ada, w_in, b_in, ln_v_g, ln_v_b, w_spatial, b_spatial,
           w_oa, q_norm_g, w_uq, kv_norm_g, w_ukv, w_ob, w_out, ln1_g, ln1_b, w_router, w1, w3, w2, ln2_g, ln2_b):
    depth = w_ada.shape[0]
    alpha = (2.0 * depth) ** 0.25
    outs = []
    for x, c in ((x_prompt, c_prompt), (x_sample, c_sample)):
        for l in range(depth):
            packed = _pack_weights(l, x.shape[1], w_in, b_in, ln_v_g, ln_v_b, w_spatial, b_spatial, w_oa,
                                   q_norm_g, w_uq, kv_norm_g, w_ukv, w_ob, w_out, ln1_g, ln1_b, w_router,
                                   ln2_g, ln2_b)
            x = _layer(x, c, l, alpha, w_ada, b_ada, w1, w3, w2, packed)
        outs.append(x)
    return tuple(outs)
```

```python
import functools
import math

import jax
import jax.numpy as jnp
from jax import lax
from jax.experimental import pallas as pl
from jax.experimental.pallas import tpu as pltpu

F32 = jnp.float32
BF16 = jnp.bfloat16
I32 = jnp.int32

D_MODEL = 1024
GM_WIDTH = 1024
GM_GROUPS = 8
CHUNK = 128
N_HEADS = 8
QK_NOPE = 128
QK_ROPE = 64
V_DIM = 128
Q_LORA = 384
KV_LORA = 256
ROPE_BASE = 10000.0
N_EXPERTS = 16
EC_CAPACITY_FACTOR = 2
D_FF = 2048
LN_EPS = 1e-5
RMS_EPS = 1e-6

LANES = 128
SUBLANES = 8
HEAD_W = 256
ROW_TILES = D_MODEL // LANES

C_U, C_V, C_CQ, C_CKV, C_KR, C_GA, C_GB, C_END = 0, 1024, 2048, 2432, 2688, 2816, 3840, 4864

TM_IN = 512
TQ = 2048
TM = 256
SLOT_TILE = 256
FF_TILE = 512
M_CHUNK = 512
MOE_ROWS = 1024
VMEM_LIMIT = 56 * 1024 * 1024


def _cparams(sem):
    return pltpu.CompilerParams(dimension_semantics=sem, vmem_limit_bytes=VMEM_LIMIT)


def _ln(x):
    mu = jnp.mean(x, axis=-1, keepdims=True)
    xc = x - mu
    var = jnp.mean(xc * xc, axis=-1, keepdims=True)
    return xc * lax.rsqrt(var + LN_EPS)


def _rms(x):
    return x * lax.rsqrt(jnp.mean(x * x, axis=-1, keepdims=True) + RMS_EPS)


def _gelu(x):
    return 0.5 * x * (1.0 + lax.erf(x * (2.0 ** -0.5)))


def _dot(a, b):
    return jnp.dot(a, b, preferred_element_type=F32)


def _dot_nt(a, b):
    return lax.dot_general(a, b, (((1,), (1,)), ((), ())), preferred_element_type=F32)


def _mod_kernel(c_ref, w_ref, b_ref, o_ref):
    c = c_ref[...]
    s = c * jax.nn.sigmoid(c)
    o_ref[...] = jnp.dot(s, w_ref[...], preferred_element_type=F32,
                         precision=lax.Precision.HIGHEST) + b_ref[...]


def _mod(c, w_ada, b_ada):
    bsz = c.shape[0]
    bp = -(-bsz // SUBLANES) * SUBLANES
    cp = jnp.zeros((bp, D_MODEL), F32).at[:bsz].set(c)
    n = w_ada.shape[1]
    tn = 1024
    out = pl.pallas_call(
        _mod_kernel,
        grid=(n // tn,),
        in_specs=[pl.BlockSpec((bp, D_MODEL), lambda j: (0, 0)),
                  pl.BlockSpec((D_MODEL, tn), lambda j: (0, j)),
                  pl.BlockSpec((1, tn), lambda j: (0, j))],
        out_specs=pl.BlockSpec((bp, tn), lambda j: (0, j)),
        out_shape=jax.ShapeDtypeStruct((bp, n), F32),
        compiler_params=_cparams(("arbitrary",)),
        name="mod",
    )(cp, w_ada, b_ada.reshape(1, n))
    return out[:bsz].reshape(bsz, 6, D_MODEL)


def _inproj_kernel(x_ref, mod_ref, w_ref, b_ref, lvg_ref, lvb_ref, qg_ref, wuq_ref, kvg_ref,
                   wuk_ref, wuvt_ref, cos_ref, sin_ref,
                   u_ref, vn_ref, q_ref, k_ref, vt_ref, sga_ref, sgb_ref, *, qscale):
    x = x_ref[...]
    sh1 = mod_ref[0, 0:1, :]
    sc1 = mod_ref[0, 1:2, :]
    h = (_ln(x) * (1.0 + sc1) + sh1).astype(BF16)

    def proj(a, b):
        return _dot(h, w_ref[:, a:b]) + b_ref[:, a:b]

    cos = cos_ref[...]
    sin = sin_ref[...]

    def rope(t):
        return t * cos + pltpu.roll(t, 64, 1) * sin

    u_ref[...] = _gelu(proj(C_U, C_V)).astype(BF16)
    v = _gelu(proj(C_V, C_CQ))
    vn_ref[...] = (_ln(v) * lvg_ref[...] + lvb_ref[...]).astype(BF16)

    cqn = (_rms(proj(C_CQ, C_CKV)) * qg_ref[...]).astype(BF16)
    q = _dot(cqn, wuq_ref[...]) * qscale
    for hh in range(N_HEADS):
        o = hh * HEAD_W
        q_ref[0, hh, :, 0:QK_NOPE] = q[:, o:o + QK_NOPE].astype(BF16)
        q_ref[0, hh, :, QK_NOPE:HEAD_W] = rope(q[:, o + QK_NOPE:o + HEAD_W]).astype(BF16)

    ckvn = (_rms(proj(C_CKV, C_KR)) * kvg_ref[...]).astype(BF16)
    kn = _dot(ckvn, wuk_ref[...])
    kr = rope(proj(C_KR, C_GA)).astype(BF16)
    for hh in range(N_HEADS):
        k_ref[0, hh, :, 0:QK_NOPE] = kn[:, hh * QK_NOPE:(hh + 1) * QK_NOPE].astype(BF16)
        k_ref[0, hh, :, QK_NOPE:HEAD_W] = kr
    vt = _dot_nt(wuvt_ref[...], ckvn)
    for hh in range(N_HEADS):
        vt_ref[0, hh, 0] = vt[hh * V_DIM:(hh + 1) * V_DIM, :].astype(BF16)

    sga_ref[...] = jax.nn.sigmoid(proj(C_GA, C_GB)).astype(BF16)
    sgb_ref[...] = jax.nn.sigmoid(proj(C_GB, C_END)).astype(BF16)


def _inproj(x2, mod3, wp, bsz, seq):
    t = bsz * seq
    tm = min(TM_IN, seq)
    spb = seq // tm
    full = lambda a: pl.BlockSpec(a.shape, lambda i: (0,) * a.ndim)
    qscale = (QK_NOPE + QK_ROPE) ** -0.5 * math.log2(math.e)
    row = pl.BlockSpec((tm, D_MODEL), lambda i: (i, 0))
    qk_spec = pl.BlockSpec((1, N_HEADS, tm, HEAD_W), lambda i: (i // spb, 0, i % spb, 0))
    ins = [x2, mod3, wp["w_in"], wp["b_in"], wp["ln_v_g"], wp["ln_v_b"], wp["q_norm_g"], wp["w_uq"],
           wp["kv_norm_g"], wp["w_uk"], wp["w_uvt"], wp["cos"], wp["sin"]]
    in_specs = [row, pl.BlockSpec((1, 6, D_MODEL), lambda i: (i // spb, 0, 0))]
    in_specs += [full(a) for a in ins[2:11]]
    in_specs += [pl.BlockSpec((tm, LANES), lambda i: (i % spb, 0))] * 2
    return pl.pallas_call(
        functools.partial(_inproj_kernel, qscale=qscale),
        grid=(t // tm,),
        in_specs=in_specs,
        out_specs=[row, row, qk_spec, qk_spec,
                   pl.BlockSpec((1, N_HEADS, 1, V_DIM, tm), lambda i: (i // spb, 0, i % spb, 0, 0)),
                   row, row],
        out_shape=[jax.ShapeDtypeStruct((t, D_MODEL), BF16),
                   jax.ShapeDtypeStruct((t, D_MODEL), BF16),
                   jax.ShapeDtypeStruct((bsz, N_HEADS, seq, HEAD_W), BF16),
                   jax.ShapeDtypeStruct((bsz, N_HEADS, seq, HEAD_W), BF16),
                   jax.ShapeDtypeStruct((bsz, N_HEADS, spb, V_DIM, tm), BF16),
                   jax.ShapeDtypeStruct((t, D_MODEL), BF16),
                   jax.ShapeDtypeStruct((t, D_MODEL), BF16)],
        compiler_params=_cparams(("arbitrary",)),
        name="inproj",
    )(*ins)


def _spatial_kernel(u_ref, vn_ref, ws_ref, bs_ref, woa_ref, sga_ref, o_ref, g_ref):
    tm = u_ref.shape[0]
    nc = tm // CHUNK
    for g in range(GM_GROUPS):
        c0 = g * CHUNK
        rhs = jnp.concatenate([vn_ref[n * CHUNK:(n + 1) * CHUNK, c0:c0 + CHUNK] for n in range(nc)], axis=1)
        mixed = _dot(ws_ref[g], rhs)
        bias = bs_ref[g]
        for n in range(nc):
            m = mixed[:, n * CHUNK:(n + 1) * CHUNK] + bias
            uu = u_ref[n * CHUNK:(n + 1) * CHUNK, c0:c0 + CHUNK].astype(F32)
            g_ref[n * CHUNK:(n + 1) * CHUNK, c0:c0 + CHUNK] = (uu * m).astype(BF16)
    ya = _dot(g_ref[...], woa_ref[...])
    o_ref[...] = (sga_ref[...].astype(F32) * ya).astype(BF16)


def _spatial(u, vn, sga, wp):
    t = u.shape[0]
    tm = TM
    row = pl.BlockSpec((tm, D_MODEL), lambda i: (i, 0))
    full = lambda a: pl.BlockSpec(a.shape, lambda i: (0,) * a.ndim)
    return pl.pallas_call(
        _spatial_kernel,
        grid=(t // tm,),
        in_specs=[row, row, full(wp["w_s"]), full(wp["b_s"]), full(wp["w_oa"]), row],
        out_specs=row,
        out_shape=jax.ShapeDtypeStruct((t, D_MODEL), BF16),
        scratch_shapes=[pltpu.VMEM((tm, GM_WIDTH), BF16)],
        compiler_params=_cparams(("arbitrary",)),
        name="spatial",
    )(u, vn, wp["w_s"], wp["b_s"], wp["w_oa"], sga)


def _attn_kernel(q_ref, k_ref, vt_ref, o_ref, acc_ref, sa_ref, sb_ref, pa_ref, pb_ref, *, nk, tk):
    q = q_ref[0, 0]
    tq = q.shape[0]
    acc_ref[...] = jnp.zeros_like(acc_ref)
    pb_ref[...] = jnp.zeros_like(pb_ref)

    def scores(c):
        return _dot_nt(k_ref[0, 0, pl.ds(pl.multiple_of(c * tk, tk), tk), :], q)

    def softmax(s_ref, p_ref, m, l):
        s = s_ref[...]
        m_new = jnp.maximum(m, jnp.max(s, axis=0, keepdims=True))
        p = jnp.exp2(s - m_new)
        p_ref[...] = p.astype(BF16)
        alpha = jnp.exp2(m - m_new)
        return m_new, alpha * l + jnp.sum(p, axis=0, keepdims=True), alpha

    def accumulate(alpha, c, p_ref):
        return alpha * acc_ref[...] + _dot(vt_ref[0, 0, c], p_ref[...])

    sa_ref[...] = scores(0)

    def body(i, carry):
        m, l, alpha = carry
        c0 = 2 * i
        sb_ref[...] = scores(c0 + 1)
        acc_ref[...] = accumulate(alpha, jnp.maximum(c0 - 1, 0), pb_ref)
        m, l, alpha = softmax(sa_ref, pa_ref, m, l)
        sa_ref[...] = scores(jnp.minimum(c0 + 2, nk - 1))
        acc_ref[...] = accumulate(alpha, c0, pa_ref)
        return softmax(sb_ref, pb_ref, m, l)

    init = (jnp.full((1, tq), -jnp.inf, F32), jnp.zeros((1, tq), F32), jnp.ones((1, tq), F32))
    _, l, alpha = lax.fori_loop(0, nk // 2, body, init)
    o = accumulate(alpha, nk - 1, pb_ref) / l
    o_ref[0] = o.T.astype(BF16)


def _attention(q, k, vt):
    bsz, nh, seq, _ = q.shape
    nk, tk = vt.shape[2], vt.shape[4]
    assert nk % 2 == 0
    tq = min(TQ, seq)
    return pl.pallas_call(
        functools.partial(_attn_kernel, nk=nk, tk=tk),
        grid=(bsz, nh, seq // tq),
        in_specs=[pl.BlockSpec((1, 1, tq, HEAD_W), lambda b, h, i: (b, h, i, 0)),
                  pl.BlockSpec((1, 1, seq, HEAD_W), lambda b, h, i: (b, h, 0, 0)),
                  pl.BlockSpec((1, 1, nk, V_DIM, tk), lambda b, h, i: (b, h, 0, 0, 0))],
        out_specs=pl.BlockSpec((1, tq, V_DIM), lambda b, h, i: (b, i, h)),
        out_shape=jax.ShapeDtypeStruct((bsz, seq, nh * V_DIM), BF16),
        scratch_shapes=[pltpu.VMEM((V_DIM, tq), F32), pltpu.VMEM((tk, tq), F32), pltpu.VMEM((tk, tq), F32),
                        pltpu.VMEM((tk, tq), BF16), pltpu.VMEM((tk, tq), BF16)],
        compiler_params=_cparams(("arbitrary", "arbitrary", "arbitrary")),
        name="attn",
    )(q, k, vt)


def _post_kernel(x_ref, mod_ref, o_ref, ap_ref, sgb_ref, wob_ref, wout_ref, g1_ref, b1_ref, wr_ref,
                 x1_ref, h2_ref, aff_ref, *, alpha):
    yb = _dot(o_ref[...], wob_ref[...])
    mixin = (ap_ref[...].astype(F32) + sgb_ref[...].astype(F32) * yb).astype(BF16)
    mix = _dot(mixin, wout_ref[...])
    gate1 = mod_ref[0, 2:3, :]
    sh2 = mod_ref[0, 3:4, :]
    sc2 = mod_ref[0, 4:5, :]
    x1 = _ln(alpha * x_ref[...] + gate1 * mix) * g1_ref[...] + b1_ref[...]
    x1_ref[...] = x1
    h2 = _ln(x1) * (1.0 + sc2) + sh2
    tm = h2.shape[0]
    for kk in range(ROW_TILES):
        h2_ref[pl.ds(kk, tm, stride=ROW_TILES), :] = h2[:, kk * LANES:(kk + 1) * LANES]
    h_hi = h2.astype(BF16)
    h_lo = (h2 - h_hi.astype(F32)).astype(BF16)
    prod = _dot(h_hi, wr_ref[...]) + _dot(h_lo, wr_ref[...])
    logits = prod[:, 0:LANES] + prod[:, LANES:2 * LANES]
    lane = lax.broadcasted_iota(I32, logits.shape, 1)
    logits = jnp.where(lane < N_EXPERTS, logits, -jnp.inf)
    ex = jnp.exp(logits - jnp.max(logits, axis=-1, keepdims=True))
    aff = ex / jnp.sum(ex, axis=-1, keepdims=True)
    aff_ref[0] = aff.T[0:N_EXPERTS, :]


def _post(x2, mod3, o, ap, sgb, wp, bsz, seq, alpha):
    t = x2.shape[0]
    tm = TM
    spb = seq // tm
    row = pl.BlockSpec((tm, D_MODEL), lambda i: (i, 0))
    full = lambda a: pl.BlockSpec(a.shape, lambda i: (0,) * a.ndim)
    return pl.pallas_call(
        functools.partial(_post_kernel, alpha=alpha),
        grid=(t // tm,),
        in_specs=[row, pl.BlockSpec((1, 6, D_MODEL), lambda i: (i // spb, 0, 0)), row, row, row,
                  full(wp["w_ob"]), full(wp["w_out"]), full(wp["ln1_g"]), full(wp["ln1_b"]),
                  full(wp["w_router"])],
        out_specs=[row, pl.BlockSpec((tm * ROW_TILES, LANES), lambda i: (i, 0)),
                   pl.BlockSpec((1, N_EXPERTS, tm), lambda i: (i, 0, 0))],
        out_shape=[jax.ShapeDtypeStruct((t, D_MODEL), F32),
                   jax.ShapeDtypeStruct((t * ROW_TILES, LANES), F32),
                   jax.ShapeDtypeStruct((t // tm, N_EXPERTS, tm), F32)],
        compiler_params=_cparams(("arbitrary",)),
        name="post",
    )(x2, mod3, o, ap, sgb, wp["w_ob"], wp["w_out"], wp["ln1_g"], wp["ln1_b"], wp["w_router"])


def _route_kernel(aff_ref, spos_ref, flat_ref, cum_ref, tokp_ref, cnt_ref, m_ref, p_ref, *, cap, nb):
    bits = pltpu.bitcast(aff_ref[...], I32)
    tm = bits.shape[2]

    def count(mask):
        c = jnp.sum(jnp.where(mask, 1, 0), axis=0, keepdims=True)
        return jnp.sum(c, axis=2, keepdims=True)

    def bisect(i, thr):
        cand = thr | lax.shift_left(jnp.int32(1), 30 - i)
        return jnp.where(count(bits >= cand) >= cap, cand, thr)

    thr = lax.fori_loop(0, 31, bisect, jnp.zeros((1, N_EXPERTS, 1), I32))
    gt = bits > thr
    eq = bits == thr
    need = cap - count(gt)

    r = lax.broadcasted_iota(I32, (tm, tm), 0)
    c = lax.broadcasted_iota(I32, (tm, tm), 1)
    upper = jnp.where(r <= c, 1.0, 0.0).astype(BF16)
    er = lax.broadcasted_iota(I32, (N_EXPERTS, N_EXPERTS), 0)
    ec = lax.broadcasted_iota(I32, (N_EXPERTS, N_EXPERTS), 1)
    below = jnp.where(ec < er, 1.0, 0.0).astype(F32)

    def prefix(write_cum):
        def body(j, carry):
            blk = m_ref[j]
            incl = _dot(blk.astype(BF16), upper)
            p_ref[j] = incl - blk + carry
            if write_cum:
                cum_ref[j] = jnp.broadcast_to(carry, (N_EXPERTS, LANES)).astype(I32)
            return carry + jnp.sum(blk, axis=1, keepdims=True)
        return lax.fori_loop(0, nb, body, jnp.zeros((N_EXPERTS, 1), F32))

    m_ref[...] = jnp.where(eq, 1.0, 0.0)
    prefix(False)
    sel = gt | (eq & (p_ref[...] < need.astype(F32)))
    self32 = jnp.where(sel, 1.0, 0.0)
    m_ref[...] = self32
    total = prefix(True)
    cum_ref[nb] = jnp.broadcast_to(total, (N_EXPERTS, LANES)).astype(I32)
    spos_ref[...] = jnp.where(sel, p_ref[...].astype(I32), -1)

    cnt = jnp.sum(self32, axis=1, keepdims=True)
    cnt_ref[...] = jnp.broadcast_to(cnt, cnt_ref.shape).astype(I32)
    m_ref[...] = jnp.broadcast_to(cnt, m_ref.shape)
    prefix(False)
    tokp_ref[...] = p_ref[...].astype(I32)

    def within(j, _):
        blk = jnp.where(spos_ref[j] >= 0, 1.0, 0.0)
        w = jnp.dot(below, blk, preferred_element_type=F32)
        flat_ref[j] = (p_ref[j] + w).astype(I32)
        return 0
    lax.fori_loop(0, nb, within, 0)


def _route(aff3, cap):
    nb, ne, tm = aff3.shape
    shp = jax.ShapeDtypeStruct((nb, ne, tm), I32)
    return pl.pallas_call(
        functools.partial(_route_kernel, cap=cap, nb=nb),
        out_shape=[shp, shp, jax.ShapeDtypeStruct((nb + 1, ne, LANES), I32), shp, shp],
        scratch_shapes=[pltpu.VMEM((nb, ne, tm), F32), pltpu.VMEM((nb, ne, tm), F32)],
        compiler_params=pltpu.CompilerParams(vmem_limit_bytes=VMEM_LIMIT),
        name="route",
    )(aff3)


def _slots_kernel(blo_ref, bhi_ref, spos_ref, aff_ref, flat_ref, idx_ref, gate_ref, dst_ref, *, tm, nt):
    e = pl.program_id(0)
    j = pl.program_id(1)
    slot = j * SLOT_TILE + lax.broadcasted_iota(I32, (SLOT_TILE, 1), 0)
    lane = lax.broadcasted_iota(I32, (1, tm), 1).astype(F32)
    zero = jnp.zeros((1, tm), F32)
    lo = blo_ref[e * nt + j]
    hi = bhi_ref[e * nt + j]

    def block(b, off):
        onehot = jnp.where(slot == spos_ref[b, pl.ds(e, 1), :] + off, 1.0, 0.0).astype(BF16)
        g = aff_ref[b, pl.ds(e, 1), :]
        g0 = g.astype(BF16).astype(F32)
        g1 = (g - g0).astype(BF16).astype(F32)
        g2 = g - g0 - g1
        fl = flat_ref[b, pl.ds(e, 1), :]
        rows = [lane, (jnp.zeros((1, tm), I32) + b).astype(F32), g0, g1, g2,
                lax.shift_right_logical(fl, 8).astype(F32), (fl & 255).astype(F32), zero]
        vals = jnp.concatenate(rows + [zero] * 8, axis=0).astype(BF16)
        return _dot_nt(vals, onehot)

    def body(i, acc):
        b0 = lo + 2 * i
        b1 = b0 + 1
        return acc + block(b0, 0) + block(jnp.minimum(b1, hi - 1), jnp.where(b1 < hi, 0, -(1 << 30)))

    a = lax.fori_loop(0, (hi - lo + 1) // 2, body, jnp.zeros((16, SLOT_TILE), F32))
    idx_ref[0] = (a[1:2] * tm + a[0:1]).astype(I32)
    gate_ref[0] = a[2:3] + a[3:4] + a[4:5]
    dst_ref[0] = (a[5:6] * 256.0 + a[6:7]).astype(I32)


def _slots(cum, spos, aff3, flat, cap):
    nb, ne, tm = spos.shape
    nt = cap // SLOT_TILE
    edges = jnp.arange(nt + 1, dtype=I32) * SLOT_TILE
    cs, ce = cum[:-1].T, cum[1:].T
    blo = jnp.sum(ce[:, None, :] <= edges[None, :-1, None], axis=2).astype(I32).reshape(-1)
    bhi = jnp.sum(cs[:, None, :] < edges[None, 1:, None], axis=2).astype(I32).reshape(-1)
    full = lambda a: pl.BlockSpec(a.shape, lambda e, j, lo, hi: (0,) * a.ndim)
    out = pl.BlockSpec((1, 1, SLOT_TILE), lambda e, j, lo, hi: (e * nt + j, 0, 0))
    return pl.pallas_call(
        functools.partial(_slots_kernel, tm=tm, nt=nt),
        grid_spec=pltpu.PrefetchScalarGridSpec(
            num_scalar_prefetch=2,
            grid=(ne, nt),
            in_specs=[full(spos), full(aff3), full(flat)],
            out_specs=[out, out, out]),
        out_shape=[jax.ShapeDtypeStruct((ne * nt, 1, SLOT_TILE), I32),
                   jax.ShapeDtypeStruct((ne * nt, 1, SLOT_TILE), F32),
                   jax.ShapeDtypeStruct((ne * nt, 1, SLOT_TILE), I32)],
        compiler_params=_cparams(("arbitrary", "arbitrary")),
        name="slots",
    )(blo, bhi, spos, aff3, flat)


def _moe_kernel(idx_ref, idxn_ref, dst_ref, gate_ref, w1_ref, w3_ref, w2_ref, h2_hbm, y_hbm,
                rin_ref, rout_ref, xe_ref, acc_ref, sem, *, ch, nf, mch):
    f = pl.program_id(2)
    blk = pl.program_id(0) * pl.num_programs(1) + pl.program_id(1)
    nblk = pl.num_programs(0) * pl.num_programs(1)
    par = blk & 1
    nm = ch // mch
    per_chunk = ch // (nf * nm)
    rows = ch * ROW_TILES

    def gather_row(ids_ref, s, buf):
        t = ids_ref[0, 0, s]
        return pltpu.make_async_copy(h2_hbm.at[pl.ds(pl.multiple_of(t * ROW_TILES, ROW_TILES), ROW_TILES), :],
                                     rin_ref.at[buf, pl.ds(pl.multiple_of(s * ROW_TILES, ROW_TILES), ROW_TILES), :],
                                     sem.at[buf])

    def gather_all(buf):
        return pltpu.make_async_copy(h2_hbm.at[pl.ds(0, rows), :], rin_ref.at[buf], sem.at[buf])

    def scatter_row(s):
        d = dst_ref[0, 0, s]
        return pltpu.make_async_copy(rout_ref.at[pl.ds(pl.multiple_of(s * ROW_TILES, ROW_TILES), ROW_TILES), :],
                                     y_hbm.at[pl.ds(pl.multiple_of(d * ROW_TILES, ROW_TILES), ROW_TILES), :],
                                     sem.at[2])

    def scatter_all():
        return pltpu.make_async_copy(rout_ref, y_hbm.at[pl.ds(0, rows), :], sem.at[2])

    @pl.when((blk == 0) & (f == 0))
    def _():
        def start(s, _):
            gather_row(idx_ref, s, 0).start()
            return 0
        lax.fori_loop(0, ch, start, 0)

    @pl.when(f == 0)
    def _():
        gather_all(par).wait()
        for mc in range(nm):
            base = mc * mch * ROW_TILES
            parts = [rin_ref[par, pl.ds(base + kk, mch, stride=ROW_TILES), :] for kk in range(ROW_TILES)]
            xe_ref[mc * mch:(mc + 1) * mch, :] = jnp.concatenate(parts, axis=1).astype(BF16)

    w1 = w1_ref[0].astype(BF16)
    w3 = w3_ref[0].astype(BF16)
    w2 = w2_ref[0].astype(BF16)
    for mc in range(nm):
        sl = slice(mc * mch, (mc + 1) * mch)
        xm = xe_ref[sl, :]
        a = _dot(xm, w1)
        hid = (a * jax.nn.sigmoid(a) * _dot(xm, w3)).astype(BF16)
        part = _dot(hid, w2)
        first = (f * nm + mc) * per_chunk
        for i in range(per_chunk):
            gather_row(idxn_ref, first + i, 1 - par).start()

        @pl.when(f == 0)
        def _():
            acc_ref[sl, :] = part

        @pl.when(f > 0)
        def _():
            acc_ref[sl, :] += part

    @pl.when(f == nf - 1)
    def _():
        @pl.when(blk > 0)
        def _():
            scatter_all().wait()
        for mc in range(nm):
            sl = slice(mc * mch, (mc + 1) * mch)
            ye = acc_ref[sl, :] * gate_ref[sl, :]
            base = mc * mch * ROW_TILES
            for kk in range(ROW_TILES):
                rout_ref[pl.ds(base + kk, mch, stride=ROW_TILES), :] = ye[:, kk * LANES:(kk + 1) * LANES]

        def start(s, _):
            scatter_row(s).start()
            return 0
        lax.fori_loop(0, ch, start, 0)

        @pl.when(blk == nblk - 1)
        def _():
            scatter_all().wait()
            gather_all(1 - par).wait()


def _moe(idx, dst, gate, h2rows, w1, w3, w2, cap):
    ne = N_EXPERTS
    ch = min(MOE_ROWS, cap)
    nh = cap // ch
    nf = D_FF // FF_TILE
    mch = min(M_CHUNK, ch)
    assert ch % (nf * (ch // mch)) == 0
    smem = lambda: pl.BlockSpec((1, 1, ch), lambda e, h, f: (e * nh + h, 0, 0), memory_space=pltpu.SMEM)
    smem_next = pl.BlockSpec((1, 1, ch), lambda e, h, f: (jnp.minimum(e * nh + h + 1, ne * nh - 1), 0, 0),
                             memory_space=pltpu.SMEM)
    return pl.pallas_call(
        functools.partial(_moe_kernel, ch=ch, nf=nf, mch=mch),
        grid=(ne, nh, nf),
        in_specs=[smem(), smem_next, smem(),
                  pl.BlockSpec((ch, 1), lambda e, h, f: (e * nh + h, 0)),
                  pl.BlockSpec((1, D_MODEL, FF_TILE), lambda e, h, f: (e, 0, f)),
                  pl.BlockSpec((1, D_MODEL, FF_TILE), lambda e, h, f: (e, 0, f)),
                  pl.BlockSpec((1, FF_TILE, D_MODEL), lambda e, h, f: (e, f, 0)),
                  pl.BlockSpec(memory_space=pl.ANY)],
        out_specs=pl.BlockSpec(memory_space=pl.ANY),
        out_shape=jax.ShapeDtypeStruct((ne * cap * ROW_TILES, LANES), F32),
        scratch_shapes=[pltpu.VMEM((2, ch * ROW_TILES, LANES), F32),
                        pltpu.VMEM((ch * ROW_TILES, LANES), F32),
                        pltpu.VMEM((ch, D_MODEL), BF16),
                        pltpu.VMEM((ch, D_MODEL), F32),
                        pltpu.SemaphoreType.DMA((3,))],
        compiler_params=_cparams(("arbitrary", "arbitrary", "arbitrary")),
        name="moe",
    )(idx.reshape(ne * nh, 1, ch), idx.reshape(ne * nh, 1, ch), dst.reshape(ne * nh, 1, ch),
      gate.reshape(ne * cap, 1), w1, w3, w2, h2rows)


def _combine_kernel(pb_ref, pc_ref, pv_ref, y_ref, tokp_ref, cnt_ref, x1_ref, mod_ref, g2_ref, b2_ref,
                    o_ref, acc_ref, *, alpha):
    k = pl.program_id(0)
    blk = pb_ref[k]
    prev = pb_ref[jnp.maximum(k - 1, 0)]
    nxt = pb_ref[jnp.minimum(k + 1, pl.num_programs(0) - 1)]
    valid = pv_ref[k] == 1
    first = valid & ((k == 0) | (prev != blk))
    last = valid & ((k == pl.num_programs(0) - 1) | (nxt != blk) | (pv_ref[jnp.minimum(k + 1, pl.num_programs(0) - 1)] == 0))

    @pl.when(first)
    def _():
        acc_ref[...] = jnp.zeros_like(acc_ref)

    @pl.when(valid)
    def _():
        parts = [y_ref[pl.ds(kk, SLOT_TILE, stride=ROW_TILES), :] for kk in range(ROW_TILES)]
        rows = jnp.concatenate(parts, axis=1).astype(BF16)
        r = pc_ref[k] * SLOT_TILE + lax.broadcasted_iota(I32, (1, SLOT_TILE), 1)
        start = tokp_ref[...]
        own = jnp.where((r >= start) & (r < start + cnt_ref[...]), 1.0, 0.0).astype(BF16)
        acc_ref[...] += _dot(own, rows)

    @pl.when(last)
    def _():
        gate2 = mod_ref[0, 5:6, :]
        o_ref[...] = _ln(alpha * x1_ref[...] + gate2 * acc_ref[...]) * g2_ref[...] + b2_ref[...]


def _combine(pb, pc, pv, yrows, tokp_col, cnt_col, x1, mod3, wp, seq, alpha):
    t = x1.shape[0]
    tm = TM
    spb = seq // tm
    npairs = pb.shape[0]
    full = lambda a: pl.BlockSpec(a.shape, lambda k, pb, pc, pv: (0,) * a.ndim)
    row = pl.BlockSpec((tm, D_MODEL), lambda k, pb, pc, pv: (pb[k], 0))
    colspec = pl.BlockSpec((tm, 1), lambda k, pb, pc, pv: (pb[k], 0))
    return pl.pallas_call(
        functools.partial(_combine_kernel, alpha=alpha),
        grid_spec=pltpu.PrefetchScalarGridSpec(
            num_scalar_prefetch=3,
            grid=(npairs,),
            in_specs=[pl.BlockSpec((SLOT_TILE * ROW_TILES, LANES), lambda k, pb, pc, pv: (pc[k], 0)),
                      colspec, colspec, row,
                      pl.BlockSpec((1, 6, D_MODEL), lambda k, pb, pc, pv: (pb[k] // spb, 0, 0)),
                      full(wp["ln2_g"]), full(wp["ln2_b"])],
            out_specs=row,
            scratch_shapes=[pltpu.VMEM((tm, D_MODEL), F32)]),
        out_shape=jax.ShapeDtypeStruct((t, D_MODEL), F32),
        compiler_params=_cparams(("arbitrary",)),
        name="combine",
    )(pb, pc, pv, yrows, tokp_col, cnt_col, x1, mod3, wp["ln2_g"], wp["ln2_b"])


def _pair_schedule(row_start, n_rows_total, nb):
    nchunks = n_rows_total // SLOT_TILE
    npairs = nb + nchunks
    rs = row_start
    re = jnp.concatenate([rs[1:], jnp.array([n_rows_total], I32)])
    c_lo = jnp.minimum(rs // SLOT_TILE, nchunks - 1)
    c_hi = jnp.maximum(c_lo, (re - 1) // SLOT_TILE)
    n_b = c_hi - c_lo + 1
    ends = jnp.cumsum(n_b)
    starts = ends - n_b
    k = jnp.arange(npairs, dtype=I32)
    valid = k < ends[-1]
    kk = jnp.minimum(k, ends[-1] - 1)
    b = jnp.sum(ends[None, :] <= kk[:, None], axis=1).astype(I32)
    c = c_lo[b] + (kk - starts[b])
    return b.astype(I32), c.astype(I32), valid.astype(I32)


def _pack_weights(l, seq, w_in, b_in, ln_v_g, ln_v_b, w_spatial, b_spatial, w_oa, q_norm_g, w_uq, kv_norm_g,
                  w_ukv, w_ob, w_out, ln1_g, ln1_b, w_router, ln2_g, ln2_b):
    half = QK_ROPE // 2
    z32 = lambda rows: jnp.zeros((rows, half), F32)

    def rope_cols(w):
        return jnp.concatenate([w[:, :half], z32(w.shape[0]), w[:, half:], z32(w.shape[0])], axis=1)

    wi, bi = w_in[l], b_in[l][None, :]
    off_cq, off_ckv, off_kr = 2 * GM_WIDTH, 2 * GM_WIDTH + Q_LORA, 2 * GM_WIDTH + Q_LORA + KV_LORA
    off_ga = off_kr + QK_ROPE

    def repack(a):
        return jnp.concatenate([a[:, :off_kr], rope_cols(a[:, off_kr:off_ga]), a[:, off_ga:]], axis=1)

    wq = w_uq[l].reshape(Q_LORA, N_HEADS, QK_NOPE + QK_ROPE)
    wq = jnp.concatenate([wq[:, :, :QK_NOPE],
                          wq[:, :, QK_NOPE:QK_NOPE + half], jnp.zeros((Q_LORA, N_HEADS, half), F32),
                          wq[:, :, QK_NOPE + half:], jnp.zeros((Q_LORA, N_HEADS, half), F32)], axis=2)
    wkv = w_ukv[l].reshape(KV_LORA, N_HEADS, QK_NOPE + V_DIM)
    inv = ROPE_BASE ** (-jnp.arange(half, dtype=F32) / half)
    ang = jnp.arange(seq, dtype=F32)[:, None] * inv[None, :]
    cos, sin, zs = jnp.cos(ang), jnp.sin(ang), jnp.zeros((seq, half), F32)
    wr_hi = w_router[l].astype(BF16)
    wr_lo = (w_router[l] - wr_hi.astype(F32)).astype(BF16)
    wr = (jnp.zeros((D_MODEL, 2 * LANES), BF16).at[:, :N_EXPERTS].set(wr_hi)
          .at[:, LANES:LANES + N_EXPERTS].set(wr_lo))
    return {
        "w_in": repack(wi).astype(BF16), "b_in": repack(bi),
        "ln_v_g": ln_v_g[l][None, :], "ln_v_b": ln_v_b[l][None, :],
        "w_s": w_spatial[l].astype(BF16),
        "b_s": jnp.broadcast_to(b_spatial[l][:, :, None], (GM_GROUPS, CHUNK, CHUNK)),
        "w_oa": w_oa[l].astype(BF16),
        "q_norm_g": q_norm_g[l][None, :], "w_uq": wq.reshape(Q_LORA, N_HEADS * HEAD_W).astype(BF16),
        "kv_norm_g": kv_norm_g[l][None, :],
        "w_uk": wkv[:, :, :QK_NOPE].reshape(KV_LORA, N_HEADS * QK_NOPE).astype(BF16),
        "w_uvt": wkv[:, :, QK_NOPE:].reshape(KV_LORA, N_HEADS * V_DIM).T.astype(BF16),
        "cos": jnp.concatenate([cos, zs, cos, zs], axis=1), "sin": jnp.concatenate([-sin, zs, sin, zs], axis=1),
        "w_ob": w_ob[l].astype(BF16), "w_out": w_out[l].astype(BF16),
        "ln1_g": ln1_g[l][None, :], "ln1_b": ln1_b[l][None, :], "w_router": wr,
        "ln2_g": ln2_g[l][None, :], "ln2_b": ln2_b[l][None, :],
    }


def _layer(x, c, l, alpha, w_ada, b_ada, w1, w3, w2, packed):
    bsz, seq, _ = x.shape
    t = bsz * seq
    cap = EC_CAPACITY_FACTOR * t // N_EXPERTS
    x2 = x.reshape(t, D_MODEL)
    mod3 = _mod(c, w_ada[l], b_ada[l])
    u, vn, q, k, vt, sga, sgb = _inproj(x2, mod3, packed, bsz, seq)
    ap = _spatial(u, vn, sga, packed)
    o = _attention(q, k, vt).reshape(t, D_MODEL)
    x1, h2, aff3 = _post(x2, mod3, o, ap, sgb, packed, bsz, seq, alpha)
    nb = t // TM
    spos, flat, cum, tokp, cnt = _route(aff3, cap)
    idx, gate, dst = _slots(cum[:, :, 0], spos, aff3, flat, cap)
    yrows = _moe(idx, dst, gate, h2, w1[l], w3[l], w2[l], cap)
    tokp_col = tokp[:, 0, :].reshape(t, 1)
    cnt_col = cnt[:, 0, :].reshape(t, 1)
    pb, pc, pv = _pair_schedule(tokp[:, 0, 0], N_EXPERTS * cap, nb)
    y = _combine(pb, pc, pv, yrows, tokp_col, cnt_col, x1, mod3, packed, seq, alpha)
    return y.reshape(bsz, seq, D_MODEL)


def kernel(x_prompt, x_sample, c_prompt, c_sample, w_ada, b_ada, w_in, b_in, ln_v_g, ln_v_b, w_spatial, b_spatial,
           w_oa, q_norm_g, w_uq, kv_norm_g, w_ukv, w_ob, w_out, ln1_g, ln1_b, w_router, w1, w3, w2, ln2_g, ln2_b):
    depth = w_ada.shape[0]
    alpha = (2.0 * depth) ** 0.25
    outs = []
    for x, c in ((x_prompt, c_prompt), (x_sample, c_sample)):
        for l in range(depth):
            packed = _pack_weights(l, x.shape[1], w_in, b_in, ln_v_g, ln_v_b, w_spatial, b_spatial, w_oa,
                                   q_norm_g, w_uq, kv_norm_g, w_ukv, w_ob, w_out, ln1_g, ln1_b, w_router,
                                   ln2_g, ln2_b)
            x = _layer(x, c, l, alpha, w_ada, b_ada, w1, w3, w2, packed)
        outs.append(x)
    return tuple(outs)
```

```python
import functools
import math

import jax
import jax.numpy as jnp
from jax import lax
from jax.experimental import pallas as pl
from jax.experimental.pallas import tpu as pltpu

F32 = jnp.float32
BF16 = jnp.bfloat16
I32 = jnp.int32

D_MODEL = 1024
GM_WIDTH = 1024
GM_GROUPS = 8
CHUNK = 128
N_HEADS = 8
QK_NOPE = 128
QK_ROPE = 64
V_DIM = 128
Q_LORA = 384
KV_LORA = 256
ROPE_BASE = 10000.0
N_EXPERTS = 16
EC_CAPACITY_FACTOR = 2
D_FF = 2048
LN_EPS = 1e-5
RMS_EPS = 1e-6

LANES = 128
SUBLANES = 8
HEAD_W = 256
ROW_TILES = D_MODEL // LANES

C_U, C_V, C_CQ, C_CKV, C_KR, C_GA, C_GB, C_END = 0, 1024, 2048, 2432, 2688, 2816, 3840, 4864

TM_IN = 512
TQ = 2048
TM = 256
TM_WIDE = 512
SLOT_TILE = 256
FF_TILE = 512
M_CHUNK = 512
MOE_ROWS = 1024
VMEM_LIMIT = 56 * 1024 * 1024


def _cparams(sem):
    return pltpu.CompilerParams(dimension_semantics=sem, vmem_limit_bytes=VMEM_LIMIT)


def _ln(x):
    mu = jnp.mean(x, axis=-1, keepdims=True)
    xc = x - mu
    var = jnp.mean(xc * xc, axis=-1, keepdims=True)
    return xc * lax.rsqrt(var + LN_EPS)


def _rms(x):
    return x * lax.rsqrt(jnp.mean(x * x, axis=-1, keepdims=True) + RMS_EPS)


def _gelu(x):
    return 0.5 * x * (1.0 + lax.erf(x * (2.0 ** -0.5)))


def _dot(a, b):
    return jnp.dot(a, b, preferred_element_type=F32)


def _dot_nt(a, b):
    return lax.dot_general(a, b, (((1,), (1,)), ((), ())), preferred_element_type=F32)


def _mod_kernel(c_ref, w_ref, b_ref, o_ref):
    c = c_ref[...]
    s = c * jax.nn.sigmoid(c)
    o_ref[...] = jnp.dot(s, w_ref[...], preferred_element_type=F32,
                         precision=lax.Precision.HIGHEST) + b_ref[...]


def _mod(c, w_ada, b_ada):
    bsz = c.shape[0]
    bp = -(-bsz // SUBLANES) * SUBLANES
    cp = jnp.zeros((bp, D_MODEL), F32).at[:bsz].set(c)
    n = w_ada.shape[1]
    tn = 1024
    out = pl.pallas_call(
        _mod_kernel,
        grid=(n // tn,),
        in_specs=[pl.BlockSpec((bp, D_MODEL), lambda j: (0, 0)),
                  pl.BlockSpec((D_MODEL, tn), lambda j: (0, j)),
                  pl.BlockSpec((1, tn), lambda j: (0, j))],
        out_specs=pl.BlockSpec((bp, tn), lambda j: (0, j)),
        out_shape=jax.ShapeDtypeStruct((bp, n), F32),
        compiler_params=_cparams(("arbitrary",)),
        name="mod",
    )(cp, w_ada, b_ada.reshape(1, n))
    return out[:bsz].reshape(bsz, 6, D_MODEL)


def _inproj_kernel(x_ref, mod_ref, w_ref, b_ref, lvg_ref, lvb_ref, qg_ref, wuq_ref, kvg_ref,
                   wuk_ref, wuvt_ref, cos_ref, sin_ref, cost_ref, sint_ref,
                   u_ref, vn_ref, q_ref, k_ref, vt_ref, sga_ref, sgb_ref, *, qscale):
    x = x_ref[...]
    sh1 = mod_ref[0, 0:1, :]
    sc1 = mod_ref[0, 1:2, :]
    h = (_ln(x) * (1.0 + sc1) + sh1).astype(BF16)

    def proj(a, b):
        return _dot(h, w_ref[:, a:b]) + b_ref[:, a:b]

    cos = cos_ref[...]
    sin = sin_ref[...]

    def rope(t):
        return t * cos + pltpu.roll(t, 64, 1) * sin

    u_ref[...] = _gelu(proj(C_U, C_V)).astype(BF16)
    v = _gelu(proj(C_V, C_CQ))
    vn_ref[...] = (_ln(v) * lvg_ref[...] + lvb_ref[...]).astype(BF16)

    cqn = (_rms(proj(C_CQ, C_CKV)) * qg_ref[...]).astype(BF16)
    qt = _dot_nt(wuq_ref[...], cqn) * qscale
    cost = cost_ref[...]
    sint = sint_ref[...]
    for hh in range(N_HEADS):
        o = hh * HEAD_W
        q_ref[0, hh, 0:QK_NOPE, :] = qt[o:o + QK_NOPE, :].astype(BF16)
        r = qt[o + QK_NOPE:o + HEAD_W, :]
        swapped = jnp.concatenate([r[64:], r[:64]], axis=0)
        q_ref[0, hh, QK_NOPE:HEAD_W, :] = (r * cost + swapped * sint).astype(BF16)

    ckvn = (_rms(proj(C_CKV, C_KR)) * kvg_ref[...]).astype(BF16)
    kn = _dot(ckvn, wuk_ref[...])
    kr = rope(proj(C_KR, C_GA)).astype(BF16)
    for hh in range(N_HEADS):
        k_ref[0, hh, :, 0:QK_NOPE] = kn[:, hh * QK_NOPE:(hh + 1) * QK_NOPE].astype(BF16)
        k_ref[0, hh, :, QK_NOPE:HEAD_W] = kr
    vt = _dot_nt(wuvt_ref[...], ckvn)
    for hh in range(N_HEADS):
        vt_ref[0, hh, 0] = vt[hh * V_DIM:(hh + 1) * V_DIM, :].astype(BF16)

    sga_ref[...] = jax.nn.sigmoid(proj(C_GA, C_GB)).astype(BF16)
    sgb_ref[...] = jax.nn.sigmoid(proj(C_GB, C_END)).astype(BF16)


def _inproj(x2, mod3, wp, bsz, seq):
    t = bsz * seq
    tm = min(TM_IN, seq)
    spb = seq // tm
    full = lambda a: pl.BlockSpec(a.shape, lambda i: (0,) * a.ndim)
    qscale = (QK_NOPE + QK_ROPE) ** -0.5 * math.log2(math.e)
    row = pl.BlockSpec((tm, D_MODEL), lambda i: (i, 0))
    qk_spec = pl.BlockSpec((1, N_HEADS, tm, HEAD_W), lambda i: (i // spb, 0, i % spb, 0))
    qt_spec = pl.BlockSpec((1, N_HEADS, HEAD_W, tm), lambda i: (i // spb, 0, 0, i % spb))
    ins = [x2, mod3, wp["w_in"], wp["b_in"], wp["ln_v_g"], wp["ln_v_b"], wp["q_norm_g"], wp["w_uq"],
           wp["kv_norm_g"], wp["w_uk"], wp["w_uvt"], wp["cos"], wp["sin"], wp["cos"].T, wp["sin"].T]
    in_specs = [row, pl.BlockSpec((1, 6, D_MODEL), lambda i: (i // spb, 0, 0))]
    in_specs += [full(a) for a in ins[2:11]]
    in_specs += [pl.BlockSpec((tm, LANES), lambda i: (i % spb, 0))] * 2
    in_specs += [pl.BlockSpec((LANES, tm), lambda i: (0, i % spb))] * 2
    return pl.pallas_call(
        functools.partial(_inproj_kernel, qscale=qscale),
        grid=(t // tm,),
        in_specs=in_specs,
        out_specs=[row, row, qt_spec, qk_spec,
                   pl.BlockSpec((1, N_HEADS, 1, V_DIM, tm), lambda i: (i // spb, 0, i % spb, 0, 0)),
                   row, row],
        out_shape=[jax.ShapeDtypeStruct((t, D_MODEL), BF16),
                   jax.ShapeDtypeStruct((t, D_MODEL), BF16),
                   jax.ShapeDtypeStruct((bsz, N_HEADS, HEAD_W, seq), BF16),
                   jax.ShapeDtypeStruct((bsz, N_HEADS, seq, HEAD_W), BF16),
                   jax.ShapeDtypeStruct((bsz, N_HEADS, spb, V_DIM, tm), BF16),
                   jax.ShapeDtypeStruct((t, D_MODEL), BF16),
                   jax.ShapeDtypeStruct((t, D_MODEL), BF16)],
        compiler_params=_cparams(("arbitrary",)),
        name="inproj",
    )(*ins)


def _spatial_kernel(u_ref, vn_ref, ws_ref, bs_ref, woa_ref, sga_ref, o_ref, g_ref):
    tm = u_ref.shape[0]
    nc = tm // CHUNK
    for g in range(GM_GROUPS):
        c0 = g * CHUNK
        rhs = jnp.concatenate([vn_ref[n * CHUNK:(n + 1) * CHUNK, c0:c0 + CHUNK] for n in range(nc)], axis=1)
        mixed = _dot(ws_ref[g], rhs)
        bias = bs_ref[g]
        for n in range(nc):
            m = mixed[:, n * CHUNK:(n + 1) * CHUNK] + bias
            uu = u_ref[n * CHUNK:(n + 1) * CHUNK, c0:c0 + CHUNK].astype(F32)
            g_ref[n * CHUNK:(n + 1) * CHUNK, c0:c0 + CHUNK] = (uu * m).astype(BF16)
    ya = _dot(g_ref[...], woa_ref[...])
    o_ref[...] = (sga_ref[...].astype(F32) * ya).astype(BF16)


def _spatial(u, vn, sga, wp):
    t = u.shape[0]
    tm = TM_WIDE
    row = pl.BlockSpec((tm, D_MODEL), lambda i: (i, 0))
    full = lambda a: pl.BlockSpec(a.shape, lambda i: (0,) * a.ndim)
    return pl.pallas_call(
        _spatial_kernel,
        grid=(t // tm,),
        in_specs=[row, row, full(wp["w_s"]), full(wp["b_s"]), full(wp["w_oa"]), row],
        out_specs=row,
        out_shape=jax.ShapeDtypeStruct((t, D_MODEL), BF16),
        scratch_shapes=[pltpu.VMEM((tm, GM_WIDTH), BF16)],
        compiler_params=_cparams(("arbitrary",)),
        name="spatial",
    )(u, vn, wp["w_s"], wp["b_s"], wp["w_oa"], sga)


def _attn_kernel(q_ref, k_ref, vt_ref, o_ref, acc_ref, sa_ref, sb_ref, pa_ref, pb_ref, *, nk, tk):
    q = q_ref[0, 0]
    tq = q.shape[1]
    acc_ref[...] = jnp.zeros_like(acc_ref)
    pb_ref[...] = jnp.zeros_like(pb_ref)

    def scores(c):
        return _dot(k_ref[0, 0, pl.ds(pl.multiple_of(c * tk, tk), tk), :], q)

    def softmax(s_ref, p_ref, m, l):
        s = s_ref[...]
        m_new = jnp.maximum(m, jnp.max(s, axis=0, keepdims=True))
        p = jnp.exp2(s - m_new)
        p_ref[...] = p.astype(BF16)
        alpha = jnp.exp2(m - m_new)
        return m_new, alpha * l + jnp.sum(p, axis=0, keepdims=True), alpha

    def accumulate(alpha, c, p_ref):
        return alpha * acc_ref[...] + _dot(vt_ref[0, 0, c], p_ref[...])

    sa_ref[...] = scores(0)

    def body(i, carry):
        m, l, alpha = carry
        c0 = 2 * i
        sb_ref[...] = scores(c0 + 1)
        acc_ref[...] = accumulate(alpha, jnp.maximum(c0 - 1, 0), pb_ref)
        m, l, alpha = softmax(sa_ref, pa_ref, m, l)
        sa_ref[...] = scores(jnp.minimum(c0 + 2, nk - 1))
        acc_ref[...] = accumulate(alpha, c0, pa_ref)
        return softmax(sb_ref, pb_ref, m, l)

    init = (jnp.full((1, tq), -jnp.inf, F32), jnp.zeros((1, tq), F32), jnp.ones((1, tq), F32))
    _, l, alpha = lax.fori_loop(0, nk // 2, body, init)
    o = accumulate(alpha, nk - 1, pb_ref) / l
    o_ref[0] = o.T.astype(BF16)


def _attention(q, k, vt):
    bsz, nh, _, seq = q.shape
    nk, tk = vt.shape[2], vt.shape[4]
    assert nk % 2 == 0
    tq = min(TQ, seq)
    return pl.pallas_call(
        functools.partial(_attn_kernel, nk=nk, tk=tk),
        grid=(bsz, nh, seq // tq),
        in_specs=[pl.BlockSpec((1, 1, HEAD_W, tq), lambda b, h, i: (b, h, 0, i)),
                  pl.BlockSpec((1, 1, seq, HEAD_W), lambda b, h, i: (b, h, 0, 0)),
                  pl.BlockSpec((1, 1, nk, V_DIM, tk), lambda b, h, i: (b, h, 0, 0, 0))],
        out_specs=pl.BlockSpec((1, tq, V_DIM), lambda b, h, i: (b, i, h)),
        out_shape=jax.ShapeDtypeStruct((bsz, seq, nh * V_DIM), BF16),
        scratch_shapes=[pltpu.VMEM((V_DIM, tq), F32), pltpu.VMEM((tk, tq), F32), pltpu.VMEM((tk, tq), F32),
                        pltpu.VMEM((tk, tq), BF16), pltpu.VMEM((tk, tq), BF16)],
        compiler_params=_cparams(("arbitrary", "arbitrary", "arbitrary")),
        name="attn",
    )(q, k, vt)


def _post_kernel(x_ref, mod_ref, o_ref, ap_ref, sgb_ref, wob_ref, wout_ref, g1_ref, b1_ref, wr_ref,
                 x1_ref, h2_ref, aff_ref, *, alpha):
    yb = _dot(o_ref[...], wob_ref[...])
    mixin = (ap_ref[...].astype(F32) + sgb_ref[...].astype(F32) * yb).astype(BF16)
    mix = _dot(mixin, wout_ref[...])
    gate1 = mod_ref[0, 2:3, :]
    sh2 = mod_ref[0, 3:4, :]
    sc2 = mod_ref[0, 4:5, :]
    x1 = _ln(alpha * x_ref[...] + gate1 * mix) * g1_ref[...] + b1_ref[...]
    x1_ref[...] = x1
    h2 = _ln(x1) * (1.0 + sc2) + sh2
    tm = h2.shape[0]
    for kk in range(ROW_TILES):
        h2_ref[pl.ds(kk, tm, stride=ROW_TILES), :] = h2[:, kk * LANES:(kk + 1) * LANES]
    h_hi = h2.astype(BF16)
    h_lo = (h2 - h_hi.astype(F32)).astype(BF16)
    prod = _dot(h_hi, wr_ref[...]) + _dot(h_lo, wr_ref[...])
    logits = prod[:, 0:LANES] + prod[:, LANES:2 * LANES]
    lane = lax.broadcasted_iota(I32, logits.shape, 1)
    logits = jnp.where(lane < N_EXPERTS, logits, -jnp.inf)
    ex = jnp.exp(logits - jnp.max(logits, axis=-1, keepdims=True))
    aff = ex / jnp.sum(ex, axis=-1, keepdims=True)
    afft = aff.T
    for j in range(tm // TM):
        aff_ref[j] = afft[0:N_EXPERTS, j * TM:(j + 1) * TM]


def _post(x2, mod3, o, ap, sgb, wp, bsz, seq, alpha):
    t = x2.shape[0]
    tm = TM_WIDE
    spb = seq // tm
    row = pl.BlockSpec((tm, D_MODEL), lambda i: (i, 0))
    full = lambda a: pl.BlockSpec(a.shape, lambda i: (0,) * a.ndim)
    return pl.pallas_call(
        functools.partial(_post_kernel, alpha=alpha),
        grid=(t // tm,),
        in_specs=[row, pl.BlockSpec((1, 6, D_MODEL), lambda i: (i // spb, 0, 0)), row, row, row,
                  full(wp["w_ob"]), full(wp["w_out"]), full(wp["ln1_g"]), full(wp["ln1_b"]),
                  full(wp["w_router"])],
        out_specs=[row, pl.BlockSpec((tm * ROW_TILES, LANES), lambda i: (i, 0)),
                   pl.BlockSpec((tm // TM, N_EXPERTS, TM), lambda i: (i, 0, 0))],
        out_shape=[jax.ShapeDtypeStruct((t, D_MODEL), F32),
                   jax.ShapeDtypeStruct((t * ROW_TILES, LANES), F32),
                   jax.ShapeDtypeStruct((t // TM, N_EXPERTS, TM), F32)],
        compiler_params=_cparams(("arbitrary",)),
        name="post",
    )(x2, mod3, o, ap, sgb, wp["w_ob"], wp["w_out"], wp["ln1_g"], wp["ln1_b"], wp["w_router"])


def _route_kernel(aff_ref, spos_ref, flat_ref, cum_ref, tokp_ref, cnt_ref, m_ref, p_ref, *, cap, nb):
    bits = pltpu.bitcast(aff_ref[...], I32)
    tm = bits.shape[2]

    def count(mask):
        c = jnp.sum(jnp.where(mask, 1, 0), axis=0, keepdims=True)
        return jnp.sum(c, axis=2, keepdims=True)

    def bisect(i, thr):
        cand = thr | lax.shift_left(jnp.int32(1), 30 - i)
        return jnp.where(count(bits >= cand) >= cap, cand, thr)

    thr = lax.fori_loop(0, 31, bisect, jnp.zeros((1, N_EXPERTS, 1), I32))
    gt = bits > thr
    eq = bits == thr
    need = cap - count(gt)

    r = lax.broadcasted_iota(I32, (tm, tm), 0)
    c = lax.broadcasted_iota(I32, (tm, tm), 1)
    upper = jnp.where(r <= c, 1.0, 0.0).astype(BF16)
    er = lax.broadcasted_iota(I32, (N_EXPERTS, N_EXPERTS), 0)
    ec = lax.broadcasted_iota(I32, (N_EXPERTS, N_EXPERTS), 1)
    below = jnp.where(ec < er, 1.0, 0.0).astype(F32)

    def prefix(write_cum):
        def body(j, carry):
            blk = m_ref[j]
            incl = _dot(blk.astype(BF16), upper)
            p_ref[j] = incl - blk + carry
            if write_cum:
                cum_ref[j] = jnp.broadcast_to(carry, (N_EXPERTS, LANES)).astype(I32)
            return carry + jnp.sum(blk, axis=1, keepdims=True)
        return lax.fori_loop(0, nb, body, jnp.zeros((N_EXPERTS, 1), F32))

    m_ref[...] = jnp.where(eq, 1.0, 0.0)
    prefix(False)
    sel = gt | (eq & (p_ref[...] < need.astype(F32)))
    self32 = jnp.where(sel, 1.0, 0.0)
    m_ref[...] = self32
    total = prefix(True)
    cum_ref[nb] = jnp.broadcast_to(total, (N_EXPERTS, LANES)).astype(I32)
    spos_ref[...] = jnp.where(sel, p_ref[...].astype(I32), -1)

    cnt = jnp.sum(self32, axis=1, keepdims=True)
    cnt_ref[...] = jnp.broadcast_to(cnt, cnt_ref.shape).astype(I32)
    m_ref[...] = jnp.broadcast_to(cnt, m_ref.shape)
    prefix(False)
    tokp_ref[...] = p_ref[...].astype(I32)

    def within(j, _):
        blk = jnp.where(spos_ref[j] >= 0, 1.0, 0.0)
        w = jnp.dot(below, blk, preferred_element_type=F32)
        flat_ref[j] = (p_ref[j] + w).astype(I32)
        return 0
    lax.fori_loop(0, nb, within, 0)


def _route(aff3, cap):
    nb, ne, tm = aff3.shape
    shp = jax.ShapeDtypeStruct((nb, ne, tm), I32)
    return pl.pallas_call(
        functools.partial(_route_kernel, cap=cap, nb=nb),
        out_shape=[shp, shp, jax.ShapeDtypeStruct((nb + 1, ne, LANES), I32), shp, shp],
        scratch_shapes=[pltpu.VMEM((nb, ne, tm), F32), pltpu.VMEM((nb, ne, tm), F32)],
        compiler_params=pltpu.CompilerParams(vmem_limit_bytes=VMEM_LIMIT),
        name="route",
    )(aff3)


def _slots_kernel(blo_ref, bhi_ref, spos_ref, aff_ref, flat_ref, idx_ref, gate_ref, dst_ref, *, tm, nt):
    e = pl.program_id(0)
    j = pl.program_id(1)
    slot = j * SLOT_TILE + lax.broadcasted_iota(I32, (SLOT_TILE, 1), 0)
    lane = lax.broadcasted_iota(I32, (1, tm), 1).astype(F32)
    zero = jnp.zeros((1, tm), F32)
    lo = blo_ref[e * nt + j]
    hi = bhi_ref[e * nt + j]

    def block(b, off):
        onehot = jnp.where(slot == spos_ref[b, pl.ds(e, 1), :] + off, 1.0, 0.0).astype(BF16)
        g = aff_ref[b, pl.ds(e, 1), :]
        g0 = g.astype(BF16).astype(F32)
        g1 = (g - g0).astype(BF16).astype(F32)
        g2 = g - g0 - g1
        fl = flat_ref[b, pl.ds(e, 1), :]
        rows = [lane, (jnp.zeros((1, tm), I32) + b).astype(F32), g0, g1, g2,
                lax.shift_right_logical(fl, 8).astype(F32), (fl & 255).astype(F32), zero]
        vals = jnp.concatenate(rows + [zero] * 8, axis=0).astype(BF16)
        return _dot_nt(vals, onehot)

    def body(i, acc):
        b0 = lo + 2 * i
        b1 = b0 + 1
        return acc + block(b0, 0) + block(jnp.minimum(b1, hi - 1), jnp.where(b1 < hi, 0, -(1 << 30)))

    a = lax.fori_loop(0, (hi - lo + 1) // 2, body, jnp.zeros((16, SLOT_TILE), F32))
    idx_ref[0] = (a[1:2] * tm + a[0:1]).astype(I32)
    gate_ref[0] = a[2:3] + a[3:4] + a[4:5]
    dst_ref[0] = (a[5:6] * 256.0 + a[6:7]).astype(I32)


def _slots(cum, spos, aff3, flat, cap):
    nb, ne, tm = spos.shape
    nt = cap // SLOT_TILE
    edges = jnp.arange(nt + 1, dtype=I32) * SLOT_TILE
    cs, ce = cum[:-1].T, cum[1:].T
    blo = jnp.sum(ce[:, None, :] <= edges[None, :-1, None], axis=2).astype(I32).reshape(-1)
    bhi = jnp.sum(cs[:, None, :] < edges[None, 1:, None], axis=2).astype(I32).reshape(-1)
    full = lambda a: pl.BlockSpec(a.shape, lambda e, j, lo, hi: (0,) * a.ndim)
    out = pl.BlockSpec((1, 1, SLOT_TILE), lambda e, j, lo, hi: (e * nt + j, 0, 0))
    return pl.pallas_call(
        functools.partial(_slots_kernel, tm=tm, nt=nt),
        grid_spec=pltpu.PrefetchScalarGridSpec(
            num_scalar_prefetch=2,
            grid=(ne, nt),
            in_specs=[full(spos), full(aff3), full(flat)],
            out_specs=[out, out, out]),
        out_shape=[jax.ShapeDtypeStruct((ne * nt, 1, SLOT_TILE), I32),
                   jax.ShapeDtypeStruct((ne * nt, 1, SLOT_TILE), F32),
                   jax.ShapeDtypeStruct((ne * nt, 1, SLOT_TILE), I32)],
        compiler_params=_cparams(("arbitrary", "arbitrary")),
        name="slots",
    )(blo, bhi, spos, aff3, flat)


def _moe_kernel(idx_ref, idxn_ref, dstp_ref, dst_ref, gate_ref, w1_ref, w3_ref, w2_ref, h2_hbm, y_hbm,
                rin_ref, rout_ref, xe_ref, acc_ref, sem, *, ch, nf, mch):
    f = pl.program_id(2)
    blk = pl.program_id(0) * pl.num_programs(1) + pl.program_id(1)
    nblk = pl.num_programs(0) * pl.num_programs(1)
    par = blk & 1
    nm = ch // mch
    per_chunk = ch // (nf * nm)
    rows = ch * ROW_TILES

    def gather_row(ids_ref, s, buf):
        t = ids_ref[0, 0, s]
        return pltpu.make_async_copy(h2_hbm.at[pl.ds(pl.multiple_of(t * ROW_TILES, ROW_TILES), ROW_TILES), :],
                                     rin_ref.at[buf, pl.ds(pl.multiple_of(s * ROW_TILES, ROW_TILES), ROW_TILES), :],
                                     sem.at[buf])

    def gather_all(buf):
        return pltpu.make_async_copy(h2_hbm.at[pl.ds(0, rows), :], rin_ref.at[buf], sem.at[buf])

    def scatter_row(ids_ref, s):
        d = ids_ref[0, 0, s]
        return pltpu.make_async_copy(rout_ref.at[pl.ds(pl.multiple_of(s * ROW_TILES, ROW_TILES), ROW_TILES), :],
                                     y_hbm.at[pl.ds(pl.multiple_of(d * ROW_TILES, ROW_TILES), ROW_TILES), :],
                                     sem.at[2])

    def scatter_all():
        return pltpu.make_async_copy(rout_ref, y_hbm.at[pl.ds(0, rows), :], sem.at[2])

    @pl.when((blk == 0) & (f == 0))
    def _():
        rout_ref[...] = jnp.zeros_like(rout_ref)

        def start(s, _):
            gather_row(idx_ref, s, 0).start()
            return 0
        lax.fori_loop(0, ch, start, 0)

    @pl.when(f == 0)
    def _():
        gather_all(par).wait()
        for mc in range(nm):
            base = mc * mch * ROW_TILES
            parts = [rin_ref[par, pl.ds(base + kk, mch, stride=ROW_TILES), :] for kk in range(ROW_TILES)]
            xe_ref[mc * mch:(mc + 1) * mch, :] = jnp.concatenate(parts, axis=1).astype(BF16)

    for mc in range(nm):
        sl = slice(mc * mch, (mc + 1) * mch)
        xm = xe_ref[sl, :]
        a = _dot(xm, w1_ref[0])
        hid = (a * jax.nn.sigmoid(a) * _dot(xm, w3_ref[0])).astype(BF16)
        part = _dot(hid, w2_ref[0])
        first = (f * nm + mc) * per_chunk
        for i in range(per_chunk):
            gather_row(idxn_ref, first + i, 1 - par).start()
            scatter_row(dstp_ref, first + i).start()

        @pl.when(f == 0)
        def _():
            acc_ref[sl, :] = part

        @pl.when(f > 0)
        def _():
            acc_ref[sl, :] += part

    @pl.when(f == nf - 1)
    def _():
        scatter_all().wait()
        for mc in range(nm):
            sl = slice(mc * mch, (mc + 1) * mch)
            ye = acc_ref[sl, :] * gate_ref[sl, :]
            base = mc * mch * ROW_TILES
            for kk in range(ROW_TILES):
                rout_ref[pl.ds(base + kk, mch, stride=ROW_TILES), :] = ye[:, kk * LANES:(kk + 1) * LANES]

        @pl.when(blk == nblk - 1)
        def _():
            def start(s, _):
                scatter_row(dst_ref, s).start()
                return 0
            lax.fori_loop(0, ch, start, 0)
            scatter_all().wait()
            gather_all(1 - par).wait()


def _moe(idx, dst, gate, h2rows, w1, w3, w2, cap):
    ne = N_EXPERTS
    ch = min(MOE_ROWS, cap)
    nh = cap // ch
    nf = D_FF // FF_TILE
    mch = min(M_CHUNK, ch)
    assert ch % (nf * (ch // mch)) == 0
    smem = lambda: pl.BlockSpec((1, 1, ch), lambda e, h, f: (e * nh + h, 0, 0), memory_space=pltpu.SMEM)
    smem_next = pl.BlockSpec((1, 1, ch), lambda e, h, f: (jnp.minimum(e * nh + h + 1, ne * nh - 1), 0, 0),
                             memory_space=pltpu.SMEM)
    smem_prev = pl.BlockSpec((1, 1, ch), lambda e, h, f: (jnp.maximum(e * nh + h - 1, 0), 0, 0),
                             memory_space=pltpu.SMEM)
    return pl.pallas_call(
        functools.partial(_moe_kernel, ch=ch, nf=nf, mch=mch),
        grid=(ne, nh, nf),
        in_specs=[smem(), smem_next, smem_prev, smem(),
                  pl.BlockSpec((ch, 1), lambda e, h, f: (e * nh + h, 0)),
                  pl.BlockSpec((1, D_MODEL, FF_TILE), lambda e, h, f: (e, 0, f)),
                  pl.BlockSpec((1, D_MODEL, FF_TILE), lambda e, h, f: (e, 0, f)),
                  pl.BlockSpec((1, FF_TILE, D_MODEL), lambda e, h, f: (e, f, 0)),
                  pl.BlockSpec(memory_space=pl.ANY)],
        out_specs=pl.BlockSpec(memory_space=pl.ANY),
        out_shape=jax.ShapeDtypeStruct((ne * cap * ROW_TILES, LANES), F32),
        scratch_shapes=[pltpu.VMEM((2, ch * ROW_TILES, LANES), F32),
                        pltpu.VMEM((ch * ROW_TILES, LANES), F32),
                        pltpu.VMEM((ch, D_MODEL), BF16),
                        pltpu.VMEM((ch, D_MODEL), F32),
                        pltpu.SemaphoreType.DMA((3,))],
        compiler_params=_cparams(("arbitrary", "arbitrary", "arbitrary")),
        name="moe",
    )(idx.reshape(ne * nh, 1, ch), idx.reshape(ne * nh, 1, ch), dst.reshape(ne * nh, 1, ch),
      dst.reshape(ne * nh, 1, ch), gate.reshape(ne * cap, 1), w1, w3, w2, h2rows)


def _combine_kernel(pb_ref, pc_ref, pv_ref, y_ref, tokp_ref, cnt_ref, x1_ref, mod_ref, g2_ref, b2_ref,
                    o_ref, acc_ref, *, alpha):
    k = pl.program_id(0)
    blk = pb_ref[k]
    prev = pb_ref[jnp.maximum(k - 1, 0)]
    nxt = pb_ref[jnp.minimum(k + 1, pl.num_programs(0) - 1)]
    valid = pv_ref[k] == 1
    first = valid & ((k == 0) | (prev != blk))
    last = valid & ((k == pl.num_programs(0) - 1) | (nxt != blk) | (pv_ref[jnp.minimum(k + 1, pl.num_programs(0) - 1)] == 0))

    @pl.when(first)
    def _():
        acc_ref[...] = jnp.zeros_like(acc_ref)

    @pl.when(valid)
    def _():
        parts = [y_ref[pl.ds(kk, SLOT_TILE, stride=ROW_TILES), :] for kk in range(ROW_TILES)]
        rows = jnp.concatenate(parts, axis=1).astype(BF16)
        r = pc_ref[k] * SLOT_TILE + lax.broadcasted_iota(I32, (1, SLOT_TILE), 1)
        start = tokp_ref[...]
        own = jnp.where((r >= start) & (r < start + cnt_ref[...]), 1.0, 0.0).astype(BF16)
        acc_ref[...] += _dot(own, rows)

    @pl.when(last)
    def _():
        gate2 = mod_ref[0, 5:6, :]
        o_ref[...] = _ln(alpha * x1_ref[...] + gate2 * acc_ref[...]) * g2_ref[...] + b2_ref[...]


def _combine(pb, pc, pv, yrows, tokp_col, cnt_col, x1, mod3, wp, seq, alpha):
    t = x1.shape[0]
    tm = TM
    spb = seq // tm
    npairs = pb.shape[0]
    full = lambda a: pl.BlockSpec(a.shape, lambda k, pb, pc, pv: (0,) * a.ndim)
    row = pl.BlockSpec((tm, D_MODEL), lambda k, pb, pc, pv: (pb[k], 0))
    colspec = pl.BlockSpec((tm, 1), lambda k, pb, pc, pv: (pb[k], 0))
    return pl.pallas_call(
        functools.partial(_combine_kernel, alpha=alpha),
        grid_spec=pltpu.PrefetchScalarGridSpec(
            num_scalar_prefetch=3,
            grid=(npairs,),
            in_specs=[pl.BlockSpec((SLOT_TILE * ROW_TILES, LANES), lambda k, pb, pc, pv: (pc[k], 0)),
                      colspec, colspec, row,
                      pl.BlockSpec((1, 6, D_MODEL), lambda k, pb, pc, pv: (pb[k] // spb, 0, 0)),
                      full(wp["ln2_g"]), full(wp["ln2_b"])],
            out_specs=row,
            scratch_shapes=[pltpu.VMEM((tm, D_MODEL), F32)]),
        out_shape=jax.ShapeDtypeStruct((t, D_MODEL), F32),
        compiler_params=_cparams(("arbitrary",)),
        name="combine",
    )(pb, pc, pv, yrows, tokp_col, cnt_col, x1, mod3, wp["ln2_g"], wp["ln2_b"])


def _pair_schedule(row_start, n_rows_total, nb):
    nchunks = n_rows_total // SLOT_TILE
    npairs = nb + nchunks
    rs = row_start
    re = jnp.concatenate([rs[1:], jnp.array([n_rows_total], I32)])
    c_lo = jnp.minimum(rs // SLOT_TILE, nchunks - 1)
    c_hi = jnp.maximum(c_lo, (re - 1) // SLOT_TILE)
    n_b = c_hi - c_lo + 1
    ends = jnp.cumsum(n_b)
    starts = ends - n_b
    k = jnp.arange(npairs, dtype=I32)
    valid = k < ends[-1]
    kk = jnp.minimum(k, ends[-1] - 1)
    b = jnp.sum(ends[None, :] <= kk[:, None], axis=1).astype(I32)
    c = c_lo[b] + (kk - starts[b])
    return b.astype(I32), c.astype(I32), valid.astype(I32)


def _pack_weights(l, seq, w_in, b_in, ln_v_g, ln_v_b, w_spatial, b_spatial, w_oa, q_norm_g, w_uq, kv_norm_g,
                  w_ukv, w_ob, w_out, ln1_g, ln1_b, w_router, ln2_g, ln2_b):
    half = QK_ROPE // 2
    z32 = lambda rows: jnp.zeros((rows, half), F32)

    def rope_cols(w):
        return jnp.concatenate([w[:, :half], z32(w.shape[0]), w[:, half:], z32(w.shape[0])], axis=1)

    wi, bi = w_in[l], b_in[l][None, :]
    off_cq, off_ckv, off_kr = 2 * GM_WIDTH, 2 * GM_WIDTH + Q_LORA, 2 * GM_WIDTH + Q_LORA + KV_LORA
    off_ga = off_kr + QK_ROPE

    def repack(a):
        return jnp.concatenate([a[:, :off_kr], rope_cols(a[:, off_kr:off_ga]), a[:, off_ga:]], axis=1)

    wq = w_uq[l].reshape(Q_LORA, N_HEADS, QK_NOPE + QK_ROPE)
    wq = jnp.concatenate([wq[:, :, :QK_NOPE],
                          wq[:, :, QK_NOPE:QK_NOPE + half], jnp.zeros((Q_LORA, N_HEADS, half), F32),
                          wq[:, :, QK_NOPE + half:], jnp.zeros((Q_LORA, N_HEADS, half), F32)], axis=2)
    wkv = w_ukv[l].reshape(KV_LORA, N_HEADS, QK_NOPE + V_DIM)
    inv = ROPE_BASE ** (-jnp.arange(half, dtype=F32) / half)
    ang = jnp.arange(seq, dtype=F32)[:, None] * inv[None, :]
    cos, sin, zs = jnp.cos(ang), jnp.sin(ang), jnp.zeros((seq, half), F32)
    wr_hi = w_router[l].astype(BF16)
    wr_lo = (w_router[l] - wr_hi.astype(F32)).astype(BF16)
    wr = (jnp.zeros((D_MODEL, 2 * LANES), BF16).at[:, :N_EXPERTS].set(wr_hi)
          .at[:, LANES:LANES + N_EXPERTS].set(wr_lo))
    return {
        "w_in": repack(wi).astype(BF16), "b_in": repack(bi),
        "ln_v_g": ln_v_g[l][None, :], "ln_v_b": ln_v_b[l][None, :],
        "w_s": w_spatial[l].astype(BF16),
        "b_s": jnp.broadcast_to(b_spatial[l][:, :, None], (GM_GROUPS, CHUNK, CHUNK)),
        "w_oa": w_oa[l].astype(BF16),
        "q_norm_g": q_norm_g[l][None, :], "w_uq": wq.reshape(Q_LORA, N_HEADS * HEAD_W).T.astype(BF16),
        "kv_norm_g": kv_norm_g[l][None, :],
        "w_uk": wkv[:, :, :QK_NOPE].reshape(KV_LORA, N_HEADS * QK_NOPE).astype(BF16),
        "w_uvt": wkv[:, :, QK_NOPE:].reshape(KV_LORA, N_HEADS * V_DIM).T.astype(BF16),
        "cos": jnp.concatenate([cos, zs, cos, zs], axis=1), "sin": jnp.concatenate([-sin, zs, sin, zs], axis=1),
        "w_ob": w_ob[l].astype(BF16), "w_out": w_out[l].astype(BF16),
        "ln1_g": ln1_g[l][None, :], "ln1_b": ln1_b[l][None, :], "w_router": wr,
        "ln2_g": ln2_g[l][None, :], "ln2_b": ln2_b[l][None, :],
    }


def _layer(x, c, l, alpha, w_ada, b_ada, w1, w3, w2, packed):
    bsz, seq, _ = x.shape
    t = bsz * seq
    cap = EC_CAPACITY_FACTOR * t // N_EXPERTS
    x2 = x.reshape(t, D_MODEL)
    mod3 = _mod(c, w_ada[l], b_ada[l])
    u, vn, q, k, vt, sga, sgb = _inproj(x2, mod3, packed, bsz, seq)
    ap = _spatial(u, vn, sga, packed)
    o = _attention(q, k, vt).reshape(t, D_MODEL)
    x1, h2, aff3 = _post(x2, mod3, o, ap, sgb, packed, bsz, seq, alpha)
    nb = t // TM
    spos, flat, cum, tokp, cnt = _route(aff3, cap)
    idx, gate, dst = _slots(cum[:, :, 0], spos, aff3, flat, cap)
    yrows = _moe(idx, dst, gate, h2, w1[l], w3[l], w2[l], cap)
    tokp_col = tokp[:, 0, :].reshape(t, 1)
    cnt_col = cnt[:, 0, :].reshape(t, 1)
    pb, pc, pv = _pair_schedule(tokp[:, 0, 0], N_EXPERTS * cap, nb)
    y = _combine(pb, pc, pv, yrows, tokp_col, cnt_col, x1, mod3, packed, seq, alpha)
    return y.reshape(bsz, seq, D_MODEL)


def kernel(x_prompt, x_sample, c_prompt, c_sample, w_ada, b_ada, w_in, b_in, ln_v_g, ln_v_b, w_spatial, b_spatial,
           w_oa, q_norm_g, w_uq, kv_norm_g, w_ukv, w_ob, w_out, ln1_g, ln1_b, w_router, w1, w3, w2, ln2_g, ln2_b):
    depth = w_ada.shape[0]
    alpha = (2.0 * depth) ** 0.25
    w1, w3, w2 = w1.astype(BF16), w3.astype(BF16), w2.astype(BF16)
    outs = []
    for x, c in ((x_prompt, c_prompt), (x_sample, c_sample)):
        for l in range(depth):
            packed = _pack_weights(l, x.shape[1], w_in, b_in, ln_v_g, ln_v_b, w_spatial, b_spatial, w_oa,
                                   q_norm_g, w_uq, kv_norm_g, w_ukv, w_ob, w_out, ln1_g, ln1_b, w_router,
                                   ln2_g, ln2_b)
            x = _layer(x, c, l, alpha, w_ada, b_ada, w1, w3, w2, packed)
        outs.append(x)
    return tuple(outs)
```

```python
import functools
import math

import jax
import jax.numpy as jnp
from jax import lax
from jax.experimental import pallas as pl
from jax.experimental.pallas import tpu as pltpu

F32 = jnp.float32
BF16 = jnp.bfloat16
I32 = jnp.int32

D_MODEL = 1024
GM_WIDTH = 1024
GM_GROUPS = 8
CHUNK = 128
N_HEADS = 8
QK_NOPE = 128
QK_ROPE = 64
V_DIM = 128
V_ROWS = 144
Q_LORA = 384
KV_LORA = 256
ROPE_BASE = 10000.0
N_EXPERTS = 16
EC_CAPACITY_FACTOR = 2
D_FF = 2048
LN_EPS = 1e-5
RMS_EPS = 1e-6

LANES = 128
SUBLANES = 8
HEAD_W = 256
ROW_TILES = D_MODEL // LANES

C_U, C_V, C_CQ, C_CKV, C_KR, C_GA, C_GB, C_END = 0, 1024, 2048, 2432, 2688, 2816, 3840, 4864

TM_IN = 512
TQ = 2048
TM = 256
TM_WIDE = 512
SLOT_TILE = 256
FF_TILE = 512
M_CHUNK = 512
MOE_ROWS = 1024
VMEM_LIMIT = 56 * 1024 * 1024


def _cparams(sem):
    return pltpu.CompilerParams(dimension_semantics=sem, vmem_limit_bytes=VMEM_LIMIT)


def _ln(x):
    mu = jnp.mean(x, axis=-1, keepdims=True)
    xc = x - mu
    var = jnp.mean(xc * xc, axis=-1, keepdims=True)
    return xc * lax.rsqrt(var + LN_EPS)


def _rms(x):
    return x * lax.rsqrt(jnp.mean(x * x, axis=-1, keepdims=True) + RMS_EPS)


def _gelu(x):
    return 0.5 * x * (1.0 + lax.erf(x * (2.0 ** -0.5)))


def _dot(a, b):
    return jnp.dot(a, b, preferred_element_type=F32)


def _dot_nt(a, b):
    return lax.dot_general(a, b, (((1,), (1,)), ((), ())), preferred_element_type=F32)


def _mod_kernel(c_ref, w_ref, b_ref, o_ref):
    c = c_ref[...]
    s = c * jax.nn.sigmoid(c)
    o_ref[...] = jnp.dot(s, w_ref[...], preferred_element_type=F32,
                         precision=lax.Precision.HIGHEST) + b_ref[...]


def _mod(c, w_ada, b_ada):
    bsz = c.shape[0]
    bp = -(-bsz // SUBLANES) * SUBLANES
    cp = jnp.zeros((bp, D_MODEL), F32).at[:bsz].set(c)
    n = w_ada.shape[1]
    tn = 1024
    out = pl.pallas_call(
        _mod_kernel,
        grid=(n // tn,),
        in_specs=[pl.BlockSpec((bp, D_MODEL), lambda j: (0, 0)),
                  pl.BlockSpec((D_MODEL, tn), lambda j: (0, j)),
                  pl.BlockSpec((1, tn), lambda j: (0, j))],
        out_specs=pl.BlockSpec((bp, tn), lambda j: (0, j)),
        out_shape=jax.ShapeDtypeStruct((bp, n), F32),
        compiler_params=_cparams(("arbitrary",)),
        name="mod",
    )(cp, w_ada, b_ada.reshape(1, n))
    return out[:bsz].reshape(bsz, 6, D_MODEL)


def _inproj_kernel(x_ref, mod_ref, w_ref, b_ref, lvg_ref, lvb_ref, qg_ref, wuq_ref, kvg_ref,
                   wuk_ref, wuvt_ref, cos_ref, sin_ref, cost_ref, sint_ref,
                   u_ref, vn_ref, q_ref, k_ref, vt_ref, sga_ref, sgb_ref, *, qscale):
    x = x_ref[...]
    sh1 = mod_ref[0, 0:1, :]
    sc1 = mod_ref[0, 1:2, :]
    h = (_ln(x) * (1.0 + sc1) + sh1).astype(BF16)

    def proj(a, b):
        return _dot(h, w_ref[:, a:b]) + b_ref[:, a:b]

    cos = cos_ref[...]
    sin = sin_ref[...]

    def rope(t):
        return t * cos + pltpu.roll(t, 64, 1) * sin

    u_ref[...] = _gelu(proj(C_U, C_V)).astype(BF16)
    v = _gelu(proj(C_V, C_CQ))
    vn_ref[...] = (_ln(v) * lvg_ref[...] + lvb_ref[...]).astype(BF16)

    cqn = (_rms(proj(C_CQ, C_CKV)) * qg_ref[...]).astype(BF16)
    qt = _dot_nt(wuq_ref[...], cqn) * qscale
    cost = cost_ref[...]
    sint = sint_ref[...]
    for hh in range(N_HEADS):
        o = hh * HEAD_W
        q_ref[0, hh, 0:QK_NOPE, :] = qt[o:o + QK_NOPE, :].astype(BF16)
        r = qt[o + QK_NOPE:o + HEAD_W, :]
        swapped = jnp.concatenate([r[64:], r[:64]], axis=0)
        q_ref[0, hh, QK_NOPE:HEAD_W, :] = (r * cost + swapped * sint).astype(BF16)

    ckvn = (_rms(proj(C_CKV, C_KR)) * kvg_ref[...]).astype(BF16)
    kn = _dot(ckvn, wuk_ref[...])
    kr = rope(proj(C_KR, C_GA)).astype(BF16)
    for hh in range(N_HEADS):
        k_ref[0, hh, :, 0:QK_NOPE] = kn[:, hh * QK_NOPE:(hh + 1) * QK_NOPE].astype(BF16)
        k_ref[0, hh, :, QK_NOPE:HEAD_W] = kr
    vt = _dot_nt(wuvt_ref[...], ckvn)
    tm = vt.shape[1]
    tail = jnp.where(lax.broadcasted_iota(I32, (V_ROWS - V_DIM, tm), 0) == 0, 1.0, 0.0).astype(BF16)
    for hh in range(N_HEADS):
        vt_ref[0, hh, 0, 0:V_DIM, :] = vt[hh * V_DIM:(hh + 1) * V_DIM, :].astype(BF16)
        vt_ref[0, hh, 0, V_DIM:V_ROWS, :] = tail

    sga_ref[...] = jax.nn.sigmoid(proj(C_GA, C_GB)).astype(BF16)
    sgb_ref[...] = jax.nn.sigmoid(proj(C_GB, C_END)).astype(BF16)


def _inproj(x2, mod3, wp, bsz, seq):
    t = bsz * seq
    tm = min(TM_IN, seq)
    spb = seq // tm
    full = lambda a: pl.BlockSpec(a.shape, lambda i: (0,) * a.ndim)
    qscale = (QK_NOPE + QK_ROPE) ** -0.5 * math.log2(math.e)
    row = pl.BlockSpec((tm, D_MODEL), lambda i: (i, 0))
    qk_spec = pl.BlockSpec((1, N_HEADS, tm, HEAD_W), lambda i: (i // spb, 0, i % spb, 0))
    qt_spec = pl.BlockSpec((1, N_HEADS, HEAD_W, tm), lambda i: (i // spb, 0, 0, i % spb))
    ins = [x2, mod3, wp["w_in"], wp["b_in"], wp["ln_v_g"], wp["ln_v_b"], wp["q_norm_g"], wp["w_uq"],
           wp["kv_norm_g"], wp["w_uk"], wp["w_uvt"], wp["cos"], wp["sin"], wp["cos"].T, wp["sin"].T]
    in_specs = [row, pl.BlockSpec((1, 6, D_MODEL), lambda i: (i // spb, 0, 0))]
    in_specs += [full(a) for a in ins[2:11]]
    in_specs += [pl.BlockSpec((tm, LANES), lambda i: (i % spb, 0))] * 2
    in_specs += [pl.BlockSpec((LANES, tm), lambda i: (0, i % spb))] * 2
    return pl.pallas_call(
        functools.partial(_inproj_kernel, qscale=qscale),
        grid=(t // tm,),
        in_specs=in_specs,
        out_specs=[row, row, qt_spec, qk_spec,
                   pl.BlockSpec((1, N_HEADS, 1, V_ROWS, tm), lambda i: (i // spb, 0, i % spb, 0, 0)),
                   row, row],
        out_shape=[jax.ShapeDtypeStruct((t, D_MODEL), BF16),
                   jax.ShapeDtypeStruct((t, D_MODEL), BF16),
                   jax.ShapeDtypeStruct((bsz, N_HEADS, HEAD_W, seq), BF16),
                   jax.ShapeDtypeStruct((bsz, N_HEADS, seq, HEAD_W), BF16),
                   jax.ShapeDtypeStruct((bsz, N_HEADS, spb, V_ROWS, tm), BF16),
                   jax.ShapeDtypeStruct((t, D_MODEL), BF16),
                   jax.ShapeDtypeStruct((t, D_MODEL), BF16)],
        compiler_params=_cparams(("arbitrary",)),
        name="inproj",
    )(*ins)


def _spatial_kernel(u_ref, vn_ref, ws_ref, bs_ref, woa_ref, sga_ref, o_ref, g_ref):
    tm = u_ref.shape[0]
    nc = tm // CHUNK
    for g in range(GM_GROUPS):
        c0 = g * CHUNK
        rhs = jnp.concatenate([vn_ref[n * CHUNK:(n + 1) * CHUNK, c0:c0 + CHUNK] for n in range(nc)], axis=1)
        mixed = _dot(ws_ref[g], rhs)
        bias = bs_ref[g]
        for n in range(nc):
            m = mixed[:, n * CHUNK:(n + 1) * CHUNK] + bias
            uu = u_ref[n * CHUNK:(n + 1) * CHUNK, c0:c0 + CHUNK].astype(F32)
            g_ref[n * CHUNK:(n + 1) * CHUNK, c0:c0 + CHUNK] = (uu * m).astype(BF16)
    ya = _dot(g_ref[...], woa_ref[...])
    o_ref[...] = (sga_ref[...].astype(F32) * ya).astype(BF16)


def _spatial(u, vn, sga, wp):
    t = u.shape[0]
    tm = TM_WIDE
    row = pl.BlockSpec((tm, D_MODEL), lambda i: (i, 0))
    full = lambda a: pl.BlockSpec(a.shape, lambda i: (0,) * a.ndim)
    return pl.pallas_call(
        _spatial_kernel,
        grid=(t // tm,),
        in_specs=[row, row, full(wp["w_s"]), full(wp["b_s"]), full(wp["w_oa"]), row],
        out_specs=row,
        out_shape=jax.ShapeDtypeStruct((t, D_MODEL), BF16),
        scratch_shapes=[pltpu.VMEM((tm, GM_WIDTH), BF16)],
        compiler_params=_cparams(("arbitrary",)),
        name="spatial",
    )(u, vn, wp["w_s"], wp["b_s"], wp["w_oa"], sga)


def _attn_kernel(q_ref, k_ref, vt_ref, o_ref, acc_ref, sa_ref, sb_ref, pa_ref, pb_ref, *, nk, tk):
    q = q_ref[0, 0]
    tq = q.shape[1]
    acc_ref[...] = jnp.zeros_like(acc_ref)

    def scores(c, s_ref):
        s = _dot(k_ref[0, 0, pl.ds(pl.multiple_of(c * tk, tk), tk), :], q)
        s_ref[...] = s
        return jnp.max(s.reshape(tk // SUBLANES, SUBLANES, tq), axis=0)

    def softmax(s_ref, p_ref, m, cmax):
        m_new = jnp.maximum(m, jnp.max(cmax, axis=0, keepdims=True))
        p_ref[...] = jnp.exp2(s_ref[...] - m_new).astype(BF16)
        return m_new, jnp.exp2(m - m_new)

    def accumulate(alpha, c, p_ref):
        return alpha * acc_ref[...] + _dot(vt_ref[0, 0, c], p_ref[...])

    cmax_a = scores(0, sa_ref)
    cmax_b = scores(1, sb_ref)
    m, alpha = softmax(sa_ref, pa_ref, jnp.full((1, tq), -jnp.inf, F32), cmax_a)

    def body(i, carry):
        m, alpha, cmax_b = carry
        h = 2 * i + 1
        cmax_a = scores(h + 1, sa_ref)
        acc_ref[...] = accumulate(alpha, h - 1, pa_ref)
        m, alpha = softmax(sb_ref, pb_ref, m, cmax_b)
        cmax_b = scores(h + 2, sb_ref)
        acc_ref[...] = accumulate(alpha, h, pb_ref)
        m, alpha = softmax(sa_ref, pa_ref, m, cmax_a)
        return m, alpha, cmax_b

    m, alpha, cmax_b = lax.fori_loop(0, nk // 2 - 1, body, (m, alpha, cmax_b))
    acc_ref[...] = accumulate(alpha, nk - 2, pa_ref)
    m, alpha = softmax(sb_ref, pb_ref, m, cmax_b)
    acc = accumulate(alpha, nk - 1, pb_ref)
    o = acc[0:V_DIM, :] / acc[V_DIM:V_DIM + 1, :]
    o_ref[0] = o.T.astype(BF16)


def _attention(q, k, vt):
    bsz, nh, _, seq = q.shape
    nk, tk = vt.shape[2], vt.shape[4]
    assert nk % 2 == 0
    tq = min(TQ, seq)
    return pl.pallas_call(
        functools.partial(_attn_kernel, nk=nk, tk=tk),
        grid=(bsz, nh, seq // tq),
        in_specs=[pl.BlockSpec((1, 1, HEAD_W, tq), lambda b, h, i: (b, h, 0, i)),
                  pl.BlockSpec((1, 1, seq, HEAD_W), lambda b, h, i: (b, h, 0, 0)),
                  pl.BlockSpec((1, 1, nk, V_ROWS, tk), lambda b, h, i: (b, h, 0, 0, 0))],
        out_specs=pl.BlockSpec((1, tq, V_DIM), lambda b, h, i: (b, i, h)),
        out_shape=jax.ShapeDtypeStruct((bsz, seq, nh * V_DIM), BF16),
        scratch_shapes=[pltpu.VMEM((V_ROWS, tq), F32), pltpu.VMEM((tk, tq), F32), pltpu.VMEM((tk, tq), F32),
                        pltpu.VMEM((tk, tq), BF16), pltpu.VMEM((tk, tq), BF16)],
        compiler_params=_cparams(("arbitrary", "arbitrary", "arbitrary")),
        name="attn",
    )(q, k, vt)


def _post_kernel(x_ref, mod_ref, o_ref, ap_ref, sgb_ref, wob_ref, wout_ref, g1_ref, b1_ref, wr_ref,
                 x1_ref, h2_ref, aff_ref, *, alpha):
    yb = _dot(o_ref[...], wob_ref[...])
    mixin = (ap_ref[...].astype(F32) + sgb_ref[...].astype(F32) * yb).astype(BF16)
    mix = _dot(mixin, wout_ref[...])
    gate1 = mod_ref[0, 2:3, :]
    sh2 = mod_ref[0, 3:4, :]
    sc2 = mod_ref[0, 4:5, :]
    x1 = _ln(alpha * x_ref[...] + gate1 * mix) * g1_ref[...] + b1_ref[...]
    x1_ref[...] = x1
    h2 = _ln(x1) * (1.0 + sc2) + sh2
    tm = h2.shape[0]
    for kk in range(ROW_TILES):
        h2_ref[pl.ds(kk, tm, stride=ROW_TILES), :] = h2[:, kk * LANES:(kk + 1) * LANES]
    h_hi = h2.astype(BF16)
    h_lo = (h2 - h_hi.astype(F32)).astype(BF16)
    prod = _dot(h_hi, wr_ref[...]) + _dot(h_lo, wr_ref[...])
    logits = prod[:, 0:LANES] + prod[:, LANES:2 * LANES]
    lane = lax.broadcasted_iota(I32, logits.shape, 1)
    logits = jnp.where(lane < N_EXPERTS, logits, -jnp.inf)
    ex = jnp.exp(logits - jnp.max(logits, axis=-1, keepdims=True))
    aff = ex / jnp.sum(ex, axis=-1, keepdims=True)
    afft = aff.T
    for j in range(tm // TM):
        aff_ref[j] = afft[0:N_EXPERTS, j * TM:(j + 1) * TM]


def _post(x2, mod3, o, ap, sgb, wp, bsz, seq, alpha):
    t = x2.shape[0]
    tm = TM_WIDE
    spb = seq // tm
    row = pl.BlockSpec((tm, D_MODEL), lambda i: (i, 0))
    full = lambda a: pl.BlockSpec(a.shape, lambda i: (0,) * a.ndim)
    return pl.pallas_call(
        functools.partial(_post_kernel, alpha=alpha),
        grid=(t // tm,),
        in_specs=[row, pl.BlockSpec((1, 6, D_MODEL), lambda i: (i // spb, 0, 0)), row, row, row,
                  full(wp["w_ob"]), full(wp["w_out"]), full(wp["ln1_g"]), full(wp["ln1_b"]),
                  full(wp["w_router"])],
        out_specs=[row, pl.BlockSpec((tm * ROW_TILES, LANES), lambda i: (i, 0)),
                   pl.BlockSpec((tm // TM, N_EXPERTS, TM), lambda i: (i, 0, 0))],
        out_shape=[jax.ShapeDtypeStruct((t, D_MODEL), F32),
                   jax.ShapeDtypeStruct((t * ROW_TILES, LANES), F32),
                   jax.ShapeDtypeStruct((t // TM, N_EXPERTS, TM), F32)],
        compiler_params=_cparams(("arbitrary",)),
        name="post",
    )(x2, mod3, o, ap, sgb, wp["w_ob"], wp["w_out"], wp["ln1_g"], wp["ln1_b"], wp["w_router"])


def _route_kernel(aff_ref, spos_ref, flat_ref, cum_ref, tokp_ref, cnt_ref, m_ref, p_ref, *, cap, nb):
    bits = pltpu.bitcast(aff_ref[...], I32)
    tm = bits.shape[2]

    def count(mask):
        c = jnp.sum(jnp.where(mask, 1, 0), axis=0, keepdims=True)
        return jnp.sum(c, axis=2, keepdims=True)

    def bisect(i, thr):
        cand = thr | lax.shift_left(jnp.int32(1), 30 - i)
        return jnp.where(count(bits >= cand) >= cap, cand, thr)

    thr = lax.fori_loop(0, 31, bisect, jnp.zeros((1, N_EXPERTS, 1), I32))
    gt = bits > thr
    eq = bits == thr
    need = cap - count(gt)

    r = lax.broadcasted_iota(I32, (tm, tm), 0)
    c = lax.broadcasted_iota(I32, (tm, tm), 1)
    upper = jnp.where(r <= c, 1.0, 0.0).astype(BF16)
    er = lax.broadcasted_iota(I32, (N_EXPERTS, N_EXPERTS), 0)
    ec = lax.broadcasted_iota(I32, (N_EXPERTS, N_EXPERTS), 1)
    below = jnp.where(ec < er, 1.0, 0.0).astype(F32)

    def prefix(write_cum):
        def body(j, carry):
            blk = m_ref[j]
            incl = _dot(blk.astype(BF16), upper)
            p_ref[j] = incl - blk + carry
            if write_cum:
                cum_ref[j] = jnp.broadcast_to(carry, (N_EXPERTS, LANES)).astype(I32)
            return carry + jnp.sum(blk, axis=1, keepdims=True)
        return lax.fori_loop(0, nb, body, jnp.zeros((N_EXPERTS, 1), F32))

    m_ref[...] = jnp.where(eq, 1.0, 0.0)
    prefix(False)
    sel = gt | (eq & (p_ref[...] < need.astype(F32)))
    self32 = jnp.where(sel, 1.0, 0.0)
    m_ref[...] = self32
    total = prefix(True)
    cum_ref[nb] = jnp.broadcast_to(total, (N_EXPERTS, LANES)).astype(I32)
    spos_ref[...] = jnp.where(sel, p_ref[...].astype(I32), -1)

    cnt = jnp.sum(self32, axis=1, keepdims=True)
    cnt_ref[...] = jnp.broadcast_to(cnt, cnt_ref.shape).astype(I32)
    m_ref[...] = jnp.broadcast_to(cnt, m_ref.shape)
    prefix(False)
    tokp_ref[...] = p_ref[...].astype(I32)

    def within(j, _):
        blk = jnp.where(spos_ref[j] >= 0, 1.0, 0.0)
        w = jnp.dot(below, blk, preferred_element_type=F32)
        flat_ref[j] = (p_ref[j] + w).astype(I32)
        return 0
    lax.fori_loop(0, nb, within, 0)


def _route(aff3, cap):
    nb, ne, tm = aff3.shape
    shp = jax.ShapeDtypeStruct((nb, ne, tm), I32)
    return pl.pallas_call(
        functools.partial(_route_kernel, cap=cap, nb=nb),
        out_shape=[shp, shp, jax.ShapeDtypeStruct((nb + 1, ne, LANES), I32), shp, shp],
        scratch_shapes=[pltpu.VMEM((nb, ne, tm), F32), pltpu.VMEM((nb, ne, tm), F32)],
        compiler_params=pltpu.CompilerParams(vmem_limit_bytes=VMEM_LIMIT),
        name="route",
    )(aff3)


def _slots_kernel(blo_ref, bhi_ref, spos_ref, aff_ref, flat_ref, idx_ref, gate_ref, dst_ref, *, tm, nt):
    e = pl.program_id(0)
    j = pl.program_id(1)
    slot = j * SLOT_TILE + lax.broadcasted_iota(I32, (SLOT_TILE, 1), 0)
    lane = lax.broadcasted_iota(I32, (1, tm), 1).astype(F32)
    zero = jnp.zeros((1, tm), F32)
    lo = blo_ref[e * nt + j]
    hi = bhi_ref[e * nt + j]

    def block(b, off):
        onehot = jnp.where(slot == spos_ref[b, pl.ds(e, 1), :] + off, 1.0, 0.0).astype(BF16)
        g = aff_ref[b, pl.ds(e, 1), :]
        g0 = g.astype(BF16).astype(F32)
        g1 = (g - g0).astype(BF16).astype(F32)
        g2 = g - g0 - g1
        fl = flat_ref[b, pl.ds(e, 1), :]
        rows = [lane, (jnp.zeros((1, tm), I32) + b).astype(F32), g0, g1, g2,
                lax.shift_right_logical(fl, 8).astype(F32), (fl & 255).astype(F32), zero]
        vals = jnp.concatenate(rows + [zero] * 8, axis=0).astype(BF16)
        return _dot_nt(vals, onehot)

    def body(i, acc):
        b0 = lo + 2 * i
        b1 = b0 + 1
        return acc + block(b0, 0) + block(jnp.minimum(b1, hi - 1), jnp.where(b1 < hi, 0, -(1 << 30)))

    a = lax.fori_loop(0, (hi - lo + 1) // 2, body, jnp.zeros((16, SLOT_TILE), F32))
    idx_ref[0] = (a[1:2] * tm + a[0:1]).astype(I32)
    gate_ref[0] = a[2:3] + a[3:4] + a[4:5]
    dst_ref[0] = (a[5:6] * 256.0 + a[6:7]).astype(I32)


def _slots(cum, spos, aff3, flat, cap):
    nb, ne, tm = spos.shape
    nt = cap // SLOT_TILE
    edges = jnp.arange(nt + 1, dtype=I32) * SLOT_TILE
    cs, ce = cum[:-1].T, cum[1:].T
    blo = jnp.sum(ce[:, None, :] <= edges[None, :-1, None], axis=2).astype(I32).reshape(-1)
    bhi = jnp.sum(cs[:, None, :] < edges[None, 1:, None], axis=2).astype(I32).reshape(-1)
    full = lambda a: pl.BlockSpec(a.shape, lambda e, j, lo, hi: (0,) * a.ndim)
    out = pl.BlockSpec((1, 1, SLOT_TILE), lambda e, j, lo, hi: (e * nt + j, 0, 0))
    return pl.pallas_call(
        functools.partial(_slots_kernel, tm=tm, nt=nt),
        grid_spec=pltpu.PrefetchScalarGridSpec(
            num_scalar_prefetch=2,
            grid=(ne, nt),
            in_specs=[full(spos), full(aff3), full(flat)],
            out_specs=[out, out, out]),
        out_shape=[jax.ShapeDtypeStruct((ne * nt, 1, SLOT_TILE), I32),
                   jax.ShapeDtypeStruct((ne * nt, 1, SLOT_TILE), F32),
                   jax.ShapeDtypeStruct((ne * nt, 1, SLOT_TILE), I32)],
        compiler_params=_cparams(("arbitrary", "arbitrary")),
        name="slots",
    )(blo, bhi, spos, aff3, flat)


def _moe_kernel(idx_ref, idxn_ref, dstp_ref, dst_ref, gate_ref, w1_ref, w3_ref, w2_ref, h2_hbm, y_hbm,
                rin_ref, rout_ref, xe_ref, acc_ref, sem, *, ch, nf, mch):
    f = pl.program_id(2)
    blk = pl.program_id(0) * pl.num_programs(1) + pl.program_id(1)
    nblk = pl.num_programs(0) * pl.num_programs(1)
    par = blk & 1
    nm = ch // mch
    per_chunk = ch // (nf * nm)
    rows = ch * ROW_TILES

    def gather_row(ids_ref, s, buf):
        t = ids_ref[0, 0, s]
        return pltpu.make_async_copy(h2_hbm.at[pl.ds(pl.multiple_of(t * ROW_TILES, ROW_TILES), ROW_TILES), :],
                                     rin_ref.at[buf, pl.ds(pl.multiple_of(s * ROW_TILES, ROW_TILES), ROW_TILES), :],
                                     sem.at[buf])

    def gather_all(buf):
        return pltpu.make_async_copy(h2_hbm.at[pl.ds(0, rows), :], rin_ref.at[buf], sem.at[buf])

    def scatter_row(ids_ref, s):
        d = ids_ref[0, 0, s]
        return pltpu.make_async_copy(rout_ref.at[pl.ds(pl.multiple_of(s * ROW_TILES, ROW_TILES), ROW_TILES), :],
                                     y_hbm.at[pl.ds(pl.multiple_of(d * ROW_TILES, ROW_TILES), ROW_TILES), :],
                                     sem.at[2])

    def scatter_all():
        return pltpu.make_async_copy(rout_ref, y_hbm.at[pl.ds(0, rows), :], sem.at[2])

    @pl.when((blk == 0) & (f == 0))
    def _():
        rout_ref[...] = jnp.zeros_like(rout_ref)

        def start(s, _):
            gather_row(idx_ref, s, 0).start()
            return 0
        lax.fori_loop(0, ch, start, 0)

    @pl.when(f == 0)
    def _():
        gather_all(par).wait()
        for mc in range(nm):
            base = mc * mch * ROW_TILES
            parts = [rin_ref[par, pl.ds(base + kk, mch, stride=ROW_TILES), :] for kk in range(ROW_TILES)]
            xe_ref[mc * mch:(mc + 1) * mch, :] = jnp.concatenate(parts, axis=1).astype(BF16)

    for mc in range(nm):
        sl = slice(mc * mch, (mc + 1) * mch)
        xm = xe_ref[sl, :]
        a = _dot(xm, w1_ref[0])
        hid = (a * jax.nn.sigmoid(a) * _dot(xm, w3_ref[0])).astype(BF16)
        part = _dot(hid, w2_ref[0])
        first = (f * nm + mc) * per_chunk
        for i in range(per_chunk):
            gather_row(idxn_ref, first + i, 1 - par).start()
            scatter_row(dstp_ref, first + i).start()

        @pl.when(f == 0)
        def _():
            acc_ref[sl, :] = part

        @pl.when(f > 0)
        def _():
            acc_ref[sl, :] += part

    @pl.when(f == nf - 1)
    def _():
        scatter_all().wait()
        for mc in range(nm):
            sl = slice(mc * mch, (mc + 1) * mch)
            ye = acc_ref[sl, :] * gate_ref[sl, :]
            base = mc * mch * ROW_TILES
            for kk in range(ROW_TILES):
                rout_ref[pl.ds(base + kk, mch, stride=ROW_TILES), :] = ye[:, kk * LANES:(kk + 1) * LANES]

        @pl.when(blk == nblk - 1)
        def _():
            def start(s, _):
                scatter_row(dst_ref, s).start()
                return 0
            lax.fori_loop(0, ch, start, 0)
            scatter_all().wait()
            gather_all(1 - par).wait()


def _moe(idx, dst, gate, h2rows, w1, w3, w2, cap):
    ne = N_EXPERTS
    ch = min(MOE_ROWS, cap)
    nh = cap // ch
    nf = D_FF // FF_TILE
    mch = min(M_CHUNK, ch)
    assert ch % (nf * (ch // mch)) == 0
    smem = lambda: pl.BlockSpec((1, 1, ch), lambda e, h, f: (e * nh + h, 0, 0), memory_space=pltpu.SMEM)
    smem_next = pl.BlockSpec((1, 1, ch), lambda e, h, f: (jnp.minimum(e * nh + h + 1, ne * nh - 1), 0, 0),
                             memory_space=pltpu.SMEM)
    smem_prev = pl.BlockSpec((1, 1, ch), lambda e, h, f: (jnp.maximum(e * nh + h - 1, 0), 0, 0),
                             memory_space=pltpu.SMEM)
    return pl.pallas_call(
        functools.partial(_moe_kernel, ch=ch, nf=nf, mch=mch),
        grid=(ne, nh, nf),
        in_specs=[smem(), smem_next, smem_prev, smem(),
                  pl.BlockSpec((ch, 1), lambda e, h, f: (e * nh + h, 0)),
                  pl.BlockSpec((1, D_MODEL, FF_TILE), lambda e, h, f: (e, 0, f)),
                  pl.BlockSpec((1, D_MODEL, FF_TILE), lambda e, h, f: (e, 0, f)),
                  pl.BlockSpec((1, FF_TILE, D_MODEL), lambda e, h, f: (e, f, 0)),
                  pl.BlockSpec(memory_space=pl.ANY)],
        out_specs=pl.BlockSpec(memory_space=pl.ANY),
        out_shape=jax.ShapeDtypeStruct((ne * cap * ROW_TILES, LANES), F32),
        scratch_shapes=[pltpu.VMEM((2, ch * ROW_TILES, LANES), F32),
                        pltpu.VMEM((ch * ROW_TILES, LANES), F32),
                        pltpu.VMEM((ch, D_MODEL), BF16),
                        pltpu.VMEM((ch, D_MODEL), F32),
                        pltpu.SemaphoreType.DMA((3,))],
        compiler_params=_cparams(("arbitrary", "arbitrary", "arbitrary")),
        name="moe",
    )(idx.reshape(ne * nh, 1, ch), idx.reshape(ne * nh, 1, ch), dst.reshape(ne * nh, 1, ch),
      dst.reshape(ne * nh, 1, ch), gate.reshape(ne * cap, 1), w1, w3, w2, h2rows)


def _combine_kernel(pb_ref, pc_ref, pv_ref, y_ref, tokp_ref, cnt_ref, x1_ref, mod_ref, g2_ref, b2_ref,
                    o_ref, acc_ref, *, alpha):
    k = pl.program_id(0)
    blk = pb_ref[k]
    prev = pb_ref[jnp.maximum(k - 1, 0)]
    nxt = pb_ref[jnp.minimum(k + 1, pl.num_programs(0) - 1)]
    valid = pv_ref[k] == 1
    first = valid & ((k == 0) | (prev != blk))
    last = valid & ((k == pl.num_programs(0) - 1) | (nxt != blk) | (pv_ref[jnp.minimum(k + 1, pl.num_programs(0) - 1)] == 0))

    @pl.when(first)
    def _():
        acc_ref[...] = jnp.zeros_like(acc_ref)

    @pl.when(valid)
    def _():
        parts = [y_ref[pl.ds(kk, SLOT_TILE, stride=ROW_TILES), :] for kk in range(ROW_TILES)]
        rows = jnp.concatenate(parts, axis=1).astype(BF16)
        r = pc_ref[k] * SLOT_TILE + lax.broadcasted_iota(I32, (1, SLOT_TILE), 1)
        start = tokp_ref[...]
        own = jnp.where((r >= start) & (r < start + cnt_ref[...]), 1.0, 0.0).astype(BF16)
        acc_ref[...] += _dot(own, rows)

    @pl.when(last)
    def _():
        gate2 = mod_ref[0, 5:6, :]
        o_ref[...] = _ln(alpha * x1_ref[...] + gate2 * acc_ref[...]) * g2_ref[...] + b2_ref[...]


def _combine(pb, pc, pv, yrows, tokp_col, cnt_col, x1, mod3, wp, seq, alpha):
    t = x1.shape[0]
    tm = TM
    spb = seq // tm
    npairs = pb.shape[0]
    full = lambda a: pl.BlockSpec(a.shape, lambda k, pb, pc, pv: (0,) * a.ndim)
    row = pl.BlockSpec((tm, D_MODEL), lambda k, pb, pc, pv: (pb[k], 0))
    colspec = pl.BlockSpec((tm, 1), lambda k, pb, pc, pv: (pb[k], 0))
    return pl.pallas_call(
        functools.partial(_combine_kernel, alpha=alpha),
        grid_spec=pltpu.PrefetchScalarGridSpec(
            num_scalar_prefetch=3,
            grid=(npairs,),
            in_specs=[pl.BlockSpec((SLOT_TILE * ROW_TILES, LANES), lambda k, pb, pc, pv: (pc[k], 0)),
                      colspec, colspec, row,
                      pl.BlockSpec((1, 6, D_MODEL), lambda k, pb, pc, pv: (pb[k] // spb, 0, 0)),
                      full(wp["ln2_g"]), full(wp["ln2_b"])],
            out_specs=row,
            scratch_shapes=[pltpu.VMEM((tm, D_MODEL), F32)]),
        out_shape=jax.ShapeDtypeStruct((t, D_MODEL), F32),
        compiler_params=_cparams(("arbitrary",)),
        name="combine",
    )(pb, pc, pv, yrows, tokp_col, cnt_col, x1, mod3, wp["ln2_g"], wp["ln2_b"])


def _pair_schedule(row_start, n_rows_total, nb):
    nchunks = n_rows_total // SLOT_TILE
    npairs = nb + nchunks
    rs = row_start
    re = jnp.concatenate([rs[1:], jnp.array([n_rows_total], I32)])
    c_lo = jnp.minimum(rs // SLOT_TILE, nchunks - 1)
    c_hi = jnp.maximum(c_lo, (re - 1) // SLOT_TILE)
    n_b = c_hi - c_lo + 1
    ends = jnp.cumsum(n_b)
    starts = ends - n_b
    k = jnp.arange(npairs, dtype=I32)
    valid = k < ends[-1]
    kk = jnp.minimum(k, ends[-1] - 1)
    b = jnp.sum(ends[None, :] <= kk[:, None], axis=1).astype(I32)
    c = c_lo[b] + (kk - starts[b])
    return b.astype(I32), c.astype(I32), valid.astype(I32)


def _pack_weights(l, seq, w_in, b_in, ln_v_g, ln_v_b, w_spatial, b_spatial, w_oa, q_norm_g, w_uq, kv_norm_g,
                  w_ukv, w_ob, w_out, ln1_g, ln1_b, w_router, ln2_g, ln2_b):
    half = QK_ROPE // 2
    z32 = lambda rows: jnp.zeros((rows, half), F32)

    def rope_cols(w):
        return jnp.concatenate([w[:, :half], z32(w.shape[0]), w[:, half:], z32(w.shape[0])], axis=1)

    wi, bi = w_in[l], b_in[l][None, :]
    off_cq, off_ckv, off_kr = 2 * GM_WIDTH, 2 * GM_WIDTH + Q_LORA, 2 * GM_WIDTH + Q_LORA + KV_LORA
    off_ga = off_kr + QK_ROPE

    def repack(a):
        return jnp.concatenate([a[:, :off_kr], rope_cols(a[:, off_kr:off_ga]), a[:, off_ga:]], axis=1)

    wq = w_uq[l].reshape(Q_LORA, N_HEADS, QK_NOPE + QK_ROPE)
    wq = jnp.concatenate([wq[:, :, :QK_NOPE],
                          wq[:, :, QK_NOPE:QK_NOPE + half], jnp.zeros((Q_LORA, N_HEADS, half), F32),
                          wq[:, :, QK_NOPE + half:], jnp.zeros((Q_LORA, N_HEADS, half), F32)], axis=2)
    wkv = w_ukv[l].reshape(KV_LORA, N_HEADS, QK_NOPE + V_DIM)
    inv = ROPE_BASE ** (-jnp.arange(half, dtype=F32) / half)
    ang = jnp.arange(seq, dtype=F32)[:, None] * inv[None, :]
    cos, sin, zs = jnp.cos(ang), jnp.sin(ang), jnp.zeros((seq, half), F32)
    wr_hi = w_router[l].astype(BF16)
    wr_lo = (w_router[l] - wr_hi.astype(F32)).astype(BF16)
    wr = (jnp.zeros((D_MODEL, 2 * LANES), BF16).at[:, :N_EXPERTS].set(wr_hi)
          .at[:, LANES:LANES + N_EXPERTS].set(wr_lo))
    return {
        "w_in": repack(wi).astype(BF16), "b_in": repack(bi),
        "ln_v_g": ln_v_g[l][None, :], "ln_v_b": ln_v_b[l][None, :],
        "w_s": w_spatial[l].astype(BF16),
        "b_s": jnp.broadcast_to(b_spatial[l][:, :, None], (GM_GROUPS, CHUNK, CHUNK)),
        "w_oa": w_oa[l].astype(BF16),
        "q_norm_g": q_norm_g[l][None, :], "w_uq": wq.reshape(Q_LORA, N_HEADS * HEAD_W).T.astype(BF16),
        "kv_norm_g": kv_norm_g[l][None, :],
        "w_uk": wkv[:, :, :QK_NOPE].reshape(KV_LORA, N_HEADS * QK_NOPE).astype(BF16),
        "w_uvt": wkv[:, :, QK_NOPE:].reshape(KV_LORA, N_HEADS * V_DIM).T.astype(BF16),
        "cos": jnp.concatenate([cos, zs, cos, zs], axis=1), "sin": jnp.concatenate([-sin, zs, sin, zs], axis=1),
        "w_ob": w_ob[l].astype(BF16), "w_out": w_out[l].astype(BF16),
        "ln1_g": ln1_g[l][None, :], "ln1_b": ln1_b[l][None, :], "w_router": wr,
        "ln2_g": ln2_g[l][None, :], "ln2_b": ln2_b[l][None, :],
    }


def _layer(x, c, l, alpha, w_ada, b_ada, w1, w3, w2, packed):
    bsz, seq, _ = x.shape
    t = bsz * seq
    cap = EC_CAPACITY_FACTOR * t // N_EXPERTS
    x2 = x.reshape(t, D_MODEL)
    mod3 = _mod(c, w_ada[l], b_ada[l])
    u, vn, q, k, vt, sga, sgb = _inproj(x2, mod3, packed, bsz, seq)
    ap = _spatial(u, vn, sga, packed)
    o = _attention(q, k, vt).reshape(t, D_MODEL)
    x1, h2, aff3 = _post(x2, mod3, o, ap, sgb, packed, bsz, seq, alpha)
    nb = t // TM
    spos, flat, cum, tokp, cnt = _route(aff3, cap)
    idx, gate, dst = _slots(cum[:, :, 0], spos, aff3, flat, cap)
    yrows = _moe(idx, dst, gate, h2, w1[l], w3[l], w2[l], cap)
    tokp_col = tokp[:, 0, :].reshape(t, 1)
    cnt_col = cnt[:, 0, :].reshape(t, 1)
    pb, pc, pv = _pair_schedule(tokp[:, 0, 0], N_EXPERTS * cap, nb)
    y = _combine(pb, pc, pv, yrows, tokp_col, cnt_col, x1, mod3, packed, seq, alpha)
    return y.reshape(bsz, seq, D_MODEL)


def kernel(x_prompt, x_sample, c_prompt, c_sample, w_ada, b_ada, w_in, b_in, ln_v_g, ln_v_b, w_spatial, b_spatial,
           w_oa, q_norm_g, w_uq, kv_norm_g, w_ukv, w_ob, w_out, ln1_g, ln1_b, w_router, w1, w3, w2, ln2_g, ln2_b):
    depth = w_ada.shape[0]
    alpha = (2.0 * depth) ** 0.25
    w1, w3, w2 = w1.astype(BF16), w3.astype(BF16), w2.astype(BF16)
    outs = []
    for x, c in ((x_prompt, c_prompt), (x_sample, c_sample)):
        for l in range(depth):
            packed = _pack_weights(l, x.shape[1], w_in, b_in, ln_v_g, ln_v_b, w_spatial, b_spatial, w_oa,
                                   q_norm_g, w_uq, kv_norm_g, w_ukv, w_ob, w_out, ln1_g, ln1_b, w_router,
                                   ln2_g, ln2_b)
            x = _layer(x, c, l, alpha, w_ada, b_ada, w1, w3, w2, packed)
        outs.append(x)
    return tuple(outs)
```

```python
import functools
import math

import jax
import jax.numpy as jnp
from jax import lax
from jax.experimental import pallas as pl
from jax.experimental.pallas import tpu as pltpu

F32 = jnp.float32
BF16 = jnp.bfloat16
I32 = jnp.int32

D_MODEL = 1024
GM_WIDTH = 1024
GM_GROUPS = 8
CHUNK = 128
N_HEADS = 8
QK_NOPE = 128
QK_ROPE = 64
V_DIM = 128
V_ROWS = 144
Q_LORA = 384
KV_LORA = 256
ROPE_BASE = 10000.0
N_EXPERTS = 16
EC_CAPACITY_FACTOR = 2
D_FF = 2048
LN_EPS = 1e-5
RMS_EPS = 1e-6

LANES = 128
SUBLANES = 8
HEAD_W = 256
ROW_TILES = D_MODEL // LANES

C_U, C_V, C_CQ, C_CKV, C_KR, C_GA, C_GB, C_END = 0, 1024, 2048, 2432, 2688, 2816, 3840, 4864

TM_IN = 512
TQ = 2048
TM = 256
TM_WIDE = 512
SLOT_TILE = 256
SLOT_TILES_PER_STEP = 4
FF_TILE = 2048
M_CHUNK = 512
MOE_ROWS = 1024
VMEM_LIMIT = 56 * 1024 * 1024


def _cparams(sem):
    return pltpu.CompilerParams(dimension_semantics=sem, vmem_limit_bytes=VMEM_LIMIT)


def _ln(x):
    mu = jnp.mean(x, axis=-1, keepdims=True)
    xc = x - mu
    var = jnp.mean(xc * xc, axis=-1, keepdims=True)
    return xc * lax.rsqrt(var + LN_EPS)


def _rms(x):
    return x * lax.rsqrt(jnp.mean(x * x, axis=-1, keepdims=True) + RMS_EPS)


def _gelu(x):
    return 0.5 * x * (1.0 + lax.erf(x * (2.0 ** -0.5)))


def _dot(a, b):
    return jnp.dot(a, b, preferred_element_type=F32)


def _dot_nt(a, b):
    return lax.dot_general(a, b, (((1,), (1,)), ((), ())), preferred_element_type=F32)


def _mod_kernel(c_ref, w_ref, b_ref, o_ref):
    c = c_ref[...]
    s = c * jax.nn.sigmoid(c)
    o_ref[...] = jnp.dot(s, w_ref[...], preferred_element_type=F32,
                         precision=lax.Precision.HIGHEST) + b_ref[...]


def _mod(c, w_ada, b_ada):
    bsz = c.shape[0]
    bp = -(-bsz // SUBLANES) * SUBLANES
    cp = jnp.zeros((bp, D_MODEL), F32).at[:bsz].set(c)
    n = w_ada.shape[1]
    tn = 1024
    out = pl.pallas_call(
        _mod_kernel,
        grid=(n // tn,),
        in_specs=[pl.BlockSpec((bp, D_MODEL), lambda j: (0, 0)),
                  pl.BlockSpec((D_MODEL, tn), lambda j: (0, j)),
                  pl.BlockSpec((1, tn), lambda j: (0, j))],
        out_specs=pl.BlockSpec((bp, tn), lambda j: (0, j)),
        out_shape=jax.ShapeDtypeStruct((bp, n), F32),
        compiler_params=_cparams(("arbitrary",)),
        name="mod",
    )(cp, w_ada, b_ada.reshape(1, n))
    return out[:bsz].reshape(bsz, 6, D_MODEL)


def _inproj_kernel(x_ref, mod_ref, w_ref, b_ref, lvg_ref, lvb_ref, qg_ref, wuq_ref, kvg_ref,
                   wuk_ref, wuvt_ref, cos_ref, sin_ref, cost_ref, sint_ref,
                   u_ref, vn_ref, q_ref, k_ref, vt_ref, sga_ref, sgb_ref, *, qscale):
    x = x_ref[...]
    sh1 = mod_ref[0, 0:1, :]
    sc1 = mod_ref[0, 1:2, :]
    h = (_ln(x) * (1.0 + sc1) + sh1).astype(BF16)

    def proj(a, b):
        return _dot(h, w_ref[:, a:b]) + b_ref[:, a:b]

    cos = cos_ref[...]
    sin = sin_ref[...]

    def rope(t):
        return t * cos + pltpu.roll(t, 64, 1) * sin

    u_ref[...] = _gelu(proj(C_U, C_V)).astype(BF16)
    v = _gelu(proj(C_V, C_CQ))
    vn_ref[...] = (_ln(v) * lvg_ref[...] + lvb_ref[...]).astype(BF16)

    cqn = (_rms(proj(C_CQ, C_CKV)) * qg_ref[...]).astype(BF16)
    qt = _dot_nt(wuq_ref[...], cqn) * qscale
    cost = cost_ref[...]
    sint = sint_ref[...]
    for hh in range(N_HEADS):
        o = hh * HEAD_W
        q_ref[0, hh, 0:QK_NOPE, :] = qt[o:o + QK_NOPE, :].astype(BF16)
        r = qt[o + QK_NOPE:o + HEAD_W, :]
        swapped = jnp.concatenate([r[64:], r[:64]], axis=0)
        q_ref[0, hh, QK_NOPE:HEAD_W, :] = (r * cost + swapped * sint).astype(BF16)

    ckvn = (_rms(proj(C_CKV, C_KR)) * kvg_ref[...]).astype(BF16)
    kn = _dot(ckvn, wuk_ref[...])
    kr = rope(proj(C_KR, C_GA)).astype(BF16)
    for hh in range(N_HEADS):
        k_ref[0, hh, :, 0:QK_NOPE] = kn[:, hh * QK_NOPE:(hh + 1) * QK_NOPE].astype(BF16)
        k_ref[0, hh, :, QK_NOPE:HEAD_W] = kr
    vt = _dot_nt(wuvt_ref[...], ckvn)
    tm = vt.shape[1]
    tail = jnp.where(lax.broadcasted_iota(I32, (V_ROWS - V_DIM, tm), 0) == 0, 1.0, 0.0).astype(BF16)
    for hh in range(N_HEADS):
        vt_ref[0, hh, 0, 0:V_DIM, :] = vt[hh * V_DIM:(hh + 1) * V_DIM, :].astype(BF16)
        vt_ref[0, hh, 0, V_DIM:V_ROWS, :] = tail

    sga_ref[...] = jax.nn.sigmoid(proj(C_GA, C_GB)).astype(BF16)
    sgb_ref[...] = jax.nn.sigmoid(proj(C_GB, C_END)).astype(BF16)


def _inproj(x2, mod3, wp, bsz, seq):
    t = bsz * seq
    tm = min(TM_IN, seq)
    spb = seq // tm
    full = lambda a: pl.BlockSpec(a.shape, lambda i: (0,) * a.ndim)
    qscale = (QK_NOPE + QK_ROPE) ** -0.5 * math.log2(math.e)
    row = pl.BlockSpec((tm, D_MODEL), lambda i: (i, 0))
    qk_spec = pl.BlockSpec((1, N_HEADS, tm, HEAD_W), lambda i: (i // spb, 0, i % spb, 0))
    qt_spec = pl.BlockSpec((1, N_HEADS, HEAD_W, tm), lambda i: (i // spb, 0, 0, i % spb))
    ins = [x2, mod3, wp["w_in"], wp["b_in"], wp["ln_v_g"], wp["ln_v_b"], wp["q_norm_g"], wp["w_uq"],
           wp["kv_norm_g"], wp["w_uk"], wp["w_uvt"], wp["cos"], wp["sin"], wp["cos"].T, wp["sin"].T]
    in_specs = [row, pl.BlockSpec((1, 6, D_MODEL), lambda i: (i // spb, 0, 0))]
    in_specs += [full(a) for a in ins[2:11]]
    in_specs += [pl.BlockSpec((tm, LANES), lambda i: (i % spb, 0))] * 2
    in_specs += [pl.BlockSpec((LANES, tm), lambda i: (0, i % spb))] * 2
    return pl.pallas_call(
        functools.partial(_inproj_kernel, qscale=qscale),
        grid=(t // tm,),
        in_specs=in_specs,
        out_specs=[row, row, qt_spec, qk_spec,
                   pl.BlockSpec((1, N_HEADS, 1, V_ROWS, tm), lambda i: (i // spb, 0, i % spb, 0, 0)),
                   row, row],
        out_shape=[jax.ShapeDtypeStruct((t, D_MODEL), BF16),
                   jax.ShapeDtypeStruct((t, D_MODEL), BF16),
                   jax.ShapeDtypeStruct((bsz, N_HEADS, HEAD_W, seq), BF16),
                   jax.ShapeDtypeStruct((bsz, N_HEADS, seq, HEAD_W), BF16),
                   jax.ShapeDtypeStruct((bsz, N_HEADS, spb, V_ROWS, tm), BF16),
                   jax.ShapeDtypeStruct((t, D_MODEL), BF16),
                   jax.ShapeDtypeStruct((t, D_MODEL), BF16)],
        compiler_params=_cparams(("arbitrary",)),
        name="inproj",
    )(*ins)


def _spatial_kernel(u_ref, vn_ref, ws_ref, bs_ref, woa_ref, sga_ref, o_ref, g_ref):
    tm = u_ref.shape[0]
    nc = tm // CHUNK
    for g in range(GM_GROUPS):
        c0 = g * CHUNK
        rhs = jnp.concatenate([vn_ref[n * CHUNK:(n + 1) * CHUNK, c0:c0 + CHUNK] for n in range(nc)], axis=1)
        mixed = _dot(ws_ref[g], rhs)
        bias = bs_ref[g]
        for n in range(nc):
            m = mixed[:, n * CHUNK:(n + 1) * CHUNK] + bias
            uu = u_ref[n * CHUNK:(n + 1) * CHUNK, c0:c0 + CHUNK].astype(F32)
            g_ref[n * CHUNK:(n + 1) * CHUNK, c0:c0 + CHUNK] = (uu * m).astype(BF16)
    ya = _dot(g_ref[...], woa_ref[...])
    o_ref[...] = (sga_ref[...].astype(F32) * ya).astype(BF16)


def _spatial(u, vn, sga, wp):
    t = u.shape[0]
    tm = TM_WIDE
    row = pl.BlockSpec((tm, D_MODEL), lambda i: (i, 0))
    full = lambda a: pl.BlockSpec(a.shape, lambda i: (0,) * a.ndim)
    return pl.pallas_call(
        _spatial_kernel,
        grid=(t // tm,),
        in_specs=[row, row, full(wp["w_s"]), full(wp["b_s"]), full(wp["w_oa"]), row],
        out_specs=row,
        out_shape=jax.ShapeDtypeStruct((t, D_MODEL), BF16),
        scratch_shapes=[pltpu.VMEM((tm, GM_WIDTH), BF16)],
        compiler_params=_cparams(("arbitrary",)),
        name="spatial",
    )(u, vn, wp["w_s"], wp["b_s"], wp["w_oa"], sga)


def _attn_kernel(q_ref, k_ref, vt_ref, o_ref, acc_ref, sa_ref, sb_ref, pa_ref, pb_ref, *, nk, tk):
    q = q_ref[0, 0]
    tq = q.shape[1]
    acc_ref[...] = jnp.zeros_like(acc_ref)

    def scores(c, s_ref):
        s = _dot(k_ref[0, 0, pl.ds(pl.multiple_of(c * tk, tk), tk), :], q)
        s_ref[...] = s
        return jnp.max(s.reshape(tk // SUBLANES, SUBLANES, tq), axis=0)

    def softmax(s_ref, p_ref, m, cmax):
        m_new = jnp.maximum(m, jnp.max(cmax, axis=0, keepdims=True))
        p_ref[...] = jnp.exp2(s_ref[...] - m_new).astype(BF16)
        return m_new, jnp.exp2(m - m_new)

    def accumulate(alpha, c, p_ref):
        return alpha * acc_ref[...] + _dot(vt_ref[0, 0, c], p_ref[...])

    cmax_a = scores(0, sa_ref)
    cmax_b = scores(1, sb_ref)
    m, alpha = softmax(sa_ref, pa_ref, jnp.full((1, tq), -jnp.inf, F32), cmax_a)

    def body(i, carry):
        m, alpha, cmax_b = carry
        h = 2 * i + 1
        cmax_a = scores(h + 1, sa_ref)
        acc_ref[...] = accumulate(alpha, h - 1, pa_ref)
        m, alpha = softmax(sb_ref, pb_ref, m, cmax_b)
        cmax_b = scores(h + 2, sb_ref)
        acc_ref[...] = accumulate(alpha, h, pb_ref)
        m, alpha = softmax(sa_ref, pa_ref, m, cmax_a)
        return m, alpha, cmax_b

    m, alpha, cmax_b = lax.fori_loop(0, nk // 2 - 1, body, (m, alpha, cmax_b))
    acc_ref[...] = accumulate(alpha, nk - 2, pa_ref)
    m, alpha = softmax(sb_ref, pb_ref, m, cmax_b)
    acc = accumulate(alpha, nk - 1, pb_ref)
    o = acc[0:V_DIM, :] / acc[V_DIM:V_DIM + 1, :]
    o_ref[0] = o.T.astype(BF16)


def _attention(q, k, vt):
    bsz, nh, _, seq = q.shape
    nk, tk = vt.shape[2], vt.shape[4]
    assert nk % 2 == 0
    tq = min(TQ, seq)
    return pl.pallas_call(
        functools.partial(_attn_kernel, nk=nk, tk=tk),
        grid=(bsz, nh, seq // tq),
        in_specs=[pl.BlockSpec((1, 1, HEAD_W, tq), lambda b, h, i: (b, h, 0, i)),
                  pl.BlockSpec((1, 1, seq, HEAD_W), lambda b, h, i: (b, h, 0, 0)),
                  pl.BlockSpec((1, 1, nk, V_ROWS, tk), lambda b, h, i: (b, h, 0, 0, 0))],
        out_specs=pl.BlockSpec((1, tq, V_DIM), lambda b, h, i: (b, i, h)),
        out_shape=jax.ShapeDtypeStruct((bsz, seq, nh * V_DIM), BF16),
        scratch_shapes=[pltpu.VMEM((V_ROWS, tq), F32), pltpu.VMEM((tk, tq), F32), pltpu.VMEM((tk, tq), F32),
                        pltpu.VMEM((tk, tq), BF16), pltpu.VMEM((tk, tq), BF16)],
        compiler_params=_cparams(("arbitrary", "arbitrary", "arbitrary")),
        name="attn",
    )(q, k, vt)


def _post_kernel(x_ref, mod_ref, o_ref, ap_ref, sgb_ref, wob_ref, wout_ref, g1_ref, b1_ref, wr_ref,
                 x1_ref, h2_ref, aff_ref, *, alpha):
    gate1 = mod_ref[0, 2:3, :]
    sh2 = mod_ref[0, 3:4, :]
    sc2 = mod_ref[0, 4:5, :]
    for j in range(x_ref.shape[0] // TM):
        rows = slice(j * TM, (j + 1) * TM)
        yb = _dot(o_ref[rows, :], wob_ref[...])
        mixin = (ap_ref[rows, :].astype(F32) + sgb_ref[rows, :].astype(F32) * yb).astype(BF16)
        mix = _dot(mixin, wout_ref[...])
        x1 = _ln(alpha * x_ref[rows, :] + gate1 * mix) * g1_ref[...] + b1_ref[...]
        x1_ref[rows, :] = x1
        h2 = _ln(x1) * (1.0 + sc2) + sh2
        for kk in range(ROW_TILES):
            h2_ref[pl.ds(j * TM * ROW_TILES + kk, TM, stride=ROW_TILES), :] = h2[:, kk * LANES:(kk + 1) * LANES]
        h_hi = h2.astype(BF16)
        h_lo = (h2 - h_hi.astype(F32)).astype(BF16)
        prod = _dot(h_hi, wr_ref[...]) + _dot(h_lo, wr_ref[...])
        logits = prod[:, 0:LANES] + prod[:, LANES:2 * LANES]
        lane = lax.broadcasted_iota(I32, logits.shape, 1)
        logits = jnp.where(lane < N_EXPERTS, logits, -jnp.inf)
        ex = jnp.exp(logits - jnp.max(logits, axis=-1, keepdims=True))
        aff = ex / jnp.sum(ex, axis=-1, keepdims=True)
        aff_ref[j] = aff.T[0:N_EXPERTS, :]


def _post(x2, mod3, o, ap, sgb, wp, bsz, seq, alpha):
    t = x2.shape[0]
    tm = TM_WIDE
    spb = seq // tm
    row = pl.BlockSpec((tm, D_MODEL), lambda i: (i, 0))
    full = lambda a: pl.BlockSpec(a.shape, lambda i: (0,) * a.ndim)
    return pl.pallas_call(
        functools.partial(_post_kernel, alpha=alpha),
        grid=(t // tm,),
        in_specs=[row, pl.BlockSpec((1, 6, D_MODEL), lambda i: (i // spb, 0, 0)), row, row, row,
                  full(wp["w_ob"]), full(wp["w_out"]), full(wp["ln1_g"]), full(wp["ln1_b"]),
                  full(wp["w_router"])],
        out_specs=[row, pl.BlockSpec((tm * ROW_TILES, LANES), lambda i: (i, 0)),
                   pl.BlockSpec((tm // TM, N_EXPERTS, TM), lambda i: (i, 0, 0))],
        out_shape=[jax.ShapeDtypeStruct((t, D_MODEL), F32),
                   jax.ShapeDtypeStruct((t * ROW_TILES, LANES), F32),
                   jax.ShapeDtypeStruct((t // TM, N_EXPERTS, TM), F32)],
        compiler_params=_cparams(("arbitrary",)),
        name="post",
    )(x2, mod3, o, ap, sgb, wp["w_ob"], wp["w_out"], wp["ln1_g"], wp["ln1_b"], wp["w_router"])


def _route_kernel(aff_ref, spos_ref, flat_ref, cum_ref, tokp_ref, cnt_ref, m_ref, p_ref, *, cap, nb):
    bits = pltpu.bitcast(aff_ref[...], I32)
    tm = bits.shape[2]

    def count(mask):
        c = jnp.sum(jnp.where(mask, 1, 0), axis=0, keepdims=True)
        return jnp.sum(c, axis=2, keepdims=True)

    def bisect(i, thr):
        cand = thr | lax.shift_left(jnp.int32(1), 30 - i)
        return jnp.where(count(bits >= cand) >= cap, cand, thr)

    thr = lax.fori_loop(0, 31, bisect, jnp.zeros((1, N_EXPERTS, 1), I32))
    gt = bits > thr
    eq = bits == thr
    need = cap - count(gt)

    r = lax.broadcasted_iota(I32, (tm, tm), 0)
    c = lax.broadcasted_iota(I32, (tm, tm), 1)
    upper = jnp.where(r <= c, 1.0, 0.0).astype(BF16)
    er = lax.broadcasted_iota(I32, (N_EXPERTS, N_EXPERTS), 0)
    ec = lax.broadcasted_iota(I32, (N_EXPERTS, N_EXPERTS), 1)
    below = jnp.where(ec < er, 1.0, 0.0).astype(F32)

    def prefix(write_cum):
        def body(j, carry):
            blk = m_ref[j]
            incl = _dot(blk.astype(BF16), upper)
            p_ref[j] = incl - blk + carry
            if write_cum:
                cum_ref[j] = jnp.broadcast_to(carry, (N_EXPERTS, LANES)).astype(I32)
            return carry + jnp.sum(blk, axis=1, keepdims=True)
        return lax.fori_loop(0, nb, body, jnp.zeros((N_EXPERTS, 1), F32))

    m_ref[...] = jnp.where(eq, 1.0, 0.0)
    prefix(False)
    sel = gt | (eq & (p_ref[...] < need.astype(F32)))
    self32 = jnp.where(sel, 1.0, 0.0)
    m_ref[...] = self32
    total = prefix(True)
    cum_ref[nb] = jnp.broadcast_to(total, (N_EXPERTS, LANES)).astype(I32)
    spos_ref[...] = jnp.where(sel, p_ref[...].astype(I32), -1)

    cnt = jnp.sum(self32, axis=1, keepdims=True)
    cnt_ref[...] = jnp.broadcast_to(cnt, cnt_ref.shape).astype(I32)
    m_ref[...] = jnp.broadcast_to(cnt, m_ref.shape)
    prefix(False)
    tokp_ref[...] = p_ref[...].astype(I32)

    def within(j, _):
        blk = jnp.where(spos_ref[j] >= 0, 1.0, 0.0)
        w = jnp.dot(below, blk, preferred_element_type=F32)
        flat_ref[j] = (p_ref[j] + w).astype(I32)
        return 0
    lax.fori_loop(0, nb, within, 0)


def _route(aff3, cap):
    nb, ne, tm = aff3.shape
    shp = jax.ShapeDtypeStruct((nb, ne, tm), I32)
    return pl.pallas_call(
        functools.partial(_route_kernel, cap=cap, nb=nb),
        out_shape=[shp, shp, jax.ShapeDtypeStruct((nb + 1, ne, LANES), I32), shp, shp],
        scratch_shapes=[pltpu.VMEM((nb, ne, tm), F32), pltpu.VMEM((nb, ne, tm), F32)],
        compiler_params=pltpu.CompilerParams(vmem_limit_bytes=VMEM_LIMIT),
        name="route",
    )(aff3)


def _slots_kernel(blo_ref, bhi_ref, spos_ref, aff_ref, flat_ref, idx_ref, gate_ref, dst_ref, *, tm, nt, tps):
    e = pl.program_id(0)
    for jj in range(tps):
        _slot_tile(e, pl.program_id(1) * tps + jj, jj, blo_ref, bhi_ref, spos_ref, aff_ref, flat_ref,
                   idx_ref, gate_ref, dst_ref, tm, nt)


def _slot_tile(e, j, jj, blo_ref, bhi_ref, spos_ref, aff_ref, flat_ref, idx_ref, gate_ref, dst_ref, tm, nt):
    slot = j * SLOT_TILE + lax.broadcasted_iota(I32, (SLOT_TILE, 1), 0)
    lane = lax.broadcasted_iota(I32, (1, tm), 1).astype(F32)
    zero = jnp.zeros((1, tm), F32)
    lo = blo_ref[e * nt + j]
    hi = bhi_ref[e * nt + j]

    def block(b, off):
        onehot = jnp.where(slot == spos_ref[b, pl.ds(e, 1), :] + off, 1.0, 0.0).astype(BF16)
        g = aff_ref[b, pl.ds(e, 1), :]
        g0 = g.astype(BF16).astype(F32)
        g1 = (g - g0).astype(BF16).astype(F32)
        g2 = g - g0 - g1
        fl = flat_ref[b, pl.ds(e, 1), :]
        rows = [lane, (jnp.zeros((1, tm), I32) + b).astype(F32), g0, g1, g2,
                lax.shift_right_logical(fl, 8).astype(F32), (fl & 255).astype(F32), zero]
        vals = jnp.concatenate(rows + [zero] * 8, axis=0).astype(BF16)
        return _dot_nt(vals, onehot)

    def body(i, acc):
        b0 = lo + 2 * i
        b1 = b0 + 1
        return acc + block(b0, 0) + block(jnp.minimum(b1, hi - 1), jnp.where(b1 < hi, 0, -(1 << 30)))

    a = lax.fori_loop(0, (hi - lo + 1) // 2, body, jnp.zeros((16, SLOT_TILE), F32))
    idx_ref[jj] = (a[1:2] * tm + a[0:1]).astype(I32)
    gate_ref[jj] = a[2:3] + a[3:4] + a[4:5]
    dst_ref[jj] = (a[5:6] * 256.0 + a[6:7]).astype(I32)


def _slots(cum, spos, aff3, flat, cap):
    nb, ne, tm = spos.shape
    nt = cap // SLOT_TILE
    edges = jnp.arange(nt + 1, dtype=I32) * SLOT_TILE
    cs, ce = cum[:-1].T, cum[1:].T
    blo = jnp.sum(ce[:, None, :] <= edges[None, :-1, None], axis=2).astype(I32).reshape(-1)
    bhi = jnp.sum(cs[:, None, :] < edges[None, 1:, None], axis=2).astype(I32).reshape(-1)
    full = lambda a: pl.BlockSpec(a.shape, lambda e, j, lo, hi: (0,) * a.ndim)
    tps = math.gcd(nt, SLOT_TILES_PER_STEP)
    out = pl.BlockSpec((tps, 1, SLOT_TILE), lambda e, j, lo, hi: (e * (nt // tps) + j, 0, 0))
    return pl.pallas_call(
        functools.partial(_slots_kernel, tm=tm, nt=nt, tps=tps),
        grid_spec=pltpu.PrefetchScalarGridSpec(
            num_scalar_prefetch=2,
            grid=(ne, nt // tps),
            in_specs=[full(spos), full(aff3), full(flat)],
            out_specs=[out, out, out]),
        out_shape=[jax.ShapeDtypeStruct((ne * nt, 1, SLOT_TILE), I32),
                   jax.ShapeDtypeStruct((ne * nt, 1, SLOT_TILE), F32),
                   jax.ShapeDtypeStruct((ne * nt, 1, SLOT_TILE), I32)],
        compiler_params=_cparams(("arbitrary", "arbitrary")),
        name="slots",
    )(blo, bhi, spos, aff3, flat)


def _moe_kernel(idx_ref, idxn_ref, dstp_ref, dst_ref, gate_ref, w1_ref, w3_ref, w2_ref, h2_hbm, y_hbm,
                rin_ref, rout_ref, xe_ref, acc_ref, sem, *, ch, nf, mch):
    f = pl.program_id(2)
    blk = pl.program_id(0) * pl.num_programs(1) + pl.program_id(1)
    nblk = pl.num_programs(0) * pl.num_programs(1)
    par = blk & 1
    nm = ch // mch
    per_chunk = ch // (nf * nm)
    rows = ch * ROW_TILES

    def gather_row(ids_ref, s, buf):
        t = ids_ref[0, 0, s]
        return pltpu.make_async_copy(h2_hbm.at[pl.ds(pl.multiple_of(t * ROW_TILES, ROW_TILES), ROW_TILES), :],
                                     rin_ref.at[buf, pl.ds(pl.multiple_of(s * ROW_TILES, ROW_TILES), ROW_TILES), :],
                                     sem.at[buf])

    def gather_all(buf):
        return pltpu.make_async_copy(h2_hbm.at[pl.ds(0, rows), :], rin_ref.at[buf], sem.at[buf])

    def scatter_row(ids_ref, s):
        d = ids_ref[0, 0, s]
        return pltpu.make_async_copy(rout_ref.at[pl.ds(pl.multiple_of(s * ROW_TILES, ROW_TILES), ROW_TILES), :],
                                     y_hbm.at[pl.ds(pl.multiple_of(d * ROW_TILES, ROW_TILES), ROW_TILES), :],
                                     sem.at[2])

    def scatter_all():
        return pltpu.make_async_copy(rout_ref, y_hbm.at[pl.ds(0, rows), :], sem.at[2])

    @pl.when((blk == 0) & (f == 0))
    def _():
        rout_ref[...] = jnp.zeros_like(rout_ref)

        def start(s, _):
            gather_row(idx_ref, s, 0).start()
            return 0
        lax.fori_loop(0, ch, start, 0)

    @pl.when(f == 0)
    def _():
        gather_all(par).wait()
        for mc in range(nm):
            base = mc * mch * ROW_TILES
            parts = [rin_ref[par, pl.ds(base + kk, mch, stride=ROW_TILES), :] for kk in range(ROW_TILES)]
            xe_ref[mc * mch:(mc + 1) * mch, :] = jnp.concatenate(parts, axis=1).astype(BF16)

    for mc in range(nm):
        sl = slice(mc * mch, (mc + 1) * mch)
        xm = xe_ref[sl, :]
        a = _dot(xm, w1_ref[0])
        hid = (a * jax.nn.sigmoid(a) * _dot(xm, w3_ref[0])).astype(BF16)
        part = _dot(hid, w2_ref[0])
        first = (f * nm + mc) * per_chunk
        for i in range(per_chunk):
            gather_row(idxn_ref, first + i, 1 - par).start()
            scatter_row(dstp_ref, first + i).start()

        @pl.when(f == 0)
        def _():
            acc_ref[sl, :] = part

        @pl.when(f > 0)
        def _():
            acc_ref[sl, :] += part

    @pl.when(f == nf - 1)
    def _():
        scatter_all().wait()
        for mc in range(nm):
            sl = slice(mc * mch, (mc + 1) * mch)
            ye = acc_ref[sl, :] * gate_ref[sl, :]
            base = mc * mch * ROW_TILES
            for kk in range(ROW_TILES):
                rout_ref[pl.ds(base + kk, mch, stride=ROW_TILES), :] = ye[:, kk * LANES:(kk + 1) * LANES]

        @pl.when(blk == nblk - 1)
        def _():
            def start(s, _):
                scatter_row(dst_ref, s).start()
                return 0
            lax.fori_loop(0, ch, start, 0)
            scatter_all().wait()
            gather_all(1 - par).wait()


def _moe(idx, dst, gate, h2rows, w1, w3, w2, cap):
    ne = N_EXPERTS
    ch = min(MOE_ROWS, cap)
    nh = cap // ch
    nf = D_FF // FF_TILE
    mch = min(M_CHUNK, ch)
    assert ch % (nf * (ch // mch)) == 0
    smem = lambda: pl.BlockSpec((1, 1, ch), lambda e, h, f: (e * nh + h, 0, 0), memory_space=pltpu.SMEM)
    smem_next = pl.BlockSpec((1, 1, ch), lambda e, h, f: (jnp.minimum(e * nh + h + 1, ne * nh - 1), 0, 0),
                             memory_space=pltpu.SMEM)
    smem_prev = pl.BlockSpec((1, 1, ch), lambda e, h, f: (jnp.maximum(e * nh + h - 1, 0), 0, 0),
                             memory_space=pltpu.SMEM)
    return pl.pallas_call(
        functools.partial(_moe_kernel, ch=ch, nf=nf, mch=mch),
        grid=(ne, nh, nf),
        in_specs=[smem(), smem_next, smem_prev, smem(),
                  pl.BlockSpec((ch, 1), lambda e, h, f: (e * nh + h, 0)),
                  pl.BlockSpec((1, D_MODEL, FF_TILE), lambda e, h, f: (e, 0, f)),
                  pl.BlockSpec((1, D_MODEL, FF_TILE), lambda e, h, f: (e, 0, f)),
                  pl.BlockSpec((1, FF_TILE, D_MODEL), lambda e, h, f: (e, f, 0)),
                  pl.BlockSpec(memory_space=pl.ANY)],
        out_specs=pl.BlockSpec(memory_space=pl.ANY),
        out_shape=jax.ShapeDtypeStruct((ne * cap * ROW_TILES, LANES), F32),
        scratch_shapes=[pltpu.VMEM((2, ch * ROW_TILES, LANES), F32),
                        pltpu.VMEM((ch * ROW_TILES, LANES), F32),
                        pltpu.VMEM((ch, D_MODEL), BF16),
                        pltpu.VMEM((ch, D_MODEL), F32),
                        pltpu.SemaphoreType.DMA((3,))],
        compiler_params=_cparams(("arbitrary", "arbitrary", "arbitrary")),
        name="moe",
    )(idx.reshape(ne * nh, 1, ch), idx.reshape(ne * nh, 1, ch), dst.reshape(ne * nh, 1, ch),
      dst.reshape(ne * nh, 1, ch), gate.reshape(ne * cap, 1), w1, w3, w2, h2rows)


def _combine_kernel(pb_ref, pc_ref, pv_ref, y_ref, tokp_ref, cnt_ref, x1_ref, mod_ref, g2_ref, b2_ref,
                    o_ref, acc_ref, *, alpha):
    k = pl.program_id(0)
    blk = pb_ref[k]
    prev = pb_ref[jnp.maximum(k - 1, 0)]
    nxt = pb_ref[jnp.minimum(k + 1, pl.num_programs(0) - 1)]
    valid = pv_ref[k] == 1
    first = valid & ((k == 0) | (prev != blk))
    last = valid & ((k == pl.num_programs(0) - 1) | (nxt != blk) | (pv_ref[jnp.minimum(k + 1, pl.num_programs(0) - 1)] == 0))

    @pl.when(first)
    def _():
        acc_ref[...] = jnp.zeros_like(acc_ref)

    @pl.when(valid)
    def _():
        parts = [y_ref[pl.ds(kk, SLOT_TILE, stride=ROW_TILES), :] for kk in range(ROW_TILES)]
        rows = jnp.concatenate(parts, axis=1).astype(BF16)
        r = pc_ref[k] * SLOT_TILE + lax.broadcasted_iota(I32, (1, SLOT_TILE), 1)
        start = tokp_ref[...]
        own = jnp.where((r >= start) & (r < start + cnt_ref[...]), 1.0, 0.0).astype(BF16)
        acc_ref[...] += _dot(own, rows)

    @pl.when(last)
    def _():
        gate2 = mod_ref[0, 5:6, :]
        o_ref[...] = _ln(alpha * x1_ref[...] + gate2 * acc_ref[...]) * g2_ref[...] + b2_ref[...]


def _combine(pb, pc, pv, yrows, tokp_col, cnt_col, x1, mod3, wp, seq, alpha):
    t = x1.shape[0]
    tm = TM
    spb = seq // tm
    npairs = pb.shape[0]
    full = lambda a: pl.BlockSpec(a.shape, lambda k, pb, pc, pv: (0,) * a.ndim)
    row = pl.BlockSpec((tm, D_MODEL), lambda k, pb, pc, pv: (pb[k], 0))
    colspec = pl.BlockSpec((tm, 1), lambda k, pb, pc, pv: (pb[k], 0))
    return pl.pallas_call(
        functools.partial(_combine_kernel, alpha=alpha),
        grid_spec=pltpu.PrefetchScalarGridSpec(
            num_scalar_prefetch=3,
            grid=(npairs,),
            in_specs=[pl.BlockSpec((SLOT_TILE * ROW_TILES, LANES), lambda k, pb, pc, pv: (pc[k], 0)),
                      colspec, colspec, row,
                      pl.BlockSpec((1, 6, D_MODEL), lambda k, pb, pc, pv: (pb[k] // spb, 0, 0)),
                      full(wp["ln2_g"]), full(wp["ln2_b"])],
            out_specs=row,
            scratch_shapes=[pltpu.VMEM((tm, D_MODEL), F32)]),
        out_shape=jax.ShapeDtypeStruct((t, D_MODEL), F32),
        compiler_params=_cparams(("arbitrary",)),
        name="combine",
    )(pb, pc, pv, yrows, tokp_col, cnt_col, x1, mod3, wp["ln2_g"], wp["ln2_b"])


def _pair_schedule(row_start, n_rows_total, nb):
    nchunks = n_rows_total // SLOT_TILE
    npairs = nb + nchunks
    rs = row_start
    re = jnp.concatenate([rs[1:], jnp.array([n_rows_total], I32)])
    c_lo = jnp.minimum(rs // SLOT_TILE, nchunks - 1)
    c_hi = jnp.maximum(c_lo, (re - 1) // SLOT_TILE)
    n_b = c_hi - c_lo + 1
    ends = jnp.cumsum(n_b)
    starts = ends - n_b
    k = jnp.arange(npairs, dtype=I32)
    valid = k < ends[-1]
    kk = jnp.minimum(k, ends[-1] - 1)
    b = jnp.sum(ends[None, :] <= kk[:, None], axis=1).astype(I32)
    c = c_lo[b] + (kk - starts[b])
    return b.astype(I32), c.astype(I32), valid.astype(I32)


def _pack_weights(l, seq, w_in, b_in, ln_v_g, ln_v_b, w_spatial, b_spatial, w_oa, q_norm_g, w_uq, kv_norm_g,
                  w_ukv, w_ob, w_out, ln1_g, ln1_b, w_router, ln2_g, ln2_b):
    half = QK_ROPE // 2
    z32 = lambda rows: jnp.zeros((rows, half), F32)

    def rope_cols(w):
        return jnp.concatenate([w[:, :half], z32(w.shape[0]), w[:, half:], z32(w.shape[0])], axis=1)

    wi, bi = w_in[l], b_in[l][None, :]
    off_cq, off_ckv, off_kr = 2 * GM_WIDTH, 2 * GM_WIDTH + Q_LORA, 2 * GM_WIDTH + Q_LORA + KV_LORA
    off_ga = off_kr + QK_ROPE

    def repack(a):
        return jnp.concatenate([a[:, :off_kr], rope_cols(a[:, off_kr:off_ga]), a[:, off_ga:]], axis=1)

    wq = w_uq[l].reshape(Q_LORA, N_HEADS, QK_NOPE + QK_ROPE)
    wq = jnp.concatenate([wq[:, :, :QK_NOPE],
                          wq[:, :, QK_NOPE:QK_NOPE + half], jnp.zeros((Q_LORA, N_HEADS, half), F32),
                          wq[:, :, QK_NOPE + half:], jnp.zeros((Q_LORA, N_HEADS, half), F32)], axis=2)
    wkv = w_ukv[l].reshape(KV_LORA, N_HEADS, QK_NOPE + V_DIM)
    inv = ROPE_BASE ** (-jnp.arange(half, dtype=F32) / half)
    ang = jnp.arange(seq, dtype=F32)[:, None] * inv[None, :]
    cos, sin, zs = jnp.cos(ang), jnp.sin(ang), jnp.zeros((seq, half), F32)
    wr_hi = w_router[l].astype(BF16)
    wr_lo = (w_router[l] - wr_hi.astype(F32)).astype(BF16)
    wr = (jnp.zeros((D_MODEL, 2 * LANES), BF16).at[:, :N_EXPERTS].set(wr_hi)
          .at[:, LANES:LANES + N_EXPERTS].set(wr_lo))
    return {
        "w_in": repack(wi).astype(BF16), "b_in": repack(bi),
        "ln_v_g": ln_v_g[l][None, :], "ln_v_b": ln_v_b[l][None, :],
        "w_s": w_spatial[l].astype(BF16),
        "b_s": jnp.broadcast_to(b_spatial[l][:, :, None], (GM_GROUPS, CHUNK, CHUNK)),
        "w_oa": w_oa[l].astype(BF16),
        "q_norm_g": q_norm_g[l][None, :], "w_uq": wq.reshape(Q_LORA, N_HEADS * HEAD_W).T.astype(BF16),
        "kv_norm_g": kv_norm_g[l][None, :],
        "w_uk": wkv[:, :, :QK_NOPE].reshape(KV_LORA, N_HEADS * QK_NOPE).astype(BF16),
        "w_uvt": wkv[:, :, QK_NOPE:].reshape(KV_LORA, N_HEADS * V_DIM).T.astype(BF16),
        "cos": jnp.concatenate([cos, zs, cos, zs], axis=1), "sin": jnp.concatenate([-sin, zs, sin, zs], axis=1),
        "w_ob": w_ob[l].astype(BF16), "w_out": w_out[l].astype(BF16),
        "ln1_g": ln1_g[l][None, :], "ln1_b": ln1_b[l][None, :], "w_router": wr,
        "ln2_g": ln2_g[l][None, :], "ln2_b": ln2_b[l][None, :],
    }


def _layer(x, c, l, alpha, w_ada, b_ada, w1, w3, w2, packed):
    bsz, seq, _ = x.shape
    t = bsz * seq
    cap = EC_CAPACITY_FACTOR * t // N_EXPERTS
    x2 = x.reshape(t, D_MODEL)
    mod3 = _mod(c, w_ada[l], b_ada[l])
    u, vn, q, k, vt, sga, sgb = _inproj(x2, mod3, packed, bsz, seq)
    ap = _spatial(u, vn, sga, packed)
    o = _attention(q, k, vt).reshape(t, D_MODEL)
    x1, h2, aff3 = _post(x2, mod3, o, ap, sgb, packed, bsz, seq, alpha)
    nb = t // TM
    spos, flat, cum, tokp, cnt = _route(aff3, cap)
    idx, gate, dst = _slots(cum[:, :, 0], spos, aff3, flat, cap)
    yrows = _moe(idx, dst, gate, h2, w1[l], w3[l], w2[l], cap)
    tokp_col = tokp[:, 0, :].reshape(t, 1)
    cnt_col = cnt[:, 0, :].reshape(t, 1)
    pb, pc, pv = _pair_schedule(tokp[:, 0, 0], N_EXPERTS * cap, nb)
    y = _combine(pb, pc, pv, yrows, tokp_col, cnt_col, x1, mod3, packed, seq, alpha)
    return y.reshape(bsz, seq, D_MODEL)


def kernel(x_prompt, x_sample, c_prompt, c_sample, w_ada, b_ada, w_in, b_in, ln_v_g, ln_v_b, w_spatial, b_spatial,
           w_oa, q_norm_g, w_uq, kv_norm_g, w_ukv, w_ob, w_out, ln1_g, ln1_b, w_router, w1, w3, w2, ln2_g, ln2_b):
    depth = w_ada.shape[0]
    alpha = (2.0 * depth) ** 0.25
    w1, w3, w2 = w1.astype(BF16), w3.astype(BF16), w2.astype(BF16)
    outs = []
    for x, c in ((x_prompt, c_prompt), (x_sample, c_sample)):
        for l in range(depth):
            packed = _pack_weights(l, x.shape[1], w_in, b_in, ln_v_g, ln_v_b, w_spatial, b_spatial, w_oa,
                                   q_norm_g, w_uq, kv_norm_g, w_ukv, w_ob, w_out, ln1_g, ln1_b, w_router,
                                   ln2_g, ln2_b)
            x = _layer(x, c, l, alpha, w_ada, b_ada, w1, w3, w2, packed)
        outs.append(x)
    return tuple(outs)
```

```python
import functools
import math

import jax
import jax.numpy as jnp
from jax import lax
from jax.experimental import pallas as pl
from jax.experimental.pallas import tpu as pltpu

F32 = jnp.float32
BF16 = jnp.bfloat16
I32 = jnp.int32

D_MODEL = 1024
GM_WIDTH = 1024
GM_GROUPS = 8
CHUNK = 128
N_HEADS = 8
QK_NOPE = 128
QK_ROPE = 64
V_DIM = 128
V_ROWS = 144
Q_LORA = 384
KV_LORA = 256
ROPE_BASE = 10000.0
N_EXPERTS = 16
EC_CAPACITY_FACTOR = 2
D_FF = 2048
LN_EPS = 1e-5
RMS_EPS = 1e-6

LANES = 128
SUBLANES = 8
HEAD_W = 256
ROW_TILES = D_MODEL // LANES

C_U, C_V, C_CQ, C_CKV, C_KR, C_GA, C_GB, C_END = 0, 1024, 2048, 2432, 2688, 2816, 3840, 4864

TM_IN = 512
TQ = 2048
TM = 256
TM_WIDE = 512
SLOT_TILE = 256
COMBINE_ROWS = 1024
SLOT_TILES_PER_STEP = 4
SLOT_BLOCKS_PER_TRIP = 6
FF_TILE = 2048
M_CHUNK = 512
MOE_ROWS = 1024
V7X_VMEM_BYTES = 64 * 1024 * 1024
VMEM_LIMIT = V7X_VMEM_BYTES * 7 // 8
BF16_INT_BITS = 8
BF16_INT = 1 << BF16_INT_BITS
NO_SLOT = -(1 << 30)


def _cparams(sem):
    return pltpu.CompilerParams(dimension_semantics=sem, vmem_limit_bytes=VMEM_LIMIT)


def _ln(x):
    mu = jnp.mean(x, axis=-1, keepdims=True)
    xc = x - mu
    var = jnp.mean(xc * xc, axis=-1, keepdims=True)
    return xc * lax.rsqrt(var + LN_EPS)


def _rms(x):
    return x * lax.rsqrt(jnp.mean(x * x, axis=-1, keepdims=True) + RMS_EPS)


def _gelu(x):
    return 0.5 * x * (1.0 + lax.erf(x * (2.0 ** -0.5)))


def _dot(a, b):
    return jnp.dot(a, b, preferred_element_type=F32)


def _dot_nt(a, b):
    return lax.dot_general(a, b, (((1,), (1,)), ((), ())), preferred_element_type=F32)


def _mod_kernel(c_ref, w_ref, b_ref, o_ref):
    c = c_ref[...]
    s = c * jax.nn.sigmoid(c)
    o_ref[...] = jnp.dot(s, w_ref[...], preferred_element_type=F32,
                         precision=lax.Precision.HIGHEST) + b_ref[...]


def _mod(c, w_ada, b_ada):
    bsz = c.shape[0]
    bp = -(-bsz // SUBLANES) * SUBLANES
    cp = jnp.zeros((bp, D_MODEL), F32).at[:bsz].set(c)
    n = w_ada.shape[1]
    tn = 1024
    out = pl.pallas_call(
        _mod_kernel,
        grid=(n // tn,),
        in_specs=[pl.BlockSpec((bp, D_MODEL), lambda j: (0, 0)),
                  pl.BlockSpec((D_MODEL, tn), lambda j: (0, j)),
                  pl.BlockSpec((1, tn), lambda j: (0, j))],
        out_specs=pl.BlockSpec((bp, tn), lambda j: (0, j)),
        out_shape=jax.ShapeDtypeStruct((bp, n), F32),
        compiler_params=_cparams(("arbitrary",)),
        name="mod",
    )(cp, w_ada, b_ada.reshape(1, n))
    return out[:bsz].reshape(bsz, 6, D_MODEL)


def _inproj_kernel(x_ref, mod_ref, w_ref, b_ref, lvg_ref, lvb_ref, qg_ref, wuq_ref, kvg_ref,
                   wuk_ref, wuvt_ref, cos_ref, sin_ref, cost_ref, sint_ref,
                   u_ref, vn_ref, q_ref, k_ref, vt_ref, sga_ref, sgb_ref, *, qscale):
    x = x_ref[...]
    sh1 = mod_ref[0, 0:1, :]
    sc1 = mod_ref[0, 1:2, :]
    h = (_ln(x) * (1.0 + sc1) + sh1).astype(BF16)

    def proj(a, b):
        return _dot(h, w_ref[:, a:b]) + b_ref[:, a:b]

    cos = cos_ref[...]
    sin = sin_ref[...]

    def rope(t):
        return t * cos + pltpu.roll(t, 64, 1) * sin

    u_ref[...] = _gelu(proj(C_U, C_V)).astype(BF16)
    v = _gelu(proj(C_V, C_CQ))
    vn_ref[...] = (_ln(v) * lvg_ref[...] + lvb_ref[...]).astype(BF16)

    cqn = (_rms(proj(C_CQ, C_CKV)) * qg_ref[...]).astype(BF16)
    qt = _dot_nt(wuq_ref[...], cqn) * qscale
    cost = cost_ref[...]
    sint = sint_ref[...]
    for hh in range(N_HEADS):
        o = hh * HEAD_W
        q_ref[0, hh, 0:QK_NOPE, :] = qt[o:o + QK_NOPE, :].astype(BF16)
        r = qt[o + QK_NOPE:o + HEAD_W, :]
        swapped = jnp.concatenate([r[64:], r[:64]], axis=0)
        q_ref[0, hh, QK_NOPE:HEAD_W, :] = (r * cost + swapped * sint).astype(BF16)

    ckvn = (_rms(proj(C_CKV, C_KR)) * kvg_ref[...]).astype(BF16)
    kn = _dot(ckvn, wuk_ref[...])
    kr = rope(proj(C_KR, C_GA)).astype(BF16)
    for hh in range(N_HEADS):
        k_ref[0, hh, :, 0:QK_NOPE] = kn[:, hh * QK_NOPE:(hh + 1) * QK_NOPE].astype(BF16)
        k_ref[0, hh, :, QK_NOPE:HEAD_W] = kr
    vt = _dot_nt(wuvt_ref[...], ckvn)
    tm = vt.shape[1]
    tail = jnp.where(lax.broadcasted_iota(I32, (V_ROWS - V_DIM, tm), 0) == 0, 1.0, 0.0).astype(BF16)
    for hh in range(N_HEADS):
        vt_ref[0, hh, 0, 0:V_DIM, :] = vt[hh * V_DIM:(hh + 1) * V_DIM, :].astype(BF16)
        vt_ref[0, hh, 0, V_DIM:V_ROWS, :] = tail

    sga_ref[...] = jax.nn.sigmoid(proj(C_GA, C_GB)).astype(BF16)
    sgb_ref[...] = jax.nn.sigmoid(proj(C_GB, C_END)).astype(BF16)


def _inproj(x2, mod3, wp, bsz, seq):
    t = bsz * seq
    tm = min(TM_IN, seq)
    spb = seq // tm
    full = lambda a: pl.BlockSpec(a.shape, lambda i: (0,) * a.ndim)
    qscale = (QK_NOPE + QK_ROPE) ** -0.5 * math.log2(math.e)
    row = pl.BlockSpec((tm, D_MODEL), lambda i: (i, 0))
    qk_spec = pl.BlockSpec((1, N_HEADS, tm, HEAD_W), lambda i: (i // spb, 0, i % spb, 0))
    qt_spec = pl.BlockSpec((1, N_HEADS, HEAD_W, tm), lambda i: (i // spb, 0, 0, i % spb))
    ins = [x2, mod3, wp["w_in"], wp["b_in"], wp["ln_v_g"], wp["ln_v_b"], wp["q_norm_g"], wp["w_uq"],
           wp["kv_norm_g"], wp["w_uk"], wp["w_uvt"], wp["cos"], wp["sin"], wp["cos"].T, wp["sin"].T]
    in_specs = [row, pl.BlockSpec((1, 6, D_MODEL), lambda i: (i // spb, 0, 0))]
    in_specs += [full(a) for a in ins[2:11]]
    in_specs += [pl.BlockSpec((tm, LANES), lambda i: (i % spb, 0))] * 2
    in_specs += [pl.BlockSpec((LANES, tm), lambda i: (0, i % spb))] * 2
    return pl.pallas_call(
        functools.partial(_inproj_kernel, qscale=qscale),
        grid=(t // tm,),
        in_specs=in_specs,
        out_specs=[row, row, qt_spec, qk_spec,
                   pl.BlockSpec((1, N_HEADS, 1, V_ROWS, tm), lambda i: (i // spb, 0, i % spb, 0, 0)),
                   row, row],
        out_shape=[jax.ShapeDtypeStruct((t, D_MODEL), BF16),
                   jax.ShapeDtypeStruct((t, D_MODEL), BF16),
                   jax.ShapeDtypeStruct((bsz, N_HEADS, HEAD_W, seq), BF16),
                   jax.ShapeDtypeStruct((bsz, N_HEADS, seq, HEAD_W), BF16),
                   jax.ShapeDtypeStruct((bsz, N_HEADS, spb, V_ROWS, tm), BF16),
                   jax.ShapeDtypeStruct((t, D_MODEL), BF16),
                   jax.ShapeDtypeStruct((t, D_MODEL), BF16)],
        compiler_params=_cparams(("arbitrary",)),
        name="inproj",
    )(*ins)


def _spatial_kernel(u_ref, vn_ref, ws_ref, bs_ref, woa_ref, sga_ref, o_ref, g_ref):
    tm = u_ref.shape[0]
    nc = tm // CHUNK
    for g in range(GM_GROUPS):
        c0 = g * CHUNK
        rhs = jnp.concatenate([vn_ref[n * CHUNK:(n + 1) * CHUNK, c0:c0 + CHUNK] for n in range(nc)], axis=1)
        mixed = _dot(ws_ref[g], rhs)
        bias = bs_ref[g]
        for n in range(nc):
            m = mixed[:, n * CHUNK:(n + 1) * CHUNK] + bias
            uu = u_ref[n * CHUNK:(n + 1) * CHUNK, c0:c0 + CHUNK].astype(F32)
            g_ref[n * CHUNK:(n + 1) * CHUNK, c0:c0 + CHUNK] = (uu * m).astype(BF16)
    ya = _dot(g_ref[...], woa_ref[...])
    o_ref[...] = (sga_ref[...].astype(F32) * ya).astype(BF16)


def _spatial(u, vn, sga, wp):
    t = u.shape[0]
    tm = TM_WIDE
    row = pl.BlockSpec((tm, D_MODEL), lambda i: (i, 0))
    full = lambda a: pl.BlockSpec(a.shape, lambda i: (0,) * a.ndim)
    return pl.pallas_call(
        _spatial_kernel,
        grid=(t // tm,),
        in_specs=[row, row, full(wp["w_s"]), full(wp["b_s"]), full(wp["w_oa"]), row],
        out_specs=row,
        out_shape=jax.ShapeDtypeStruct((t, D_MODEL), BF16),
        scratch_shapes=[pltpu.VMEM((tm, GM_WIDTH), BF16)],
        compiler_params=_cparams(("arbitrary",)),
        name="spatial",
    )(u, vn, wp["w_s"], wp["b_s"], wp["w_oa"], sga)


def _attn_kernel(q_ref, k_ref, vt_ref, o_ref, acc_ref, sa_ref, sb_ref, pa_ref, pb_ref, *, nk, tk):
    q = q_ref[0, 0]
    tq = q.shape[1]
    acc_ref[...] = jnp.zeros_like(acc_ref)

    def scores(c, s_ref):
        s = _dot(k_ref[0, 0, pl.ds(pl.multiple_of(c * tk, tk), tk), :], q)
        s_ref[...] = s
        return jnp.max(s.reshape(tk // SUBLANES, SUBLANES, tq), axis=0)

    def softmax(s_ref, p_ref, m, cmax):
        m_new = jnp.maximum(m, jnp.max(cmax, axis=0, keepdims=True))
        p_ref[...] = jnp.exp2(s_ref[...] - m_new).astype(BF16)
        return m_new, jnp.exp2(m - m_new)

    def accumulate(alpha, c, p_ref):
        return alpha * acc_ref[...] + _dot(vt_ref[0, 0, c], p_ref[...])

    cmax_a = scores(0, sa_ref)
    cmax_b = scores(1, sb_ref)
    m, alpha = softmax(sa_ref, pa_ref, jnp.full((1, tq), -jnp.inf, F32), cmax_a)

    def body(i, carry):
        m, alpha, cmax_b = carry
        h = 2 * i + 1
        cmax_a = scores(h + 1, sa_ref)
        acc_ref[...] = accumulate(alpha, h - 1, pa_ref)
        m, alpha = softmax(sb_ref, pb_ref, m, cmax_b)
        cmax_b = scores(h + 2, sb_ref)
        acc_ref[...] = accumulate(alpha, h, pb_ref)
        m, alpha = softmax(sa_ref, pa_ref, m, cmax_a)
        return m, alpha, cmax_b

    m, alpha, cmax_b = lax.fori_loop(0, nk // 2 - 1, body, (m, alpha, cmax_b))
    acc_ref[...] = accumulate(alpha, nk - 2, pa_ref)
    m, alpha = softmax(sb_ref, pb_ref, m, cmax_b)
    acc = accumulate(alpha, nk - 1, pb_ref)
    o = acc[0:V_DIM, :] / acc[V_DIM:V_DIM + 1, :]
    o_ref[0] = o.T.astype(BF16)


def _attention(q, k, vt):
    bsz, nh, _, seq = q.shape
    nk, tk = vt.shape[2], vt.shape[4]
    assert nk % 2 == 0
    tq = min(TQ, seq)
    return pl.pallas_call(
        functools.partial(_attn_kernel, nk=nk, tk=tk),
        grid=(bsz, nh, seq // tq),
        in_specs=[pl.BlockSpec((1, 1, HEAD_W, tq), lambda b, h, i: (b, h, 0, i)),
                  pl.BlockSpec((1, 1, seq, HEAD_W), lambda b, h, i: (b, h, 0, 0)),
                  pl.BlockSpec((1, 1, nk, V_ROWS, tk), lambda b, h, i: (b, h, 0, 0, 0))],
        out_specs=pl.BlockSpec((1, tq, V_DIM), lambda b, h, i: (b, i, h)),
        out_shape=jax.ShapeDtypeStruct((bsz, seq, nh * V_DIM), BF16),
        scratch_shapes=[pltpu.VMEM((V_ROWS, tq), F32), pltpu.VMEM((tk, tq), F32), pltpu.VMEM((tk, tq), F32),
                        pltpu.VMEM((tk, tq), BF16), pltpu.VMEM((tk, tq), BF16)],
        compiler_params=_cparams(("arbitrary", "arbitrary", "arbitrary")),
        name="attn",
    )(q, k, vt)


def _post_kernel(x_ref, mod_ref, o_ref, ap_ref, sgb_ref, wob_ref, wout_ref, g1_ref, b1_ref, wr_ref,
                 x1_ref, h2_ref, aff_ref, *, alpha):
    gate1 = mod_ref[0, 2:3, :]
    sh2 = mod_ref[0, 3:4, :]
    sc2 = mod_ref[0, 4:5, :]
    for j in range(x_ref.shape[0] // TM):
        rows = slice(j * TM, (j + 1) * TM)
        yb = _dot(o_ref[rows, :], wob_ref[...])
        mixin = (ap_ref[rows, :].astype(F32) + sgb_ref[rows, :].astype(F32) * yb).astype(BF16)
        mix = _dot(mixin, wout_ref[...])
        x1 = _ln(alpha * x_ref[rows, :] + gate1 * mix) * g1_ref[...] + b1_ref[...]
        x1_ref[rows, :] = x1
        h2 = _ln(x1) * (1.0 + sc2) + sh2
        for kk in range(ROW_TILES):
            h2_ref[pl.ds(j * TM * ROW_TILES + kk, TM, stride=ROW_TILES), :] = h2[:, kk * LANES:(kk + 1) * LANES]
        h_hi = h2.astype(BF16)
        h_lo = (h2 - h_hi.astype(F32)).astype(BF16)
        prod = _dot(h_hi, wr_ref[...]) + _dot(h_lo, wr_ref[...])
        logits = prod[:, 0:LANES] + prod[:, LANES:2 * LANES]
        lane = lax.broadcasted_iota(I32, logits.shape, 1)
        logits = jnp.where(lane < N_EXPERTS, logits, -jnp.inf)
        ex = jnp.exp(logits - jnp.max(logits, axis=-1, keepdims=True))
        aff = ex / jnp.sum(ex, axis=-1, keepdims=True)
        aff_ref[j] = aff.T[0:N_EXPERTS, :]


def _post(x2, mod3, o, ap, sgb, wp, bsz, seq, alpha):
    t = x2.shape[0]
    tm = TM_WIDE
    spb = seq // tm
    row = pl.BlockSpec((tm, D_MODEL), lambda i: (i, 0))
    full = lambda a: pl.BlockSpec(a.shape, lambda i: (0,) * a.ndim)
    return pl.pallas_call(
        functools.partial(_post_kernel, alpha=alpha),
        grid=(t // tm,),
        in_specs=[row, pl.BlockSpec((1, 6, D_MODEL), lambda i: (i // spb, 0, 0)), row, row, row,
                  full(wp["w_ob"]), full(wp["w_out"]), full(wp["ln1_g"]), full(wp["ln1_b"]),
                  full(wp["w_router"])],
        out_specs=[row, pl.BlockSpec((tm * ROW_TILES, LANES), lambda i: (i, 0)),
                   pl.BlockSpec((tm // TM, N_EXPERTS, TM), lambda i: (i, 0, 0))],
        out_shape=[jax.ShapeDtypeStruct((t, D_MODEL), F32),
                   jax.ShapeDtypeStruct((t * ROW_TILES, LANES), F32),
                   jax.ShapeDtypeStruct((t // TM, N_EXPERTS, TM), F32)],
        compiler_params=_cparams(("arbitrary",)),
        name="post",
    )(x2, mod3, o, ap, sgb, wp["w_ob"], wp["w_out"], wp["ln1_g"], wp["ln1_b"], wp["w_router"])


def _route_kernel(aff_ref, spos_ref, flat_ref, cum_ref, tokp_ref, cnt_ref, op_ref, *, cap, nb):
    bits = pltpu.bitcast(aff_ref[...], I32)
    tm = bits.shape[2]

    def count(mask):
        c = jnp.sum(jnp.where(mask, 1, 0), axis=0, keepdims=True)
        return jnp.sum(c, axis=2, keepdims=True)

    def bisect(i, thr):
        cand = thr | lax.shift_left(jnp.int32(1), 30 - i)
        return jnp.where(count(bits >= cand) >= cap, cand, thr)

    thr = lax.fori_loop(0, 31, bisect, jnp.zeros((1, N_EXPERTS, 1), I32))
    gt = bits > thr
    eq = bits == thr
    need = cap - count(gt)

    n = nb * N_EXPERTS
    ebits = N_EXPERTS.bit_length() - 1
    emask = N_EXPERTS - 1
    r = lax.broadcasted_iota(I32, (tm, tm), 0)
    c = lax.broadcasted_iota(I32, (tm, tm), 1)
    upper = jnp.where(r <= c, 1.0, 0.0).astype(BF16)
    ones = jnp.ones((tm, tm), BF16)
    rows_per = min(256, n)

    def fill(pred):
        for k in range(n // rows_per):
            ri = k * rows_per + lax.broadcasted_iota(I32, (rows_per, n), 0)
            ci = lax.broadcasted_iota(I32, (rows_per, n), 1)
            op_ref[k * rows_per:(k + 1) * rows_per, :] = jnp.where(pred(ri, ci), 1.0, 0.0).astype(BF16)

    def prefix(x3):
        x = x3.reshape(n, tm)
        xb = x.astype(BF16)
        incl = _dot(xb, upper)
        tot = _dot(xb, ones)
        hi = jnp.floor(tot * (1.0 / BF16_INT))
        lo = tot - float(BF16_INT) * hi
        offs = float(BF16_INT) * _dot(op_ref[...], hi.astype(BF16)) + _dot(op_ref[...], lo.astype(BF16))
        shape = (nb, N_EXPERTS, tm)
        return (incl - x + offs).reshape(shape), offs.reshape(shape), tot.reshape(shape)

    fill(lambda ri, ci: ((ri & emask) == (ci & emask)) & (ci < ri))
    rank_eq, _, _ = prefix(jnp.where(eq, 1.0, 0.0))
    sel = gt | (eq & (rank_eq < need.astype(F32)))
    self32 = jnp.where(sel, 1.0, 0.0)
    pos, offs, tot = prefix(self32)
    cum_ref[0:nb] = offs[:, :, 0:LANES].astype(I32)
    cum_ref[nb] = (offs[nb - 1] + tot[nb - 1])[:, 0:LANES].astype(I32)
    spos_ref[...] = jnp.where(sel, pos.astype(I32), -1)

    cnt = jnp.sum(self32, axis=1, keepdims=True)
    cnt_ref[...] = jnp.broadcast_to(cnt, cnt_ref.shape).astype(I32)
    tokp, _, _ = prefix(jnp.broadcast_to(cnt, (nb, N_EXPERTS, tm)))
    tokp_ref[...] = tokp.astype(I32)

    fill(lambda ri, ci: ((ri >> ebits) == (ci >> ebits)) & ((ci & emask) < (ri & emask)))
    within = _dot(op_ref[...], self32.reshape(n, tm).astype(BF16)).reshape(nb, N_EXPERTS, tm)
    flat_ref[...] = (tokp + within).astype(I32)


def _route(aff3, cap):
    nb, ne, tm = aff3.shape
    assert ne == N_EXPERTS and ne & (ne - 1) == 0 and ne * tm <= BF16_INT * BF16_INT
    shp = jax.ShapeDtypeStruct((nb, ne, tm), I32)
    return pl.pallas_call(
        functools.partial(_route_kernel, cap=cap, nb=nb),
        out_shape=[shp, shp, jax.ShapeDtypeStruct((nb + 1, ne, LANES), I32), shp, shp],
        scratch_shapes=[pltpu.VMEM((nb * ne, nb * ne), BF16)],
        compiler_params=pltpu.CompilerParams(vmem_limit_bytes=VMEM_LIMIT),
        name="route",
    )(aff3)


def _slots_kernel(blo_ref, bhi_ref, spos_ref, aff_ref, flat_ref, idx_ref, gate_ref, dst_ref, *, tm, nt, tps):
    e = pl.program_id(0)
    for jj in range(tps):
        _slot_tile(e, pl.program_id(1) * tps + jj, jj, blo_ref, bhi_ref, spos_ref, aff_ref, flat_ref,
                   idx_ref, gate_ref, dst_ref, tm, nt)


def _slot_tile(e, j, jj, blo_ref, bhi_ref, spos_ref, aff_ref, flat_ref, idx_ref, gate_ref, dst_ref, tm, nt):
    slot = j * SLOT_TILE + lax.broadcasted_iota(I32, (SLOT_TILE, 1), 0)
    lane = lax.broadcasted_iota(I32, (1, tm), 1).astype(F32)
    zero = jnp.zeros((1, tm), F32)
    lo = blo_ref[e * nt + j]
    hi = bhi_ref[e * nt + j]

    def block(b, off):
        onehot = jnp.where(slot == spos_ref[b, pl.ds(e, 1), :] + off, 1.0, 0.0).astype(BF16)
        g = aff_ref[b, pl.ds(e, 1), :]
        g0 = g.astype(BF16).astype(F32)
        g1 = (g - g0).astype(BF16).astype(F32)
        g2 = g - g0 - g1
        fl = flat_ref[b, pl.ds(e, 1), :]
        rows = [lane, (jnp.zeros((1, tm), I32) + b).astype(F32), g0, g1, g2,
                lax.shift_right_logical(fl, BF16_INT_BITS).astype(F32), (fl & (BF16_INT - 1)).astype(F32), zero]
        vals = jnp.concatenate(rows + [zero] * 8, axis=0).astype(BF16)
        return _dot_nt(vals, onehot)

    def body(i, acc):
        b0 = lo + SLOT_BLOCKS_PER_TRIP * i
        for u in range(SLOT_BLOCKS_PER_TRIP):
            b = b0 + u
            acc = acc + block(jnp.minimum(b, hi - 1), jnp.where(b < hi, 0, NO_SLOT))
        return acc

    trips = (hi - lo + SLOT_BLOCKS_PER_TRIP - 1) // SLOT_BLOCKS_PER_TRIP
    a = lax.fori_loop(0, trips, body, jnp.zeros((16, SLOT_TILE), F32))
    idx_ref[jj] = (a[1:2] * tm + a[0:1]).astype(I32)
    gate_ref[jj] = a[2:3] + a[3:4] + a[4:5]
    dst_ref[jj] = (a[5:6] * float(BF16_INT) + a[6:7]).astype(I32)


def _slots(cum, spos, aff3, flat, cap):
    nb, ne, tm = spos.shape
    nt = cap // SLOT_TILE
    assert tm <= BF16_INT and nb <= BF16_INT and ne * cap <= BF16_INT * BF16_INT
    edges = jnp.arange(nt + 1, dtype=I32) * SLOT_TILE
    cs, ce = cum[:-1].T, cum[1:].T
    blo = jnp.sum(ce[:, None, :] <= edges[None, :-1, None], axis=2).astype(I32).reshape(-1)
    bhi = jnp.sum(cs[:, None, :] < edges[None, 1:, None], axis=2).astype(I32).reshape(-1)
    full = lambda a: pl.BlockSpec(a.shape, lambda e, j, lo, hi: (0,) * a.ndim)
    tps = math.gcd(nt, SLOT_TILES_PER_STEP)
    out = pl.BlockSpec((tps, 1, SLOT_TILE), lambda e, j, lo, hi: (e * (nt // tps) + j, 0, 0))
    return pl.pallas_call(
        functools.partial(_slots_kernel, tm=tm, nt=nt, tps=tps),
        grid_spec=pltpu.PrefetchScalarGridSpec(
            num_scalar_prefetch=2,
            grid=(ne, nt // tps),
            in_specs=[full(spos), full(aff3), full(flat)],
            out_specs=[out, out, out]),
        out_shape=[jax.ShapeDtypeStruct((ne * nt, 1, SLOT_TILE), I32),
                   jax.ShapeDtypeStruct((ne * nt, 1, SLOT_TILE), F32),
                   jax.ShapeDtypeStruct((ne * nt, 1, SLOT_TILE), I32)],
        compiler_params=_cparams(("arbitrary", "arbitrary")),
        name="slots",
    )(blo, bhi, spos, aff3, flat)


def _moe_kernel(idx_ref, idxn_ref, dstp_ref, dst_ref, gate_ref, w1_ref, w3_ref, w2_ref, h2_hbm, y_hbm,
                rin_ref, rout_ref, xe_ref, acc_ref, sem, *, ch, nf, mch):
    f = pl.program_id(2)
    blk = pl.program_id(0) * pl.num_programs(1) + pl.program_id(1)
    nblk = pl.num_programs(0) * pl.num_programs(1)
    par = blk & 1
    nm = ch // mch
    per_chunk = ch // (nf * nm)
    rows = ch * ROW_TILES

    def gather_row(ids_ref, s, buf):
        t = ids_ref[0, 0, s]
        return pltpu.make_async_copy(h2_hbm.at[pl.ds(pl.multiple_of(t * ROW_TILES, ROW_TILES), ROW_TILES), :],
                                     rin_ref.at[buf, pl.ds(pl.multiple_of(s * ROW_TILES, ROW_TILES), ROW_TILES), :],
                                     sem.at[buf])

    def gather_all(buf):
        return pltpu.make_async_copy(h2_hbm.at[pl.ds(0, rows), :], rin_ref.at[buf], sem.at[buf])

    def scatter_row(ids_ref, s):
        d = ids_ref[0, 0, s]
        return pltpu.make_async_copy(rout_ref.at[pl.ds(pl.multiple_of(s * ROW_TILES, ROW_TILES), ROW_TILES), :],
                                     y_hbm.at[pl.ds(pl.multiple_of(d * ROW_TILES, ROW_TILES), ROW_TILES), :],
                                     sem.at[2])

    def scatter_all():
        return pltpu.make_async_copy(rout_ref, y_hbm.at[pl.ds(0, rows), :], sem.at[2])

    @pl.when((blk == 0) & (f == 0))
    def _():
        rout_ref[...] = jnp.zeros_like(rout_ref)

        def start(s, _):
            gather_row(idx_ref, s, 0).start()
            return 0
        lax.fori_loop(0, ch, start, 0)

    @pl.when(f == 0)
    def _():
        gather_all(par).wait()
        for mc in range(nm):
            base = mc * mch * ROW_TILES
            parts = [rin_ref[par, pl.ds(base + kk, mch, stride=ROW_TILES), :] for kk in range(ROW_TILES)]
            xe_ref[mc * mch:(mc + 1) * mch, :] = jnp.concatenate(parts, axis=1).astype(BF16)

    for mc in range(nm):
        sl = slice(mc * mch, (mc + 1) * mch)
        xm = xe_ref[sl, :]
        a = _dot(xm, w1_ref[0])
        hid = (a * jax.nn.sigmoid(a) * _dot(xm, w3_ref[0])).astype(BF16)
        part = _dot(hid, w2_ref[0])
        first = (f * nm + mc) * per_chunk
        for i in range(per_chunk):
            gather_row(idxn_ref, first + i, 1 - par).start()
            scatter_row(dstp_ref, first + i).start()

        @pl.when(f == 0)
        def _():
            acc_ref[sl, :] = part

        @pl.when(f > 0)
        def _():
            acc_ref[sl, :] += part

    @pl.when(f == nf - 1)
    def _():
        scatter_all().wait()
        for mc in range(nm):
            sl = slice(mc * mch, (mc + 1) * mch)
            ye = acc_ref[sl, :] * gate_ref[sl, :]
            base = mc * mch * ROW_TILES
            for kk in range(ROW_TILES):
                rout_ref[pl.ds(base + kk, mch, stride=ROW_TILES), :] = ye[:, kk * LANES:(kk + 1) * LANES]

        @pl.when(blk == nblk - 1)
        def _():
            def start(s, _):
                scatter_row(dst_ref, s).start()
                return 0
            lax.fori_loop(0, ch, start, 0)
            scatter_all().wait()
            gather_all(1 - par).wait()


def _moe(idx, dst, gate, h2rows, w1, w3, w2, cap):
    ne = N_EXPERTS
    ch = min(MOE_ROWS, cap)
    nh = cap // ch
    nf = D_FF // FF_TILE
    mch = min(M_CHUNK, ch)
    assert ch % (nf * (ch // mch)) == 0
    smem = lambda: pl.BlockSpec((1, 1, ch), lambda e, h, f: (e * nh + h, 0, 0), memory_space=pltpu.SMEM)
    smem_next = pl.BlockSpec((1, 1, ch), lambda e, h, f: (jnp.minimum(e * nh + h + 1, ne * nh - 1), 0, 0),
                             memory_space=pltpu.SMEM)
    smem_prev = pl.BlockSpec((1, 1, ch), lambda e, h, f: (jnp.maximum(e * nh + h - 1, 0), 0, 0),
                             memory_space=pltpu.SMEM)
    return pl.pallas_call(
        functools.partial(_moe_kernel, ch=ch, nf=nf, mch=mch),
        grid=(ne, nh, nf),
        in_specs=[smem(), smem_next, smem_prev, smem(),
                  pl.BlockSpec((ch, 1), lambda e, h, f: (e * nh + h, 0)),
                  pl.BlockSpec((1, D_MODEL, FF_TILE), lambda e, h, f: (e, 0, f)),
                  pl.BlockSpec((1, D_MODEL, FF_TILE), lambda e, h, f: (e, 0, f)),
                  pl.BlockSpec((1, FF_TILE, D_MODEL), lambda e, h, f: (e, f, 0)),
                  pl.BlockSpec(memory_space=pl.ANY)],
        out_specs=pl.BlockSpec(memory_space=pl.ANY),
        out_shape=jax.ShapeDtypeStruct((ne * cap * ROW_TILES, LANES), F32),
        scratch_shapes=[pltpu.VMEM((2, ch * ROW_TILES, LANES), F32),
                        pltpu.VMEM((ch * ROW_TILES, LANES), F32),
                        pltpu.VMEM((ch, D_MODEL), BF16),
                        pltpu.VMEM((ch, D_MODEL), F32),
                        pltpu.SemaphoreType.DMA((3,))],
        compiler_params=_cparams(("arbitrary", "arbitrary", "arbitrary")),
        name="moe",
    )(idx.reshape(ne * nh, 1, ch), idx.reshape(ne * nh, 1, ch), dst.reshape(ne * nh, 1, ch),
      dst.reshape(ne * nh, 1, ch), gate.reshape(ne * cap, 1), w1, w3, w2, h2rows)


def _combine_kernel(pb_ref, pc_ref, pv_ref, y_ref, tokp_ref, cnt_ref, x1_ref, mod_ref, g2_ref, b2_ref,
                    o_ref, acc_ref, start_ref, count_ref, *, alpha):
    k = pl.program_id(0)
    blk = pb_ref[k]
    prev = pb_ref[jnp.maximum(k - 1, 0)]
    nxt = pb_ref[jnp.minimum(k + 1, pl.num_programs(0) - 1)]
    valid = pv_ref[k] == 1
    first = valid & ((k == 0) | (prev != blk))
    last = valid & ((k == pl.num_programs(0) - 1) | (nxt != blk) | (pv_ref[jnp.minimum(k + 1, pl.num_programs(0) - 1)] == 0))

    @pl.when(first)
    def _():
        acc_ref[...] = jnp.zeros_like(acc_ref)
        start_ref[...] = tokp_ref[0].astype(F32).T[:, 0:1].astype(I32)
        count_ref[...] = cnt_ref[0].astype(F32).T[:, 0:1].astype(I32)

    @pl.when(valid)
    def _():
        parts = [y_ref[pl.ds(kk, COMBINE_ROWS, stride=ROW_TILES), :] for kk in range(ROW_TILES)]
        rows = jnp.concatenate(parts, axis=1).astype(BF16)
        r = pc_ref[k] * COMBINE_ROWS + lax.broadcasted_iota(I32, (1, COMBINE_ROWS), 1)
        start = start_ref[...]
        own = jnp.where((r >= start) & (r < start + count_ref[...]), 1.0, 0.0).astype(BF16)
        acc_ref[...] += _dot(own, rows)

    @pl.when(last)
    def _():
        gate2 = mod_ref[0, 5:6, :]
        o_ref[...] = _ln(alpha * x1_ref[...] + gate2 * acc_ref[...]) * g2_ref[...] + b2_ref[...]


def _combine(pb, pc, pv, yrows, tokp_col, cnt_col, x1, mod3, wp, seq, alpha):
    t = x1.shape[0]
    tm = TM
    spb = seq // tm
    npairs = pb.shape[0]
    full = lambda a: pl.BlockSpec(a.shape, lambda k, pb, pc, pv: (0,) * a.ndim)
    row = pl.BlockSpec((tm, D_MODEL), lambda k, pb, pc, pv: (pb[k], 0))
    colspec = pl.BlockSpec((1, N_EXPERTS, tm), lambda k, pb, pc, pv: (pb[k], 0, 0))
    return pl.pallas_call(
        functools.partial(_combine_kernel, alpha=alpha),
        grid_spec=pltpu.PrefetchScalarGridSpec(
            num_scalar_prefetch=3,
            grid=(npairs,),
            in_specs=[pl.BlockSpec((COMBINE_ROWS * ROW_TILES, LANES), lambda k, pb, pc, pv: (pc[k], 0)),
                      colspec, colspec, row,
                      pl.BlockSpec((1, 6, D_MODEL), lambda k, pb, pc, pv: (pb[k] // spb, 0, 0)),
                      full(wp["ln2_g"]), full(wp["ln2_b"])],
            out_specs=row,
            scratch_shapes=[pltpu.VMEM((tm, D_MODEL), F32), pltpu.VMEM((tm, 1), I32), pltpu.VMEM((tm, 1), I32)]),
        out_shape=jax.ShapeDtypeStruct((t, D_MODEL), F32),
        compiler_params=_cparams(("arbitrary",)),
        name="combine",
    )(pb, pc, pv, yrows, tokp_col, cnt_col, x1, mod3, wp["ln2_g"], wp["ln2_b"])


def _pair_schedule(row_start, n_rows_total, nb):
    nchunks = n_rows_total // COMBINE_ROWS
    npairs = nb + nchunks
    rs = row_start
    re = jnp.concatenate([rs[1:], jnp.array([n_rows_total], I32)])
    c_lo = jnp.minimum(rs // COMBINE_ROWS, nchunks - 1)
    c_hi = jnp.maximum(c_lo, (re - 1) // COMBINE_ROWS)
    n_b = c_hi - c_lo + 1
    ends = jnp.cumsum(n_b)
    starts = ends - n_b
    k = jnp.arange(npairs, dtype=I32)
    valid = k < ends[-1]
    kk = jnp.minimum(k, ends[-1] - 1)
    b = jnp.sum(ends[None, :] <= kk[:, None], axis=1).astype(I32)
    c = c_lo[b] + (kk - starts[b])
    return b.astype(I32), c.astype(I32), valid.astype(I32)


def _pack_weights(l, seq, w_in, b_in, ln_v_g, ln_v_b, w_spatial, b_spatial, w_oa, q_norm_g, w_uq, kv_norm_g,
                  w_ukv, w_ob, w_out, ln1_g, ln1_b, w_router, ln2_g, ln2_b):
    half = QK_ROPE // 2
    z32 = lambda rows: jnp.zeros((rows, half), F32)

    def rope_cols(w):
        return jnp.concatenate([w[:, :half], z32(w.shape[0]), w[:, half:], z32(w.shape[0])], axis=1)

    wi, bi = w_in[l], b_in[l][None, :]
    off_cq, off_ckv, off_kr = 2 * GM_WIDTH, 2 * GM_WIDTH + Q_LORA, 2 * GM_WIDTH + Q_LORA + KV_LORA
    off_ga = off_kr + QK_ROPE

    def repack(a):
        return jnp.concatenate([a[:, :off_kr], rope_cols(a[:, off_kr:off_ga]), a[:, off_ga:]], axis=1)

    wq = w_uq[l].reshape(Q_LORA, N_HEADS, QK_NOPE + QK_ROPE)
    wq = jnp.concatenate([wq[:, :, :QK_NOPE],
                          wq[:, :, QK_NOPE:QK_NOPE + half], jnp.zeros((Q_LORA, N_HEADS, half), F32),
                          wq[:, :, QK_NOPE + half:], jnp.zeros((Q_LORA, N_HEADS, half), F32)], axis=2)
    wkv = w_ukv[l].reshape(KV_LORA, N_HEADS, QK_NOPE + V_DIM)
    inv = ROPE_BASE ** (-jnp.arange(half, dtype=F32) / half)
    ang = jnp.arange(seq, dtype=F32)[:, None] * inv[None, :]
    cos, sin, zs = jnp.cos(ang), jnp.sin(ang), jnp.zeros((seq, half), F32)
    wr_hi = w_router[l].astype(BF16)
    wr_lo = (w_router[l] - wr_hi.astype(F32)).astype(BF16)
    wr = (jnp.zeros((D_MODEL, 2 * LANES), BF16).at[:, :N_EXPERTS].set(wr_hi)
          .at[:, LANES:LANES + N_EXPERTS].set(wr_lo))
    return {
        "w_in": repack(wi).astype(BF16), "b_in": repack(bi),
        "ln_v_g": ln_v_g[l][None, :], "ln_v_b": ln_v_b[l][None, :],
        "w_s": w_spatial[l].astype(BF16),
        "b_s": jnp.broadcast_to(b_spatial[l][:, :, None], (GM_GROUPS, CHUNK, CHUNK)),
        "w_oa": w_oa[l].astype(BF16),
        "q_norm_g": q_norm_g[l][None, :], "w_uq": wq.reshape(Q_LORA, N_HEADS * HEAD_W).T.astype(BF16),
        "kv_norm_g": kv_norm_g[l][None, :],
        "w_uk": wkv[:, :, :QK_NOPE].reshape(KV_LORA, N_HEADS * QK_NOPE).astype(BF16),
        "w_uvt": wkv[:, :, QK_NOPE:].reshape(KV_LORA, N_HEADS * V_DIM).T.astype(BF16),
        "cos": jnp.concatenate([cos, zs, cos, zs], axis=1), "sin": jnp.concatenate([-sin, zs, sin, zs], axis=1),
        "w_ob": w_ob[l].astype(BF16), "w_out": w_out[l].astype(BF16),
        "ln1_g": ln1_g[l][None, :], "ln1_b": ln1_b[l][None, :], "w_router": wr,
        "ln2_g": ln2_g[l][None, :], "ln2_b": ln2_b[l][None, :],
    }


def _layer(x, c, l, alpha, w_ada, b_ada, w1, w3, w2, packed):
    bsz, seq, _ = x.shape
    t = bsz * seq
    cap = EC_CAPACITY_FACTOR * t // N_EXPERTS
    x2 = x.reshape(t, D_MODEL)
    mod3 = _mod(c, w_ada[l], b_ada[l])
    u, vn, q, k, vt, sga, sgb = _inproj(x2, mod3, packed, bsz, seq)
    ap = _spatial(u, vn, sga, packed)
    o = _attention(q, k, vt).reshape(t, D_MODEL)
    x1, h2, aff3 = _post(x2, mod3, o, ap, sgb, packed, bsz, seq, alpha)
    nb = t // TM
    spos, flat, cum, tokp, cnt = _route(aff3, cap)
    idx, gate, dst = _slots(cum[:, :, 0], spos, aff3, flat, cap)
    yrows = _moe(idx, dst, gate, h2, w1[l], w3[l], w2[l], cap)
    pb, pc, pv = _pair_schedule(tokp[:, 0, 0], N_EXPERTS * cap, nb)
    y = _combine(pb, pc, pv, yrows, tokp, cnt, x1, mod3, packed, seq, alpha)
    return y.reshape(bsz, seq, D_MODEL)


def kernel(x_prompt, x_sample, c_prompt, c_sample, w_ada, b_ada, w_in, b_in, ln_v_g, ln_v_b, w_spatial, b_spatial,
           w_oa, q_norm_g, w_uq, kv_norm_g, w_ukv, w_ob, w_out, ln1_g, ln1_b, w_router, w1, w3, w2, ln2_g, ln2_b):
    depth = w_ada.shape[0]
    alpha = (2.0 * depth) ** 0.25
    w1, w3, w2 = w1.astype(BF16), w3.astype(BF16), w2.astype(BF16)
    outs = []
    for x, c in ((x_prompt, c_prompt), (x_sample, c_sample)):
        for l in range(depth):
            packed = _pack_weights(l, x.shape[1], w_in, b_in, ln_v_g, ln_v_b, w_spatial, b_spatial, w_oa,
                                   q_norm_g, w_uq, kv_norm_g, w_ukv, w_ob, w_out, ln1_g, ln1_b, w_router,
                                   ln2_g, ln2_b)
            x = _layer(x, c, l, alpha, w_ada, b_ada, w1, w3, w2, packed)
        outs.append(x)
    return tuple(outs)
```

```python
import functools
import math

import jax
import jax.numpy as jnp
from jax import lax
from jax.experimental import pallas as pl
from jax.experimental.pallas import tpu as pltpu

F32 = jnp.float32
BF16 = jnp.bfloat16
I32 = jnp.int32

D_MODEL = 1024
GM_WIDTH = 1024
GM_GROUPS = 8
CHUNK = 128
N_HEADS = 8
QK_NOPE = 128
QK_ROPE = 64
V_DIM = 128
V_ROWS = 144
Q_LORA = 384
KV_LORA = 256
ROPE_BASE = 10000.0
N_EXPERTS = 16
EC_CAPACITY_FACTOR = 2
D_FF = 2048
LN_EPS = 1e-5
RMS_EPS = 1e-6

LANES = 128
SUBLANES = 8
HEAD_W = 256
ROW_TILES = D_MODEL // LANES

C_U, C_V, C_CQ, C_CKV, C_KR, C_GA, C_GB, C_END = 0, 1024, 2048, 2432, 2688, 2816, 3840, 4864

TM_IN = 512
ATTN_VMEM_RESERVE = 6 * 1024 * 1024
TM = 256
TM_WIDE = 512
SLOT_TILE = 256
COMBINE_ROWS = 1024
SLOT_TILES_PER_STEP = 4
SLOT_BLOCKS_PER_TRIP = 6
FF_TILE = 2048
M_CHUNK = 512
MOE_ROWS = 1024
V7X_VMEM_BYTES = 64 * 1024 * 1024
VMEM_LIMIT = V7X_VMEM_BYTES * 7 // 8
BF16_INT_BITS = 8
BF16_INT = 1 << BF16_INT_BITS
NO_SLOT = -(1 << 30)


def _cparams(sem):
    return pltpu.CompilerParams(dimension_semantics=sem, vmem_limit_bytes=VMEM_LIMIT)


def _ln(x):
    mu = jnp.mean(x, axis=-1, keepdims=True)
    xc = x - mu
    var = jnp.mean(xc * xc, axis=-1, keepdims=True)
    return xc * lax.rsqrt(var + LN_EPS)


def _rms(x):
    return x * lax.rsqrt(jnp.mean(x * x, axis=-1, keepdims=True) + RMS_EPS)


def _gelu(x):
    return 0.5 * x * (1.0 + lax.erf(x * (2.0 ** -0.5)))


def _dot(a, b):
    return jnp.dot(a, b, preferred_element_type=F32)


def _dot_nt(a, b):
    return lax.dot_general(a, b, (((1,), (1,)), ((), ())), preferred_element_type=F32)


def _mod_kernel(c_ref, w_ref, b_ref, o_ref):
    c = c_ref[...]
    s = c * jax.nn.sigmoid(c)
    o_ref[...] = jnp.dot(s, w_ref[...], preferred_element_type=F32,
                         precision=lax.Precision.HIGHEST) + b_ref[...]


def _mod(c, w_ada, b_ada):
    bsz = c.shape[0]
    bp = -(-bsz // SUBLANES) * SUBLANES
    cp = jnp.zeros((bp, D_MODEL), F32).at[:bsz].set(c)
    n = w_ada.shape[1]
    tn = 1024
    out = pl.pallas_call(
        _mod_kernel,
        grid=(n // tn,),
        in_specs=[pl.BlockSpec((bp, D_MODEL), lambda j: (0, 0)),
                  pl.BlockSpec((D_MODEL, tn), lambda j: (0, j)),
                  pl.BlockSpec((1, tn), lambda j: (0, j))],
        out_specs=pl.BlockSpec((bp, tn), lambda j: (0, j)),
        out_shape=jax.ShapeDtypeStruct((bp, n), F32),
        compiler_params=_cparams(("arbitrary",)),
        name="mod",
    )(cp, w_ada, b_ada.reshape(1, n))
    return out[:bsz].reshape(bsz, 6, D_MODEL)


def _inproj_kernel(x_ref, mod_ref, w_ref, b_ref, lvg_ref, lvb_ref, qg_ref, wuq_ref, kvg_ref,
                   wuk_ref, wuvt_ref, cos_ref, sin_ref, cost_ref, sint_ref,
                   u_ref, vn_ref, q_ref, k_ref, vt_ref, sga_ref, sgb_ref, *, qscale):
    x = x_ref[...]
    sh1 = mod_ref[0, 0:1, :]
    sc1 = mod_ref[0, 1:2, :]
    h = (_ln(x) * (1.0 + sc1) + sh1).astype(BF16)

    def proj(a, b):
        return _dot(h, w_ref[:, a:b]) + b_ref[:, a:b]

    cos = cos_ref[...]
    sin = sin_ref[...]

    def rope(t):
        return t * cos + pltpu.roll(t, 64, 1) * sin

    u_ref[...] = _gelu(proj(C_U, C_V)).astype(BF16)
    v = _gelu(proj(C_V, C_CQ))
    vn_ref[...] = (_ln(v) * lvg_ref[...] + lvb_ref[...]).astype(BF16)

    cqn = (_rms(proj(C_CQ, C_CKV)) * qg_ref[...]).astype(BF16)
    qt = _dot_nt(wuq_ref[...], cqn) * qscale
    cost = cost_ref[...]
    sint = sint_ref[...]
    for hh in range(N_HEADS):
        o = hh * HEAD_W
        q_ref[0, hh, 0:QK_NOPE, :] = qt[o:o + QK_NOPE, :].astype(BF16)
        r = qt[o + QK_NOPE:o + HEAD_W, :]
        swapped = jnp.concatenate([r[64:], r[:64]], axis=0)
        q_ref[0, hh, QK_NOPE:HEAD_W, :] = (r * cost + swapped * sint).astype(BF16)

    ckvn = (_rms(proj(C_CKV, C_KR)) * kvg_ref[...]).astype(BF16)
    kn = _dot(ckvn, wuk_ref[...])
    kr = rope(proj(C_KR, C_GA)).astype(BF16)
    for hh in range(N_HEADS):
        k_ref[0, hh, :, 0:QK_NOPE] = kn[:, hh * QK_NOPE:(hh + 1) * QK_NOPE].astype(BF16)
        k_ref[0, hh, :, QK_NOPE:HEAD_W] = kr
    vt = _dot_nt(wuvt_ref[...], ckvn)
    tm = vt.shape[1]
    tail = jnp.where(lax.broadcasted_iota(I32, (V_ROWS - V_DIM, tm), 0) == 0, 1.0, 0.0).astype(BF16)
    for hh in range(N_HEADS):
        vt_ref[0, hh, 0, 0:V_DIM, :] = vt[hh * V_DIM:(hh + 1) * V_DIM, :].astype(BF16)
        vt_ref[0, hh, 0, V_DIM:V_ROWS, :] = tail

    sga_ref[...] = jax.nn.sigmoid(proj(C_GA, C_GB)).astype(BF16)
    sgb_ref[...] = jax.nn.sigmoid(proj(C_GB, C_END)).astype(BF16)


def _inproj(x2, mod3, wp, bsz, seq):
    t = bsz * seq
    tm = min(TM_IN, seq)
    spb = seq // tm
    full = lambda a: pl.BlockSpec(a.shape, lambda i: (0,) * a.ndim)
    qscale = (QK_NOPE + QK_ROPE) ** -0.5 * math.log2(math.e)
    row = pl.BlockSpec((tm, D_MODEL), lambda i: (i, 0))
    qk_spec = pl.BlockSpec((1, N_HEADS, tm, HEAD_W), lambda i: (i // spb, 0, i % spb, 0))
    qt_spec = pl.BlockSpec((1, N_HEADS, HEAD_W, tm), lambda i: (i // spb, 0, 0, i % spb))
    ins = [x2, mod3, wp["w_in"], wp["b_in"], wp["ln_v_g"], wp["ln_v_b"], wp["q_norm_g"], wp["w_uq"],
           wp["kv_norm_g"], wp["w_uk"], wp["w_uvt"], wp["cos"], wp["sin"], wp["cos"].T, wp["sin"].T]
    in_specs = [row, pl.BlockSpec((1, 6, D_MODEL), lambda i: (i // spb, 0, 0))]
    in_specs += [full(a) for a in ins[2:11]]
    in_specs += [pl.BlockSpec((tm, LANES), lambda i: (i % spb, 0))] * 2
    in_specs += [pl.BlockSpec((LANES, tm), lambda i: (0, i % spb))] * 2
    return pl.pallas_call(
        functools.partial(_inproj_kernel, qscale=qscale),
        grid=(t // tm,),
        in_specs=in_specs,
        out_specs=[row, row, qt_spec, qk_spec,
                   pl.BlockSpec((1, N_HEADS, 1, V_ROWS, tm), lambda i: (i // spb, 0, i % spb, 0, 0)),
                   row, row],
        out_shape=[jax.ShapeDtypeStruct((t, D_MODEL), BF16),
                   jax.ShapeDtypeStruct((t, D_MODEL), BF16),
                   jax.ShapeDtypeStruct((bsz, N_HEADS, HEAD_W, seq), BF16),
                   jax.ShapeDtypeStruct((bsz, N_HEADS, seq, HEAD_W), BF16),
                   jax.ShapeDtypeStruct((bsz, N_HEADS, spb, V_ROWS, tm), BF16),
                   jax.ShapeDtypeStruct((t, D_MODEL), BF16),
                   jax.ShapeDtypeStruct((t, D_MODEL), BF16)],
        compiler_params=_cparams(("arbitrary",)),
        name="inproj",
    )(*ins)


def _spatial_kernel(u_ref, vn_ref, ws_ref, bs_ref, woa_ref, sga_ref, o_ref, g_ref):
    tm = u_ref.shape[0]
    nc = tm // CHUNK
    for g in range(GM_GROUPS):
        c0 = g * CHUNK
        rhs = jnp.concatenate([vn_ref[n * CHUNK:(n + 1) * CHUNK, c0:c0 + CHUNK] for n in range(nc)], axis=1)
        mixed = _dot(ws_ref[g], rhs)
        bias = bs_ref[g]
        for n in range(nc):
            m = mixed[:, n * CHUNK:(n + 1) * CHUNK] + bias
            uu = u_ref[n * CHUNK:(n + 1) * CHUNK, c0:c0 + CHUNK].astype(F32)
            g_ref[n * CHUNK:(n + 1) * CHUNK, c0:c0 + CHUNK] = (uu * m).astype(BF16)
    ya = _dot(g_ref[...], woa_ref[...])
    o_ref[...] = (sga_ref[...].astype(F32) * ya).astype(BF16)


def _spatial(u, vn, sga, wp):
    t = u.shape[0]
    tm = TM_WIDE
    row = pl.BlockSpec((tm, D_MODEL), lambda i: (i, 0))
    full = lambda a: pl.BlockSpec(a.shape, lambda i: (0,) * a.ndim)
    return pl.pallas_call(
        _spatial_kernel,
        grid=(t // tm,),
        in_specs=[row, row, full(wp["w_s"]), full(wp["b_s"]), full(wp["w_oa"]), row],
        out_specs=row,
        out_shape=jax.ShapeDtypeStruct((t, D_MODEL), BF16),
        scratch_shapes=[pltpu.VMEM((tm, GM_WIDTH), BF16)],
        compiler_params=_cparams(("arbitrary",)),
        name="spatial",
    )(u, vn, wp["w_s"], wp["b_s"], wp["w_oa"], sga)


def _attn_kernel(q_ref, k_ref, vt_ref, o_ref, acc_ref, sa_ref, sb_ref, pa_ref, pb_ref, *, nk, tk):
    q = q_ref[0, 0]
    tq = q.shape[1]
    acc_ref[...] = jnp.zeros_like(acc_ref)

    def scores(c, s_ref):
        s = _dot(k_ref[0, 0, pl.ds(pl.multiple_of(c * tk, tk), tk), :], q)
        s_ref[...] = s
        return jnp.max(s.reshape(tk // SUBLANES, SUBLANES, tq), axis=0)

    def softmax(s_ref, p_ref, m, cmax):
        m_new = jnp.maximum(m, jnp.max(cmax, axis=0, keepdims=True))
        p_ref[...] = jnp.exp2(s_ref[...] - m_new).astype(BF16)
        return m_new, jnp.exp2(m - m_new)

    def accumulate(alpha, c, p_ref):
        return alpha * acc_ref[...] + _dot(vt_ref[0, 0, c], p_ref[...])

    cmax_a = scores(0, sa_ref)
    cmax_b = scores(1, sb_ref)
    m, alpha = softmax(sa_ref, pa_ref, jnp.full((1, tq), -jnp.inf, F32), cmax_a)

    def body(i, carry):
        m, alpha, cmax_b = carry
        h = 2 * i + 1
        cmax_a = scores(h + 1, sa_ref)
        acc_ref[...] = accumulate(alpha, h - 1, pa_ref)
        m, alpha = softmax(sb_ref, pb_ref, m, cmax_b)
        cmax_b = scores(h + 2, sb_ref)
        acc_ref[...] = accumulate(alpha, h, pb_ref)
        m, alpha = softmax(sa_ref, pa_ref, m, cmax_a)
        return m, alpha, cmax_b

    m, alpha, cmax_b = lax.fori_loop(0, nk // 2 - 1, body, (m, alpha, cmax_b))
    acc_ref[...] = accumulate(alpha, nk - 2, pa_ref)
    m, alpha = softmax(sb_ref, pb_ref, m, cmax_b)
    acc = accumulate(alpha, nk - 1, pb_ref)
    o = acc[0:V_DIM, :] / acc[V_DIM:V_DIM + 1, :]
    o_ref[0] = o.T.astype(BF16)


def _attention(q, k, vt):
    bsz, nh, _, seq = q.shape
    nk, tk = vt.shape[2], vt.shape[4]
    assert nk % 2 == 0
    kv_bytes = 2 * (seq * HEAD_W + nk * V_ROWS * tk)
    per_query = tk * (4 + 4 + 2 + 2) + 2 * 2 * HEAD_W + 2 * 2 * V_DIM + 4 * V_ROWS

    def tile(kv_buffers):
        room = VMEM_LIMIT - ATTN_VMEM_RESERVE - kv_buffers * kv_bytes
        return min(seq, 1 << (room // per_query).bit_length() - 1)

    kv_buffers = 2 if tile(2) == tile(1) else 1
    tq = tile(kv_buffers)
    kv_mode = pl.Buffered(kv_buffers)
    return pl.pallas_call(
        functools.partial(_attn_kernel, nk=nk, tk=tk),
        grid=(bsz, nh, seq // tq),
        in_specs=[pl.BlockSpec((1, 1, HEAD_W, tq), lambda b, h, i: (b, h, 0, i)),
                  pl.BlockSpec((1, 1, seq, HEAD_W), lambda b, h, i: (b, h, 0, 0), pipeline_mode=kv_mode),
                  pl.BlockSpec((1, 1, nk, V_ROWS, tk), lambda b, h, i: (b, h, 0, 0, 0), pipeline_mode=kv_mode)],
        out_specs=pl.BlockSpec((1, tq, V_DIM), lambda b, h, i: (b, i, h)),
        out_shape=jax.ShapeDtypeStruct((bsz, seq, nh * V_DIM), BF16),
        scratch_shapes=[pltpu.VMEM((V_ROWS, tq), F32), pltpu.VMEM((tk, tq), F32), pltpu.VMEM((tk, tq), F32),
                        pltpu.VMEM((tk, tq), BF16), pltpu.VMEM((tk, tq), BF16)],
        compiler_params=_cparams(("arbitrary", "arbitrary", "arbitrary")),
        name="attn",
    )(q, k, vt)


def _post_kernel(x_ref, mod_ref, o_ref, ap_ref, sgb_ref, wob_ref, wout_ref, g1_ref, b1_ref, wr_ref,
                 x1_ref, h2_ref, aff_ref, *, alpha):
    gate1 = mod_ref[0, 2:3, :]
    sh2 = mod_ref[0, 3:4, :]
    sc2 = mod_ref[0, 4:5, :]
    for j in range(x_ref.shape[0] // TM):
        rows = slice(j * TM, (j + 1) * TM)
        yb = _dot(o_ref[rows, :], wob_ref[...])
        mixin = (ap_ref[rows, :].astype(F32) + sgb_ref[rows, :].astype(F32) * yb).astype(BF16)
        mix = _dot(mixin, wout_ref[...])
        x1 = _ln(alpha * x_ref[rows, :] + gate1 * mix) * g1_ref[...] + b1_ref[...]
        x1_ref[rows, :] = x1
        h2 = _ln(x1) * (1.0 + sc2) + sh2
        for kk in range(ROW_TILES):
            h2_ref[pl.ds(j * TM * ROW_TILES + kk, TM, stride=ROW_TILES), :] = h2[:, kk * LANES:(kk + 1) * LANES]
        h_hi = h2.astype(BF16)
        h_lo = (h2 - h_hi.astype(F32)).astype(BF16)
        prod = _dot(h_hi, wr_ref[...]) + _dot(h_lo, wr_ref[...])
        logits = prod[:, 0:LANES] + prod[:, LANES:2 * LANES]
        lane = lax.broadcasted_iota(I32, logits.shape, 1)
        logits = jnp.where(lane < N_EXPERTS, logits, -jnp.inf)
        ex = jnp.exp(logits - jnp.max(logits, axis=-1, keepdims=True))
        aff = ex / jnp.sum(ex, axis=-1, keepdims=True)
        aff_ref[j] = aff.T[0:N_EXPERTS, :]


def _post(x2, mod3, o, ap, sgb, wp, bsz, seq, alpha):
    t = x2.shape[0]
    tm = TM_WIDE
    spb = seq // tm
    row = pl.BlockSpec((tm, D_MODEL), lambda i: (i, 0))
    full = lambda a: pl.BlockSpec(a.shape, lambda i: (0,) * a.ndim)
    return pl.pallas_call(
        functools.partial(_post_kernel, alpha=alpha),
        grid=(t // tm,),
        in_specs=[row, pl.BlockSpec((1, 6, D_MODEL), lambda i: (i // spb, 0, 0)), row, row, row,
                  full(wp["w_ob"]), full(wp["w_out"]), full(wp["ln1_g"]), full(wp["ln1_b"]),
                  full(wp["w_router"])],
        out_specs=[row, pl.BlockSpec((tm * ROW_TILES, LANES), lambda i: (i, 0)),
                   pl.BlockSpec((tm // TM, N_EXPERTS, TM), lambda i: (i, 0, 0))],
        out_shape=[jax.ShapeDtypeStruct((t, D_MODEL), F32),
                   jax.ShapeDtypeStruct((t * ROW_TILES, LANES), F32),
                   jax.ShapeDtypeStruct((t // TM, N_EXPERTS, TM), F32)],
        compiler_params=_cparams(("arbitrary",)),
        name="post",
    )(x2, mod3, o, ap, sgb, wp["w_ob"], wp["w_out"], wp["ln1_g"], wp["ln1_b"], wp["w_router"])


def _route_kernel(aff_ref, spos_ref, flat_ref, cum_ref, tokp_ref, cnt_ref, op_ref, *, cap, nb):
    bits = pltpu.bitcast(aff_ref[...], I32)
    tm = bits.shape[2]

    def count(mask):
        c = jnp.sum(jnp.where(mask, 1, 0), axis=0, keepdims=True)
        return jnp.sum(c, axis=2, keepdims=True)

    def bisect(i, thr):
        cand = thr | lax.shift_left(jnp.int32(1), 30 - i)
        return jnp.where(count(bits >= cand) >= cap, cand, thr)

    thr = lax.fori_loop(0, 31, bisect, jnp.zeros((1, N_EXPERTS, 1), I32))
    gt = bits > thr
    eq = bits == thr
    need = cap - count(gt)

    n = nb * N_EXPERTS
    ebits = N_EXPERTS.bit_length() - 1
    emask = N_EXPERTS - 1
    r = lax.broadcasted_iota(I32, (tm, tm), 0)
    c = lax.broadcasted_iota(I32, (tm, tm), 1)
    upper = jnp.where(r <= c, 1.0, 0.0).astype(BF16)
    ones = jnp.ones((tm, tm), BF16)
    rows_per = min(256, n)

    def fill(pred):
        for k in range(n // rows_per):
            ri = k * rows_per + lax.broadcasted_iota(I32, (rows_per, n), 0)
            ci = lax.broadcasted_iota(I32, (rows_per, n), 1)
            op_ref[k * rows_per:(k + 1) * rows_per, :] = jnp.where(pred(ri, ci), 1.0, 0.0).astype(BF16)

    def prefix(x3):
        x = x3.reshape(n, tm)
        xb = x.astype(BF16)
        incl = _dot(xb, upper)
        tot = _dot(xb, ones)
        hi = jnp.floor(tot * (1.0 / BF16_INT))
        lo = tot - float(BF16_INT) * hi
        offs = float(BF16_INT) * _dot(op_ref[...], hi.astype(BF16)) + _dot(op_ref[...], lo.astype(BF16))
        shape = (nb, N_EXPERTS, tm)
        return (incl - x + offs).reshape(shape), offs.reshape(shape), tot.reshape(shape)

    fill(lambda ri, ci: ((ri & emask) == (ci & emask)) & (ci < ri))
    rank_eq, _, _ = prefix(jnp.where(eq, 1.0, 0.0))
    sel = gt | (eq & (rank_eq < need.astype(F32)))
    self32 = jnp.where(sel, 1.0, 0.0)
    pos, offs, tot = prefix(self32)
    cum_ref[0:nb] = offs[:, :, 0:LANES].astype(I32)
    cum_ref[nb] = (offs[nb - 1] + tot[nb - 1])[:, 0:LANES].astype(I32)
    spos_ref[...] = jnp.where(sel, pos.astype(I32), -1)

    cnt = jnp.sum(self32, axis=1, keepdims=True)
    cnt_ref[...] = jnp.broadcast_to(cnt, cnt_ref.shape).astype(I32)
    tokp, _, _ = prefix(jnp.broadcast_to(cnt, (nb, N_EXPERTS, tm)))
    tokp_ref[...] = tokp.astype(I32)

    fill(lambda ri, ci: ((ri >> ebits) == (ci >> ebits)) & ((ci & emask) < (ri & emask)))
    within = _dot(op_ref[...], self32.reshape(n, tm).astype(BF16)).reshape(nb, N_EXPERTS, tm)
    flat_ref[...] = (tokp + within).astype(I32)


def _route(aff3, cap):
    nb, ne, tm = aff3.shape
    assert ne == N_EXPERTS and ne & (ne - 1) == 0 and ne * tm <= BF16_INT * BF16_INT
    shp = jax.ShapeDtypeStruct((nb, ne, tm), I32)
    return pl.pallas_call(
        functools.partial(_route_kernel, cap=cap, nb=nb),
        out_shape=[shp, shp, jax.ShapeDtypeStruct((nb + 1, ne, LANES), I32), shp, shp],
        scratch_shapes=[pltpu.VMEM((nb * ne, nb * ne), BF16)],
        compiler_params=pltpu.CompilerParams(vmem_limit_bytes=VMEM_LIMIT),
        name="route",
    )(aff3)


def _slots_kernel(blo_ref, bhi_ref, spos_ref, aff_ref, flat_ref, idx_ref, gate_ref, dst_ref, *, tm, nt, tps):
    e = pl.program_id(0)
    for jj in range(tps):
        _slot_tile(e, pl.program_id(1) * tps + jj, jj, blo_ref, bhi_ref, spos_ref, aff_ref, flat_ref,
                   idx_ref, gate_ref, dst_ref, tm, nt)


def _slot_tile(e, j, jj, blo_ref, bhi_ref, spos_ref, aff_ref, flat_ref, idx_ref, gate_ref, dst_ref, tm, nt):
    slot = j * SLOT_TILE + lax.broadcasted_iota(I32, (SLOT_TILE, 1), 0)
    lane = lax.broadcasted_iota(I32, (1, tm), 1).astype(F32)
    zero = jnp.zeros((1, tm), F32)
    lo = blo_ref[e * nt + j]
    hi = bhi_ref[e * nt + j]

    def block(b, off):
        onehot = jnp.where(slot == spos_ref[b, pl.ds(e, 1), :] + off, 1.0, 0.0).astype(BF16)
        g = aff_ref[b, pl.ds(e, 1), :]
        g0 = g.astype(BF16).astype(F32)
        g1 = (g - g0).astype(BF16).astype(F32)
        g2 = g - g0 - g1
        fl = flat_ref[b, pl.ds(e, 1), :]
        rows = [lane, (jnp.zeros((1, tm), I32) + b).astype(F32), g0, g1, g2,
                lax.shift_right_logical(fl, BF16_INT_BITS).astype(F32), (fl & (BF16_INT - 1)).astype(F32), zero]
        vals = jnp.concatenate(rows + [zero] * 8, axis=0).astype(BF16)
        return _dot_nt(vals, onehot)

    def body(i, acc):
        b0 = lo + SLOT_BLOCKS_PER_TRIP * i
        for u in range(SLOT_BLOCKS_PER_TRIP):
            b = b0 + u
            acc = acc + block(jnp.minimum(b, hi - 1), jnp.where(b < hi, 0, NO_SLOT))
        return acc

    trips = (hi - lo + SLOT_BLOCKS_PER_TRIP - 1) // SLOT_BLOCKS_PER_TRIP
    a = lax.fori_loop(0, trips, body, jnp.zeros((16, SLOT_TILE), F32))
    idx_ref[jj] = (a[1:2] * tm + a[0:1]).astype(I32)
    gate_ref[jj] = a[2:3] + a[3:4] + a[4:5]
    dst_ref[jj] = (a[5:6] * float(BF16_INT) + a[6:7]).astype(I32)


def _slots(cum, spos, aff3, flat, cap):
    nb, ne, tm = spos.shape
    nt = cap // SLOT_TILE
    assert tm <= BF16_INT and nb <= BF16_INT and ne * cap <= BF16_INT * BF16_INT
    edges = jnp.arange(nt + 1, dtype=I32) * SLOT_TILE
    cs, ce = cum[:-1].T, cum[1:].T
    blo = jnp.sum(ce[:, None, :] <= edges[None, :-1, None], axis=2).astype(I32).reshape(-1)
    bhi = jnp.sum(cs[:, None, :] < edges[None, 1:, None], axis=2).astype(I32).reshape(-1)
    full = lambda a: pl.BlockSpec(a.shape, lambda e, j, lo, hi: (0,) * a.ndim)
    tps = math.gcd(nt, SLOT_TILES_PER_STEP)
    out = pl.BlockSpec((tps, 1, SLOT_TILE), lambda e, j, lo, hi: (e * (nt // tps) + j, 0, 0))
    return pl.pallas_call(
        functools.partial(_slots_kernel, tm=tm, nt=nt, tps=tps),
        grid_spec=pltpu.PrefetchScalarGridSpec(
            num_scalar_prefetch=2,
            grid=(ne, nt // tps),
            in_specs=[full(spos), full(aff3), full(flat)],
            out_specs=[out, out, out]),
        out_shape=[jax.ShapeDtypeStruct((ne * nt, 1, SLOT_TILE), I32),
                   jax.ShapeDtypeStruct((ne * nt, 1, SLOT_TILE), F32),
                   jax.ShapeDtypeStruct((ne * nt, 1, SLOT_TILE), I32)],
        compiler_params=_cparams(("arbitrary", "arbitrary")),
        name="slots",
    )(blo, bhi, spos, aff3, flat)


def _moe_kernel(idx_ref, idxn_ref, dstp_ref, dst_ref, gate_ref, w1_ref, w3_ref, w2_ref, h2_hbm, y_hbm,
                rin_ref, rout_ref, xe_ref, acc_ref, sem, *, ch, nf, mch):
    f = pl.program_id(2)
    blk = pl.program_id(0) * pl.num_programs(1) + pl.program_id(1)
    nblk = pl.num_programs(0) * pl.num_programs(1)
    par = blk & 1
    nm = ch // mch
    per_chunk = ch // (nf * nm)
    rows = ch * ROW_TILES

    def gather_row(ids_ref, s, buf):
        t = ids_ref[0, 0, s]
        return pltpu.make_async_copy(h2_hbm.at[pl.ds(pl.multiple_of(t * ROW_TILES, ROW_TILES), ROW_TILES), :],
                                     rin_ref.at[buf, pl.ds(pl.multiple_of(s * ROW_TILES, ROW_TILES), ROW_TILES), :],
                                     sem.at[buf])

    def gather_all(buf):
        return pltpu.make_async_copy(h2_hbm.at[pl.ds(0, rows), :], rin_ref.at[buf], sem.at[buf])

    def scatter_row(ids_ref, s):
        d = ids_ref[0, 0, s]
        return pltpu.make_async_copy(rout_ref.at[pl.ds(pl.multiple_of(s * ROW_TILES, ROW_TILES), ROW_TILES), :],
                                     y_hbm.at[pl.ds(pl.multiple_of(d * ROW_TILES, ROW_TILES), ROW_TILES), :],
                                     sem.at[2])

    def scatter_all():
        return pltpu.make_async_copy(rout_ref, y_hbm.at[pl.ds(0, rows), :], sem.at[2])

    @pl.when((blk == 0) & (f == 0))
    def _():
        rout_ref[...] = jnp.zeros_like(rout_ref)

        def start(s, _):
            gather_row(idx_ref, s, 0).start()
            return 0
        lax.fori_loop(0, ch, start, 0)

    @pl.when(f == 0)
    def _():
        gather_all(par).wait()
        for mc in range(nm):
            base = mc * mch * ROW_TILES
            parts = [rin_ref[par, pl.ds(base + kk, mch, stride=ROW_TILES), :] for kk in range(ROW_TILES)]
            xe_ref[mc * mch:(mc + 1) * mch, :] = jnp.concatenate(parts, axis=1).astype(BF16)

    for mc in range(nm):
        sl = slice(mc * mch, (mc + 1) * mch)
        xm = xe_ref[sl, :]
        a = _dot(xm, w1_ref[0])
        hid = (a * jax.nn.sigmoid(a) * _dot(xm, w3_ref[0])).astype(BF16)
        part = _dot(hid, w2_ref[0])
        first = (f * nm + mc) * per_chunk
        for i in range(per_chunk):
            gather_row(idxn_ref, first + i, 1 - par).start()
            scatter_row(dstp_ref, first + i).start()

        @pl.when(f == 0)
        def _():
            acc_ref[sl, :] = part

        @pl.when(f > 0)
        def _():
            acc_ref[sl, :] += part

    @pl.when(f == nf - 1)
    def _():
        scatter_all().wait()
        for mc in range(nm):
            sl = slice(mc * mch, (mc + 1) * mch)
            ye = acc_ref[sl, :] * gate_ref[sl, :]
            base = mc * mch * ROW_TILES
            for kk in range(ROW_TILES):
                rout_ref[pl.ds(base + kk, mch, stride=ROW_TILES), :] = ye[:, kk * LANES:(kk + 1) * LANES]

        @pl.when(blk == nblk - 1)
        def _():
            def start(s, _):
                scatter_row(dst_ref, s).start()
                return 0
            lax.fori_loop(0, ch, start, 0)
            scatter_all().wait()
            gather_all(1 - par).wait()


def _moe(idx, dst, gate, h2rows, w1, w3, w2, cap):
    ne = N_EXPERTS
    ch = min(MOE_ROWS, cap)
    nh = cap // ch
    nf = D_FF // FF_TILE
    mch = min(M_CHUNK, ch)
    assert ch % (nf * (ch // mch)) == 0
    smem = lambda: pl.BlockSpec((1, 1, ch), lambda e, h, f: (e * nh + h, 0, 0), memory_space=pltpu.SMEM)
    smem_next = pl.BlockSpec((1, 1, ch), lambda e, h, f: (jnp.minimum(e * nh + h + 1, ne * nh - 1), 0, 0),
                             memory_space=pltpu.SMEM)
    smem_prev = pl.BlockSpec((1, 1, ch), lambda e, h, f: (jnp.maximum(e * nh + h - 1, 0), 0, 0),
                             memory_space=pltpu.SMEM)
    return pl.pallas_call(
        functools.partial(_moe_kernel, ch=ch, nf=nf, mch=mch),
        grid=(ne, nh, nf),
        in_specs=[smem(), smem_next, smem_prev, smem(),
                  pl.BlockSpec((ch, 1), lambda e, h, f: (e * nh + h, 0)),
                  pl.BlockSpec((1, D_MODEL, FF_TILE), lambda e, h, f: (e, 0, f)),
                  pl.BlockSpec((1, D_MODEL, FF_TILE), lambda e, h, f: (e, 0, f)),
                  pl.BlockSpec((1, FF_TILE, D_MODEL), lambda e, h, f: (e, f, 0)),
                  pl.BlockSpec(memory_space=pl.ANY)],
        out_specs=pl.BlockSpec(memory_space=pl.ANY),
        out_shape=jax.ShapeDtypeStruct((ne * cap * ROW_TILES, LANES), F32),
        scratch_shapes=[pltpu.VMEM((2, ch * ROW_TILES, LANES), F32),
                        pltpu.VMEM((ch * ROW_TILES, LANES), F32),
                        pltpu.VMEM((ch, D_MODEL), BF16),
                        pltpu.VMEM((ch, D_MODEL), F32),
                        pltpu.SemaphoreType.DMA((3,))],
        compiler_params=_cparams(("arbitrary", "arbitrary", "arbitrary")),
        name="moe",
    )(idx.reshape(ne * nh, 1, ch), idx.reshape(ne * nh, 1, ch), dst.reshape(ne * nh, 1, ch),
      dst.reshape(ne * nh, 1, ch), gate.reshape(ne * cap, 1), w1, w3, w2, h2rows)


def _combine_kernel(pb_ref, pc_ref, pv_ref, y_ref, tokp_ref, cnt_ref, x1_ref, mod_ref, g2_ref, b2_ref,
                    o_ref, acc_ref, start_ref, count_ref, *, alpha):
    k = pl.program_id(0)
    blk = pb_ref[k]
    prev = pb_ref[jnp.maximum(k - 1, 0)]
    nxt = pb_ref[jnp.minimum(k + 1, pl.num_programs(0) - 1)]
    valid = pv_ref[k] == 1
    first = valid & ((k == 0) | (prev != blk))
    last = valid & ((k == pl.num_programs(0) - 1) | (nxt != blk) | (pv_ref[jnp.minimum(k + 1, pl.num_programs(0) - 1)] == 0))

    @pl.when(first)
    def _():
        acc_ref[...] = jnp.zeros_like(acc_ref)
        start_ref[...] = tokp_ref[0].astype(F32).T[:, 0:1].astype(I32)
        count_ref[...] = cnt_ref[0].astype(F32).T[:, 0:1].astype(I32)

    @pl.when(valid)
    def _():
        parts = [y_ref[pl.ds(kk, COMBINE_ROWS, stride=ROW_TILES), :] for kk in range(ROW_TILES)]
        rows = jnp.concatenate(parts, axis=1).astype(BF16)
        r = pc_ref[k] * COMBINE_ROWS + lax.broadcasted_iota(I32, (1, COMBINE_ROWS), 1)
        start = start_ref[...]
        own = jnp.where((r >= start) & (r < start + count_ref[...]), 1.0, 0.0).astype(BF16)
        acc_ref[...] += _dot(own, rows)

    @pl.when(last)
    def _():
        gate2 = mod_ref[0, 5:6, :]
        o_ref[...] = _ln(alpha * x1_ref[...] + gate2 * acc_ref[...]) * g2_ref[...] + b2_ref[...]


def _combine(pb, pc, pv, yrows, tokp_col, cnt_col, x1, mod3, wp, seq, alpha):
    t = x1.shape[0]
    tm = TM
    spb = seq // tm
    npairs = pb.shape[0]
    full = lambda a: pl.BlockSpec(a.shape, lambda k, pb, pc, pv: (0,) * a.ndim)
    row = pl.BlockSpec((tm, D_MODEL), lambda k, pb, pc, pv: (pb[k], 0))
    colspec = pl.BlockSpec((1, N_EXPERTS, tm), lambda k, pb, pc, pv: (pb[k], 0, 0))
    return pl.pallas_call(
        functools.partial(_combine_kernel, alpha=alpha),
        grid_spec=pltpu.PrefetchScalarGridSpec(
            num_scalar_prefetch=3,
            grid=(npairs,),
            in_specs=[pl.BlockSpec((COMBINE_ROWS * ROW_TILES, LANES), lambda k, pb, pc, pv: (pc[k], 0)),
                      colspec, colspec, row,
                      pl.BlockSpec((1, 6, D_MODEL), lambda k, pb, pc, pv: (pb[k] // spb, 0, 0)),
                      full(wp["ln2_g"]), full(wp["ln2_b"])],
            out_specs=row,
            scratch_shapes=[pltpu.VMEM((tm, D_MODEL), F32), pltpu.VMEM((tm, 1), I32), pltpu.VMEM((tm, 1), I32)]),
        out_shape=jax.ShapeDtypeStruct((t, D_MODEL), F32),
        compiler_params=_cparams(("arbitrary",)),
        name="combine",
    )(pb, pc, pv, yrows, tokp_col, cnt_col, x1, mod3, wp["ln2_g"], wp["ln2_b"])


def _pair_schedule(row_start, n_rows_total, nb):
    nchunks = n_rows_total // COMBINE_ROWS
    npairs = nb + nchunks
    rs = row_start
    re = jnp.concatenate([rs[1:], jnp.array([n_rows_total], I32)])
    c_lo = jnp.minimum(rs // COMBINE_ROWS, nchunks - 1)
    c_hi = jnp.maximum(c_lo, (re - 1) // COMBINE_ROWS)
    n_b = c_hi - c_lo + 1
    ends = jnp.cumsum(n_b)
    starts = ends - n_b
    k = jnp.arange(npairs, dtype=I32)
    valid = k < ends[-1]
    kk = jnp.minimum(k, ends[-1] - 1)
    b = jnp.sum(ends[None, :] <= kk[:, None], axis=1).astype(I32)
    c = c_lo[b] + (kk - starts[b])
    return b.astype(I32), c.astype(I32), valid.astype(I32)


def _pack_weights(l, seq, w_in, b_in, ln_v_g, ln_v_b, w_spatial, b_spatial, w_oa, q_norm_g, w_uq, kv_norm_g,
                  w_ukv, w_ob, w_out, ln1_g, ln1_b, w_router, ln2_g, ln2_b):
    half = QK_ROPE // 2
    z32 = lambda rows: jnp.zeros((rows, half), F32)

    def rope_cols(w):
        return jnp.concatenate([w[:, :half], z32(w.shape[0]), w[:, half:], z32(w.shape[0])], axis=1)

    wi, bi = w_in[l], b_in[l][None, :]
    off_cq, off_ckv, off_kr = 2 * GM_WIDTH, 2 * GM_WIDTH + Q_LORA, 2 * GM_WIDTH + Q_LORA + KV_LORA
    off_ga = off_kr + QK_ROPE

    def repack(a):
        return jnp.concatenate([a[:, :off_kr], rope_cols(a[:, off_kr:off_ga]), a[:, off_ga:]], axis=1)

    wq = w_uq[l].reshape(Q_LORA, N_HEADS, QK_NOPE + QK_ROPE)
    wq = jnp.concatenate([wq[:, :, :QK_NOPE],
                          wq[:, :, QK_NOPE:QK_NOPE + half], jnp.zeros((Q_LORA, N_HEADS, half), F32),
                          wq[:, :, QK_NOPE + half:], jnp.zeros((Q_LORA, N_HEADS, half), F32)], axis=2)
    wkv = w_ukv[l].reshape(KV_LORA, N_HEADS, QK_NOPE + V_DIM)
    inv = ROPE_BASE ** (-jnp.arange(half, dtype=F32) / half)
    ang = jnp.arange(seq, dtype=F32)[:, None] * inv[None, :]
    cos, sin, zs = jnp.cos(ang), jnp.sin(ang), jnp.zeros((seq, half), F32)
    wr_hi = w_router[l].astype(BF16)
    wr_lo = (w_router[l] - wr_hi.astype(F32)).astype(BF16)
    wr = (jnp.zeros((D_MODEL, 2 * LANES), BF16).at[:, :N_EXPERTS].set(wr_hi)
          .at[:, LANES:LANES + N_EXPERTS].set(wr_lo))
    return {
        "w_in": repack(wi).astype(BF16), "b_in": repack(bi),
        "ln_v_g": ln_v_g[l][None, :], "ln_v_b": ln_v_b[l][None, :],
        "w_s": w_spatial[l].astype(BF16),
        "b_s": jnp.broadcast_to(b_spatial[l][:, :, None], (GM_GROUPS, CHUNK, CHUNK)),
        "w_oa": w_oa[l].astype(BF16),
        "q_norm_g": q_norm_g[l][None, :], "w_uq": wq.reshape(Q_LORA, N_HEADS * HEAD_W).T.astype(BF16),
        "kv_norm_g": kv_norm_g[l][None, :],
        "w_uk": wkv[:, :, :QK_NOPE].reshape(KV_LORA, N_HEADS * QK_NOPE).astype(BF16),
        "w_uvt": wkv[:, :, QK_NOPE:].reshape(KV_LORA, N_HEADS * V_DIM).T.astype(BF16),
        "cos": jnp.concatenate([cos, zs, cos, zs], axis=1), "sin": jnp.concatenate([-sin, zs, sin, zs], axis=1),
        "w_ob": w_ob[l].astype(BF16), "w_out": w_out[l].astype(BF16),
        "ln1_g": ln1_g[l][None, :], "ln1_b": ln1_b[l][None, :], "w_router": wr,
        "ln2_g": ln2_g[l][None, :], "ln2_b": ln2_b[l][None, :],
    }


def _layer(x, c, l, alpha, w_ada, b_ada, w1, w3, w2, packed):
    bsz, seq, _ = x.shape
    t = bsz * seq
    cap = EC_CAPACITY_FACTOR * t // N_EXPERTS
    x2 = x.reshape(t, D_MODEL)
    mod3 = _mod(c, w_ada[l], b_ada[l])
    u, vn, q, k, vt, sga, sgb = _inproj(x2, mod3, packed, bsz, seq)
    ap = _spatial(u, vn, sga, packed)
    o = _attention(q, k, vt).reshape(t, D_MODEL)
    x1, h2, aff3 = _post(x2, mod3, o, ap, sgb, packed, bsz, seq, alpha)
    nb = t // TM
    spos, flat, cum, tokp, cnt = _route(aff3, cap)
    idx, gate, dst = _slots(cum[:, :, 0], spos, aff3, flat, cap)
    yrows = _moe(idx, dst, gate, h2, w1[l], w3[l], w2[l], cap)
    pb, pc, pv = _pair_schedule(tokp[:, 0, 0], N_EXPERTS * cap, nb)
    y = _combine(pb, pc, pv, yrows, tokp, cnt, x1, mod3, packed, seq, alpha)
    return y.reshape(bsz, seq, D_MODEL)


def kernel(x_prompt, x_sample, c_prompt, c_sample, w_ada, b_ada, w_in, b_in, ln_v_g, ln_v_b, w_spatial, b_spatial,
           w_oa, q_norm_g, w_uq, kv_norm_g, w_ukv, w_ob, w_out, ln1_g, ln1_b, w_router, w1, w3, w2, ln2_g, ln2_b):
    depth = w_ada.shape[0]
    alpha = (2.0 * depth) ** 0.25
    w1, w3, w2 = w1.astype(BF16), w3.astype(BF16), w2.astype(BF16)
    outs = []
    for x, c in ((x_prompt, c_prompt), (x_sample, c_sample)):
        for l in range(depth):
            packed = _pack_weights(l, x.shape[1], w_in, b_in, ln_v_g, ln_v_b, w_spatial, b_spatial, w_oa,
                                   q_norm_g, w_uq, kv_norm_g, w_ukv, w_ob, w_out, ln1_g, ln1_b, w_router,
                                   ln2_g, ln2_b)
            x = _layer(x, c, l, alpha, w_ada, b_ada, w1, w3, w2, packed)
        outs.append(x)
    return tuple(outs)
```

```python
import functools
import math

import jax
import jax.numpy as jnp
from jax import lax
from jax.experimental import pallas as pl
from jax.experimental.pallas import tpu as pltpu

F32 = jnp.float32
BF16 = jnp.bfloat16
I32 = jnp.int32

D_MODEL = 1024
GM_WIDTH = 1024
GM_GROUPS = 8
CHUNK = 128
N_HEADS = 8
QK_NOPE = 128
QK_ROPE = 64
V_DIM = 128
V_ROWS = 144
Q_LORA = 384
KV_LORA = 256
ROPE_BASE = 10000.0
N_EXPERTS = 16
EC_CAPACITY_FACTOR = 2
D_FF = 2048
LN_EPS = 1e-5
RMS_EPS = 1e-6

LANES = 128
SUBLANES = 8
HEAD_W = 256
ROW_TILES = D_MODEL // LANES

C_U, C_V, C_CQ, C_CKV, C_KR, C_GA, C_GB, C_END = 0, 1024, 2048, 2432, 2688, 2816, 3840, 4864

TM_IN = 512
ATTN_VMEM_RESERVE = 6 * 1024 * 1024
TM = 256
TM_WIDE = 1024
SLOT_TILE = 256
COMBINE_ROWS = 1024
SLOT_TILES_PER_STEP = 4
SLOT_BLOCKS_PER_TRIP = 6
FF_TILE = 2048
M_CHUNK = 512
MOE_ROWS = 1024
V7X_VMEM_BYTES = 64 * 1024 * 1024
VMEM_LIMIT = V7X_VMEM_BYTES * 7 // 8
BF16_INT_BITS = 8
BF16_INT = 1 << BF16_INT_BITS
NO_SLOT = -(1 << 30)


def _cparams(sem):
    return pltpu.CompilerParams(dimension_semantics=sem, vmem_limit_bytes=VMEM_LIMIT)


def _ln(x):
    mu = jnp.mean(x, axis=-1, keepdims=True)
    xc = x - mu
    var = jnp.mean(xc * xc, axis=-1, keepdims=True)
    return xc * lax.rsqrt(var + LN_EPS)


def _rms(x):
    return x * lax.rsqrt(jnp.mean(x * x, axis=-1, keepdims=True) + RMS_EPS)


def _gelu(x):
    return 0.5 * x * (1.0 + lax.erf(x * (2.0 ** -0.5)))


def _dot(a, b):
    return jnp.dot(a, b, preferred_element_type=F32)


def _dot_nt(a, b):
    return lax.dot_general(a, b, (((1,), (1,)), ((), ())), preferred_element_type=F32)


def _mod_kernel(c_ref, w_ref, b_ref, o_ref):
    c = c_ref[...]
    s = c * jax.nn.sigmoid(c)
    o_ref[...] = jnp.dot(s, w_ref[...], preferred_element_type=F32,
                         precision=lax.Precision.HIGHEST) + b_ref[...]


def _mod(c, w_ada, b_ada):
    bsz = c.shape[0]
    bp = -(-bsz // SUBLANES) * SUBLANES
    cp = jnp.zeros((bp, D_MODEL), F32).at[:bsz].set(c)
    n = w_ada.shape[1]
    tn = 1024
    out = pl.pallas_call(
        _mod_kernel,
        grid=(n // tn,),
        in_specs=[pl.BlockSpec((bp, D_MODEL), lambda j: (0, 0)),
                  pl.BlockSpec((D_MODEL, tn), lambda j: (0, j)),
                  pl.BlockSpec((1, tn), lambda j: (0, j))],
        out_specs=pl.BlockSpec((bp, tn), lambda j: (0, j)),
        out_shape=jax.ShapeDtypeStruct((bp, n), F32),
        compiler_params=_cparams(("arbitrary",)),
        name="mod",
    )(cp, w_ada, b_ada.reshape(1, n))
    return out[:bsz].reshape(bsz, 6, D_MODEL)


def _inproj_kernel(x_ref, mod_ref, w_ref, b_ref, lvg_ref, lvb_ref, qg_ref, wuq_ref, kvg_ref,
                   wuk_ref, wuvt_ref, cos_ref, sin_ref, cost_ref, sint_ref,
                   u_ref, vn_ref, q_ref, k_ref, vt_ref, sga_ref, sgb_ref, *, qscale):
    x = x_ref[...]
    sh1 = mod_ref[0, 0:1, :]
    sc1 = mod_ref[0, 1:2, :]
    h = (_ln(x) * (1.0 + sc1) + sh1).astype(BF16)

    def proj(a, b):
        return _dot(h, w_ref[:, a:b]) + b_ref[:, a:b]

    cos = cos_ref[...]
    sin = sin_ref[...]

    def rope(t):
        return t * cos + pltpu.roll(t, 64, 1) * sin

    u_ref[...] = _gelu(proj(C_U, C_V)).astype(BF16)
    v = _gelu(proj(C_V, C_CQ))
    vn_ref[...] = (_ln(v) * lvg_ref[...] + lvb_ref[...]).astype(BF16)

    cqn = (_rms(proj(C_CQ, C_CKV)) * qg_ref[...]).astype(BF16)
    qt = _dot_nt(wuq_ref[...], cqn) * qscale
    cost = cost_ref[...]
    sint = sint_ref[...]
    for hh in range(N_HEADS):
        o = hh * HEAD_W
        q_ref[0, hh, 0:QK_NOPE, :] = qt[o:o + QK_NOPE, :].astype(BF16)
        r = qt[o + QK_NOPE:o + HEAD_W, :]
        swapped = jnp.concatenate([r[64:], r[:64]], axis=0)
        q_ref[0, hh, QK_NOPE:HEAD_W, :] = (r * cost + swapped * sint).astype(BF16)

    ckvn = (_rms(proj(C_CKV, C_KR)) * kvg_ref[...]).astype(BF16)
    kn = _dot(ckvn, wuk_ref[...])
    kr = rope(proj(C_KR, C_GA)).astype(BF16)
    for hh in range(N_HEADS):
        k_ref[0, hh, :, 0:QK_NOPE] = kn[:, hh * QK_NOPE:(hh + 1) * QK_NOPE].astype(BF16)
        k_ref[0, hh, :, QK_NOPE:HEAD_W] = kr
    vt = _dot_nt(wuvt_ref[...], ckvn)
    tm = vt.shape[1]
    tail = jnp.where(lax.broadcasted_iota(I32, (V_ROWS - V_DIM, tm), 0) == 0, 1.0, 0.0).astype(BF16)
    for hh in range(N_HEADS):
        vt_ref[0, hh, 0, 0:V_DIM, :] = vt[hh * V_DIM:(hh + 1) * V_DIM, :].astype(BF16)
        vt_ref[0, hh, 0, V_DIM:V_ROWS, :] = tail

    sga_ref[...] = jax.nn.sigmoid(proj(C_GA, C_GB)).astype(BF16)
    sgb_ref[...] = jax.nn.sigmoid(proj(C_GB, C_END)).astype(BF16)


def _inproj(x2, mod3, wp, bsz, seq):
    t = bsz * seq
    tm = min(TM_IN, seq)
    spb = seq // tm
    full = lambda a: pl.BlockSpec(a.shape, lambda i: (0,) * a.ndim)
    qscale = (QK_NOPE + QK_ROPE) ** -0.5 * math.log2(math.e)
    row = pl.BlockSpec((tm, D_MODEL), lambda i: (i, 0))
    qk_spec = pl.BlockSpec((1, N_HEADS, tm, HEAD_W), lambda i: (i // spb, 0, i % spb, 0))
    qt_spec = pl.BlockSpec((1, N_HEADS, HEAD_W, tm), lambda i: (i // spb, 0, 0, i % spb))
    ins = [x2, mod3, wp["w_in"], wp["b_in"], wp["ln_v_g"], wp["ln_v_b"], wp["q_norm_g"], wp["w_uq"],
           wp["kv_norm_g"], wp["w_uk"], wp["w_uvt"], wp["cos"], wp["sin"], wp["cos"].T, wp["sin"].T]
    in_specs = [row, pl.BlockSpec((1, 6, D_MODEL), lambda i: (i // spb, 0, 0))]
    in_specs += [full(a) for a in ins[2:11]]
    in_specs += [pl.BlockSpec((tm, LANES), lambda i: (i % spb, 0))] * 2
    in_specs += [pl.BlockSpec((LANES, tm), lambda i: (0, i % spb))] * 2
    return pl.pallas_call(
        functools.partial(_inproj_kernel, qscale=qscale),
        grid=(t // tm,),
        in_specs=in_specs,
        out_specs=[row, row, qt_spec, qk_spec,
                   pl.BlockSpec((1, N_HEADS, 1, V_ROWS, tm), lambda i: (i // spb, 0, i % spb, 0, 0)),
                   row, row],
        out_shape=[jax.ShapeDtypeStruct((t, D_MODEL), BF16),
                   jax.ShapeDtypeStruct((t, D_MODEL), BF16),
                   jax.ShapeDtypeStruct((bsz, N_HEADS, HEAD_W, seq), BF16),
                   jax.ShapeDtypeStruct((bsz, N_HEADS, seq, HEAD_W), BF16),
                   jax.ShapeDtypeStruct((bsz, N_HEADS, spb, V_ROWS, tm), BF16),
                   jax.ShapeDtypeStruct((t, D_MODEL), BF16),
                   jax.ShapeDtypeStruct((t, D_MODEL), BF16)],
        compiler_params=_cparams(("arbitrary",)),
        name="inproj",
    )(*ins)


def _spatial_kernel(u_ref, vn_ref, ws_ref, bs_ref, woa_ref, sga_ref, o_ref, g_ref):
    tm = u_ref.shape[0]
    nc = tm // CHUNK
    for g in range(GM_GROUPS):
        c0 = g * CHUNK
        rhs = jnp.concatenate([vn_ref[n * CHUNK:(n + 1) * CHUNK, c0:c0 + CHUNK] for n in range(nc)], axis=1)
        mixed = _dot(ws_ref[g], rhs)
        bias = bs_ref[g]
        for n in range(nc):
            m = mixed[:, n * CHUNK:(n + 1) * CHUNK] + bias
            uu = u_ref[n * CHUNK:(n + 1) * CHUNK, c0:c0 + CHUNK].astype(F32)
            g_ref[n * CHUNK:(n + 1) * CHUNK, c0:c0 + CHUNK] = (uu * m).astype(BF16)
    ya = _dot(g_ref[...], woa_ref[...])
    o_ref[...] = (sga_ref[...].astype(F32) * ya).astype(BF16)


def _spatial(u, vn, sga, wp):
    t = u.shape[0]
    tm = TM_WIDE
    row = pl.BlockSpec((tm, D_MODEL), lambda i: (i, 0))
    full = lambda a: pl.BlockSpec(a.shape, lambda i: (0,) * a.ndim)
    return pl.pallas_call(
        _spatial_kernel,
        grid=(t // tm,),
        in_specs=[row, row, full(wp["w_s"]), full(wp["b_s"]), full(wp["w_oa"]), row],
        out_specs=row,
        out_shape=jax.ShapeDtypeStruct((t, D_MODEL), BF16),
        scratch_shapes=[pltpu.VMEM((tm, GM_WIDTH), BF16)],
        compiler_params=_cparams(("arbitrary",)),
        name="spatial",
    )(u, vn, wp["w_s"], wp["b_s"], wp["w_oa"], sga)


def _attn_kernel(q_ref, k_ref, vt_ref, o_ref, acc_ref, sa_ref, sb_ref, pa_ref, pb_ref, *, nk, tk):
    q = q_ref[0, 0]
    tq = q.shape[1]
    acc_ref[...] = jnp.zeros_like(acc_ref)

    def scores(c, s_ref):
        s = _dot(k_ref[0, 0, pl.ds(pl.multiple_of(c * tk, tk), tk), :], q)
        s_ref[...] = s
        return jnp.max(s.reshape(tk // SUBLANES, SUBLANES, tq), axis=0)

    def softmax(s_ref, p_ref, m, cmax):
        m_new = jnp.maximum(m, jnp.max(cmax, axis=0, keepdims=True))
        p_ref[...] = jnp.exp2(s_ref[...] - m_new).astype(BF16)
        return m_new, jnp.exp2(m - m_new)

    def accumulate(alpha, c, p_ref):
        return alpha * acc_ref[...] + _dot(vt_ref[0, 0, c], p_ref[...])

    cmax_a = scores(0, sa_ref)
    cmax_b = scores(1, sb_ref)
    m, alpha = softmax(sa_ref, pa_ref, jnp.full((1, tq), -jnp.inf, F32), cmax_a)

    def body(i, carry):
        m, alpha, cmax_b = carry
        h = 2 * i + 1
        cmax_a = scores(h + 1, sa_ref)
        acc_ref[...] = accumulate(alpha, h - 1, pa_ref)
        m, alpha = softmax(sb_ref, pb_ref, m, cmax_b)
        cmax_b = scores(h + 2, sb_ref)
        acc_ref[...] = accumulate(alpha, h, pb_ref)
        m, alpha = softmax(sa_ref, pa_ref, m, cmax_a)
        return m, alpha, cmax_b

    m, alpha, cmax_b = lax.fori_loop(0, nk // 2 - 1, body, (m, alpha, cmax_b))
    acc_ref[...] = accumulate(alpha, nk - 2, pa_ref)
    m, alpha = softmax(sb_ref, pb_ref, m, cmax_b)
    acc = accumulate(alpha, nk - 1, pb_ref)
    o = acc[0:V_DIM, :] / acc[V_DIM:V_DIM + 1, :]
    o_ref[0] = o.T.astype(BF16)


def _attention(q, k, vt):
    bsz, nh, _, seq = q.shape
    nk, tk = vt.shape[2], vt.shape[4]
    assert nk % 2 == 0
    kv_bytes = 2 * (seq * HEAD_W + nk * V_ROWS * tk)
    per_query = tk * (4 + 4 + 2 + 2) + 2 * 2 * HEAD_W + 2 * 2 * V_DIM + 4 * V_ROWS

    def tile(kv_buffers):
        room = VMEM_LIMIT - ATTN_VMEM_RESERVE - kv_buffers * kv_bytes
        return min(seq, 1 << (room // per_query).bit_length() - 1)

    kv_buffers = 2 if tile(2) == tile(1) else 1
    tq = tile(kv_buffers)
    kv_mode = pl.Buffered(kv_buffers)
    return pl.pallas_call(
        functools.partial(_attn_kernel, nk=nk, tk=tk),
        grid=(bsz, nh, seq // tq),
        in_specs=[pl.BlockSpec((1, 1, HEAD_W, tq), lambda b, h, i: (b, h, 0, i)),
                  pl.BlockSpec((1, 1, seq, HEAD_W), lambda b, h, i: (b, h, 0, 0), pipeline_mode=kv_mode),
                  pl.BlockSpec((1, 1, nk, V_ROWS, tk), lambda b, h, i: (b, h, 0, 0, 0), pipeline_mode=kv_mode)],
        out_specs=pl.BlockSpec((1, tq, V_DIM), lambda b, h, i: (b, i, h)),
        out_shape=jax.ShapeDtypeStruct((bsz, seq, nh * V_DIM), BF16),
        scratch_shapes=[pltpu.VMEM((V_ROWS, tq), F32), pltpu.VMEM((tk, tq), F32), pltpu.VMEM((tk, tq), F32),
                        pltpu.VMEM((tk, tq), BF16), pltpu.VMEM((tk, tq), BF16)],
        compiler_params=_cparams(("arbitrary", "arbitrary", "arbitrary")),
        name="attn",
    )(q, k, vt)


def _post_kernel(x_ref, mod_ref, o_ref, ap_ref, sgb_ref, wob_ref, wout_ref, g1_ref, b1_ref, wr_ref,
                 x1_ref, h2_ref, aff_ref, *, alpha):
    gate1 = mod_ref[0, 2:3, :]
    sh2 = mod_ref[0, 3:4, :]
    sc2 = mod_ref[0, 4:5, :]
    for j in range(x_ref.shape[0] // TM):
        rows = slice(j * TM, (j + 1) * TM)
        yb = _dot(o_ref[rows, :], wob_ref[...])
        mixin = (ap_ref[rows, :].astype(F32) + sgb_ref[rows, :].astype(F32) * yb).astype(BF16)
        mix = _dot(mixin, wout_ref[...])
        x1 = _ln(alpha * x_ref[rows, :] + gate1 * mix) * g1_ref[...] + b1_ref[...]
        x1_ref[rows, :] = x1
        h2 = _ln(x1) * (1.0 + sc2) + sh2
        for kk in range(ROW_TILES):
            h2_ref[pl.ds(j * TM * ROW_TILES + kk, TM, stride=ROW_TILES), :] = h2[:, kk * LANES:(kk + 1) * LANES]
        h_hi = h2.astype(BF16)
        h_lo = (h2 - h_hi.astype(F32)).astype(BF16)
        prod = _dot(h_hi, wr_ref[...]) + _dot(h_lo, wr_ref[...])
        logits = prod[:, 0:LANES] + prod[:, LANES:2 * LANES]
        lane = lax.broadcasted_iota(I32, logits.shape, 1)
        logits = jnp.where(lane < N_EXPERTS, logits, -jnp.inf)
        ex = jnp.exp(logits - jnp.max(logits, axis=-1, keepdims=True))
        aff = ex / jnp.sum(ex, axis=-1, keepdims=True)
        aff_ref[j] = aff.T[0:N_EXPERTS, :]


def _post(x2, mod3, o, ap, sgb, wp, bsz, seq, alpha):
    t = x2.shape[0]
    tm = TM_WIDE
    spb = seq // tm
    row = pl.BlockSpec((tm, D_MODEL), lambda i: (i, 0))
    full = lambda a: pl.BlockSpec(a.shape, lambda i: (0,) * a.ndim)
    return pl.pallas_call(
        functools.partial(_post_kernel, alpha=alpha),
        grid=(t // tm,),
        in_specs=[row, pl.BlockSpec((1, 6, D_MODEL), lambda i: (i // spb, 0, 0)), row, row, row,
                  full(wp["w_ob"]), full(wp["w_out"]), full(wp["ln1_g"]), full(wp["ln1_b"]),
                  full(wp["w_router"])],
        out_specs=[row, pl.BlockSpec((tm * ROW_TILES, LANES), lambda i: (i, 0)),
                   pl.BlockSpec((tm // TM, N_EXPERTS, TM), lambda i: (i, 0, 0))],
        out_shape=[jax.ShapeDtypeStruct((t, D_MODEL), F32),
                   jax.ShapeDtypeStruct((t * ROW_TILES, LANES), F32),
                   jax.ShapeDtypeStruct((t // TM, N_EXPERTS, TM), F32)],
        compiler_params=_cparams(("arbitrary",)),
        name="post",
    )(x2, mod3, o, ap, sgb, wp["w_ob"], wp["w_out"], wp["ln1_g"], wp["ln1_b"], wp["w_router"])


def _route_kernel(aff_ref, spos_ref, flat_ref, cum_ref, tokp_ref, cnt_ref, op_ref, *, cap, nb):
    bits = pltpu.bitcast(aff_ref[...], I32)
    tm = bits.shape[2]

    def count(mask):
        c = jnp.sum(jnp.where(mask, 1, 0), axis=0, keepdims=True)
        return jnp.sum(c, axis=2, keepdims=True)

    def bisect(i, thr):
        cand = thr | lax.shift_left(jnp.int32(1), 30 - i)
        return jnp.where(count(bits >= cand) >= cap, cand, thr)

    thr = lax.fori_loop(0, 31, bisect, jnp.zeros((1, N_EXPERTS, 1), I32))
    gt = bits > thr
    eq = bits == thr
    need = cap - count(gt)

    n = nb * N_EXPERTS
    ebits = N_EXPERTS.bit_length() - 1
    emask = N_EXPERTS - 1
    r = lax.broadcasted_iota(I32, (tm, tm), 0)
    c = lax.broadcasted_iota(I32, (tm, tm), 1)
    upper = jnp.where(r <= c, 1.0, 0.0).astype(BF16)
    ones = jnp.ones((tm, tm), BF16)
    rows_per = min(256, n)

    def fill(pred):
        for k in range(n // rows_per):
            ri = k * rows_per + lax.broadcasted_iota(I32, (rows_per, n), 0)
            ci = lax.broadcasted_iota(I32, (rows_per, n), 1)
            op_ref[k * rows_per:(k + 1) * rows_per, :] = jnp.where(pred(ri, ci), 1.0, 0.0).astype(BF16)

    def prefix(x3):
        x = x3.reshape(n, tm)
        xb = x.astype(BF16)
        incl = _dot(xb, upper)
        tot = _dot(xb, ones)
        hi = jnp.floor(tot * (1.0 / BF16_INT))
        lo = tot - float(BF16_INT) * hi
        offs = float(BF16_INT) * _dot(op_ref[...], hi.astype(BF16)) + _dot(op_ref[...], lo.astype(BF16))
        shape = (nb, N_EXPERTS, tm)
        return (incl - x + offs).reshape(shape), offs.reshape(shape), tot.reshape(shape)

    fill(lambda ri, ci: ((ri & emask) == (ci & emask)) & (ci < ri))
    rank_eq, _, _ = prefix(jnp.where(eq, 1.0, 0.0))
    sel = gt | (eq & (rank_eq < need.astype(F32)))
    self32 = jnp.where(sel, 1.0, 0.0)
    pos, offs, tot = prefix(self32)
    cum_ref[0:nb] = offs[:, :, 0:LANES].astype(I32)
    cum_ref[nb] = (offs[nb - 1] + tot[nb - 1])[:, 0:LANES].astype(I32)
    spos_ref[...] = jnp.where(sel, pos.astype(I32), -1)

    cnt = jnp.sum(self32, axis=1, keepdims=True)
    cnt_ref[...] = jnp.broadcast_to(cnt, cnt_ref.shape).astype(I32)
    tokp, _, _ = prefix(jnp.broadcast_to(cnt, (nb, N_EXPERTS, tm)))
    tokp_ref[...] = tokp.astype(I32)

    fill(lambda ri, ci: ((ri >> ebits) == (ci >> ebits)) & ((ci & emask) < (ri & emask)))
    within = _dot(op_ref[...], self32.reshape(n, tm).astype(BF16)).reshape(nb, N_EXPERTS, tm)
    flat_ref[...] = (tokp + within).astype(I32)


def _route(aff3, cap):
    nb, ne, tm = aff3.shape
    assert ne == N_EXPERTS and ne & (ne - 1) == 0 and ne * tm <= BF16_INT * BF16_INT
    shp = jax.ShapeDtypeStruct((nb, ne, tm), I32)
    return pl.pallas_call(
        functools.partial(_route_kernel, cap=cap, nb=nb),
        out_shape=[shp, shp, jax.ShapeDtypeStruct((nb + 1, ne, LANES), I32), shp, shp],
        scratch_shapes=[pltpu.VMEM((nb * ne, nb * ne), BF16)],
        compiler_params=pltpu.CompilerParams(vmem_limit_bytes=VMEM_LIMIT),
        name="route",
    )(aff3)


def _slots_kernel(blo_ref, bhi_ref, spos_ref, aff_ref, flat_ref, idx_ref, gate_ref, dst_ref, *, tm, nt, tps):
    e = pl.program_id(0)
    for jj in range(tps):
        _slot_tile(e, pl.program_id(1) * tps + jj, jj, blo_ref, bhi_ref, spos_ref, aff_ref, flat_ref,
                   idx_ref, gate_ref, dst_ref, tm, nt)


def _slot_tile(e, j, jj, blo_ref, bhi_ref, spos_ref, aff_ref, flat_ref, idx_ref, gate_ref, dst_ref, tm, nt):
    slot = j * SLOT_TILE + lax.broadcasted_iota(I32, (SLOT_TILE, 1), 0)
    lane = lax.broadcasted_iota(I32, (1, tm), 1).astype(F32)
    zero = jnp.zeros((1, tm), F32)
    lo = blo_ref[e * nt + j]
    hi = bhi_ref[e * nt + j]

    def block(b, off):
        onehot = jnp.where(slot == spos_ref[b, pl.ds(e, 1), :] + off, 1.0, 0.0).astype(BF16)
        g = aff_ref[b, pl.ds(e, 1), :]
        g0 = g.astype(BF16).astype(F32)
        g1 = (g - g0).astype(BF16).astype(F32)
        g2 = g - g0 - g1
        fl = flat_ref[b, pl.ds(e, 1), :]
        rows = [lane, (jnp.zeros((1, tm), I32) + b).astype(F32), g0, g1, g2,
                lax.shift_right_logical(fl, BF16_INT_BITS).astype(F32), (fl & (BF16_INT - 1)).astype(F32), zero]
        vals = jnp.concatenate(rows + [zero] * 8, axis=0).astype(BF16)
        return _dot_nt(vals, onehot)

    def body(i, acc):
        b0 = lo + SLOT_BLOCKS_PER_TRIP * i
        for u in range(SLOT_BLOCKS_PER_TRIP):
            b = b0 + u
            acc = acc + block(jnp.minimum(b, hi - 1), jnp.where(b < hi, 0, NO_SLOT))
        return acc

    trips = (hi - lo + SLOT_BLOCKS_PER_TRIP - 1) // SLOT_BLOCKS_PER_TRIP
    a = lax.fori_loop(0, trips, body, jnp.zeros((16, SLOT_TILE), F32))
    idx_ref[jj] = (a[1:2] * tm + a[0:1]).astype(I32)
    gate_ref[jj] = a[2:3] + a[3:4] + a[4:5]
    dst_ref[jj] = (a[5:6] * float(BF16_INT) + a[6:7]).astype(I32)


def _slots(cum, spos, aff3, flat, cap):
    nb, ne, tm = spos.shape
    nt = cap // SLOT_TILE
    assert tm <= BF16_INT and nb <= BF16_INT and ne * cap <= BF16_INT * BF16_INT
    edges = jnp.arange(nt + 1, dtype=I32) * SLOT_TILE
    cs, ce = cum[:-1].T, cum[1:].T
    blo = jnp.sum(ce[:, None, :] <= edges[None, :-1, None], axis=2).astype(I32).reshape(-1)
    bhi = jnp.sum(cs[:, None, :] < edges[None, 1:, None], axis=2).astype(I32).reshape(-1)
    full = lambda a: pl.BlockSpec(a.shape, lambda e, j, lo, hi: (0,) * a.ndim)
    tps = math.gcd(nt, SLOT_TILES_PER_STEP)
    out = pl.BlockSpec((tps, 1, SLOT_TILE), lambda e, j, lo, hi: (e * (nt // tps) + j, 0, 0))
    return pl.pallas_call(
        functools.partial(_slots_kernel, tm=tm, nt=nt, tps=tps),
        grid_spec=pltpu.PrefetchScalarGridSpec(
            num_scalar_prefetch=2,
            grid=(ne, nt // tps),
            in_specs=[full(spos), full(aff3), full(flat)],
            out_specs=[out, out, out]),
        out_shape=[jax.ShapeDtypeStruct((ne * nt, 1, SLOT_TILE), I32),
                   jax.ShapeDtypeStruct((ne * nt, 1, SLOT_TILE), F32),
                   jax.ShapeDtypeStruct((ne * nt, 1, SLOT_TILE), I32)],
        compiler_params=_cparams(("arbitrary", "arbitrary")),
        name="slots",
    )(blo, bhi, spos, aff3, flat)


def _moe_kernel(idx_ref, idxn_ref, dstp_ref, dst_ref, gate_ref, w1_ref, w3_ref, w2_ref, h2_hbm, y_hbm,
                rin_ref, rout_ref, xe_ref, acc_ref, sem, *, ch, nf, mch):
    f = pl.program_id(2)
    blk = pl.program_id(0) * pl.num_programs(1) + pl.program_id(1)
    nblk = pl.num_programs(0) * pl.num_programs(1)
    par = blk & 1
    nm = ch // mch
    per_chunk = ch // (nf * nm)
    rows = ch * ROW_TILES

    def gather_row(ids_ref, s, buf):
        t = ids_ref[0, 0, s]
        return pltpu.make_async_copy(h2_hbm.at[pl.ds(pl.multiple_of(t * ROW_TILES, ROW_TILES), ROW_TILES), :],
                                     rin_ref.at[buf, pl.ds(pl.multiple_of(s * ROW_TILES, ROW_TILES), ROW_TILES), :],
                                     sem.at[buf])

    def gather_all(buf):
        return pltpu.make_async_copy(h2_hbm.at[pl.ds(0, rows), :], rin_ref.at[buf], sem.at[buf])

    def scatter_row(ids_ref, s):
        d = ids_ref[0, 0, s]
        return pltpu.make_async_copy(rout_ref.at[pl.ds(pl.multiple_of(s * ROW_TILES, ROW_TILES), ROW_TILES), :],
                                     y_hbm.at[pl.ds(pl.multiple_of(d * ROW_TILES, ROW_TILES), ROW_TILES), :],
                                     sem.at[2])

    def scatter_all():
        return pltpu.make_async_copy(rout_ref, y_hbm.at[pl.ds(0, rows), :], sem.at[2])

    @pl.when((blk == 0) & (f == 0))
    def _():
        rout_ref[...] = jnp.zeros_like(rout_ref)

        def start(s, _):
            gather_row(idx_ref, s, 0).start()
            return 0
        lax.fori_loop(0, ch, start, 0)

    @pl.when(f == 0)
    def _():
        gather_all(par).wait()
        for mc in range(nm):
            base = mc * mch * ROW_TILES
            parts = [rin_ref[par, pl.ds(base + kk, mch, stride=ROW_TILES), :] for kk in range(ROW_TILES)]
            xe_ref[mc * mch:(mc + 1) * mch, :] = jnp.concatenate(parts, axis=1).astype(BF16)

    for mc in range(nm):
        sl = slice(mc * mch, (mc + 1) * mch)
        xm = xe_ref[sl, :]
        a = _dot(xm, w1_ref[0])
        hid = (a * jax.nn.sigmoid(a) * _dot(xm, w3_ref[0])).astype(BF16)
        part = _dot(hid, w2_ref[0])
        first = (f * nm + mc) * per_chunk
        for i in range(per_chunk):
            gather_row(idxn_ref, first + i, 1 - par).start()
            scatter_row(dstp_ref, first + i).start()

        @pl.when(f == 0)
        def _():
            acc_ref[sl, :] = part

        @pl.when(f > 0)
        def _():
            acc_ref[sl, :] += part

    @pl.when(f == nf - 1)
    def _():
        scatter_all().wait()
        for mc in range(nm):
            sl = slice(mc * mch, (mc + 1) * mch)
            ye = acc_ref[sl, :] * gate_ref[sl, :]
            base = mc * mch * ROW_TILES
            for kk in range(ROW_TILES):
                rout_ref[pl.ds(base + kk, mch, stride=ROW_TILES), :] = ye[:, kk * LANES:(kk + 1) * LANES]

        @pl.when(blk == nblk - 1)
        def _():
            def start(s, _):
                scatter_row(dst_ref, s).start()
                return 0
            lax.fori_loop(0, ch, start, 0)
            scatter_all().wait()
            gather_all(1 - par).wait()


def _moe(idx, dst, gate, h2rows, w1, w3, w2, cap):
    ne = N_EXPERTS
    ch = min(MOE_ROWS, cap)
    nh = cap // ch
    nf = D_FF // FF_TILE
    mch = min(M_CHUNK, ch)
    assert ch % (nf * (ch // mch)) == 0
    smem = lambda: pl.BlockSpec((1, 1, ch), lambda e, h, f: (e * nh + h, 0, 0), memory_space=pltpu.SMEM)
    smem_next = pl.BlockSpec((1, 1, ch), lambda e, h, f: (jnp.minimum(e * nh + h + 1, ne * nh - 1), 0, 0),
                             memory_space=pltpu.SMEM)
    smem_prev = pl.BlockSpec((1, 1, ch), lambda e, h, f: (jnp.maximum(e * nh + h - 1, 0), 0, 0),
                             memory_space=pltpu.SMEM)
    return pl.pallas_call(
        functools.partial(_moe_kernel, ch=ch, nf=nf, mch=mch),
        grid=(ne, nh, nf),
        in_specs=[smem(), smem_next, smem_prev, smem(),
                  pl.BlockSpec((ch, 1), lambda e, h, f: (e * nh + h, 0)),
                  pl.BlockSpec((1, D_MODEL, FF_TILE), lambda e, h, f: (e, 0, f)),
                  pl.BlockSpec((1, D_MODEL, FF_TILE), lambda e, h, f: (e, 0, f)),
                  pl.BlockSpec((1, FF_TILE, D_MODEL), lambda e, h, f: (e, f, 0)),
                  pl.BlockSpec(memory_space=pl.ANY)],
        out_specs=pl.BlockSpec(memory_space=pl.ANY),
        out_shape=jax.ShapeDtypeStruct((ne * cap * ROW_TILES, LANES), F32),
        scratch_shapes=[pltpu.VMEM((2, ch * ROW_TILES, LANES), F32),
                        pltpu.VMEM((ch * ROW_TILES, LANES), F32),
                        pltpu.VMEM((ch, D_MODEL), BF16),
                        pltpu.VMEM((ch, D_MODEL), F32),
                        pltpu.SemaphoreType.DMA((3,))],
        compiler_params=_cparams(("arbitrary", "arbitrary", "arbitrary")),
        name="moe",
    )(idx.reshape(ne * nh, 1, ch), idx.reshape(ne * nh, 1, ch), dst.reshape(ne * nh, 1, ch),
      dst.reshape(ne * nh, 1, ch), gate.reshape(ne * cap, 1), w1, w3, w2, h2rows)


def _combine_kernel(pb_ref, pc_ref, pv_ref, y_ref, tokp_ref, cnt_ref, x1_ref, mod_ref, g2_ref, b2_ref,
                    o_ref, acc_ref, start_ref, count_ref, *, alpha):
    k = pl.program_id(0)
    blk = pb_ref[k]
    prev = pb_ref[jnp.maximum(k - 1, 0)]
    nxt = pb_ref[jnp.minimum(k + 1, pl.num_programs(0) - 1)]
    valid = pv_ref[k] == 1
    first = valid & ((k == 0) | (prev != blk))
    last = valid & ((k == pl.num_programs(0) - 1) | (nxt != blk) | (pv_ref[jnp.minimum(k + 1, pl.num_programs(0) - 1)] == 0))

    @pl.when(first)
    def _():
        acc_ref[...] = jnp.zeros_like(acc_ref)
        start_ref[...] = tokp_ref[0].astype(F32).T[:, 0:1].astype(I32)
        count_ref[...] = cnt_ref[0].astype(F32).T[:, 0:1].astype(I32)

    @pl.when(valid)
    def _():
        parts = [y_ref[pl.ds(kk, COMBINE_ROWS, stride=ROW_TILES), :] for kk in range(ROW_TILES)]
        rows = jnp.concatenate(parts, axis=1).astype(BF16)
        r = pc_ref[k] * COMBINE_ROWS + lax.broadcasted_iota(I32, (1, COMBINE_ROWS), 1)
        start = start_ref[...]
        own = jnp.where((r >= start) & (r < start + count_ref[...]), 1.0, 0.0).astype(BF16)
        acc_ref[...] += _dot(own, rows)

    @pl.when(last)
    def _():
        gate2 = mod_ref[0, 5:6, :]
        o_ref[...] = _ln(alpha * x1_ref[...] + gate2 * acc_ref[...]) * g2_ref[...] + b2_ref[...]


def _combine(pb, pc, pv, yrows, tokp_col, cnt_col, x1, mod3, wp, seq, alpha):
    t = x1.shape[0]
    tm = TM
    spb = seq // tm
    npairs = pb.shape[0]
    full = lambda a: pl.BlockSpec(a.shape, lambda k, pb, pc, pv: (0,) * a.ndim)
    row = pl.BlockSpec((tm, D_MODEL), lambda k, pb, pc, pv: (pb[k], 0))
    colspec = pl.BlockSpec((1, N_EXPERTS, tm), lambda k, pb, pc, pv: (pb[k], 0, 0))
    return pl.pallas_call(
        functools.partial(_combine_kernel, alpha=alpha),
        grid_spec=pltpu.PrefetchScalarGridSpec(
            num_scalar_prefetch=3,
            grid=(npairs,),
            in_specs=[pl.BlockSpec((COMBINE_ROWS * ROW_TILES, LANES), lambda k, pb, pc, pv: (pc[k], 0)),
                      colspec, colspec, row,
                      pl.BlockSpec((1, 6, D_MODEL), lambda k, pb, pc, pv: (pb[k] // spb, 0, 0)),
                      full(wp["ln2_g"]), full(wp["ln2_b"])],
            out_specs=row,
            scratch_shapes=[pltpu.VMEM((tm, D_MODEL), F32), pltpu.VMEM((tm, 1), I32), pltpu.VMEM((tm, 1), I32)]),
        out_shape=jax.ShapeDtypeStruct((t, D_MODEL), F32),
        compiler_params=_cparams(("arbitrary",)),
        name="combine",
    )(pb, pc, pv, yrows, tokp_col, cnt_col, x1, mod3, wp["ln2_g"], wp["ln2_b"])


def _pair_schedule(row_start, n_rows_total, nb):
    nchunks = n_rows_total // COMBINE_ROWS
    npairs = nb + nchunks
    rs = row_start
    re = jnp.concatenate([rs[1:], jnp.array([n_rows_total], I32)])
    c_lo = jnp.minimum(rs // COMBINE_ROWS, nchunks - 1)
    c_hi = jnp.maximum(c_lo, (re - 1) // COMBINE_ROWS)
    n_b = c_hi - c_lo + 1
    ends = jnp.cumsum(n_b)
    starts = ends - n_b
    k = jnp.arange(npairs, dtype=I32)
    valid = k < ends[-1]
    kk = jnp.minimum(k, ends[-1] - 1)
    b = jnp.sum(ends[None, :] <= kk[:, None], axis=1).astype(I32)
    c = c_lo[b] + (kk - starts[b])
    return b.astype(I32), c.astype(I32), valid.astype(I32)


def _pack_weights(l, seq, w_in, b_in, ln_v_g, ln_v_b, w_spatial, b_spatial, w_oa, q_norm_g, w_uq, kv_norm_g,
                  w_ukv, w_ob, w_out, ln1_g, ln1_b, w_router, ln2_g, ln2_b):
    half = QK_ROPE // 2
    z32 = lambda rows: jnp.zeros((rows, half), F32)

    def rope_cols(w):
        return jnp.concatenate([w[:, :half], z32(w.shape[0]), w[:, half:], z32(w.shape[0])], axis=1)

    wi, bi = w_in[l], b_in[l][None, :]
    off_cq, off_ckv, off_kr = 2 * GM_WIDTH, 2 * GM_WIDTH + Q_LORA, 2 * GM_WIDTH + Q_LORA + KV_LORA
    off_ga = off_kr + QK_ROPE

    def repack(a):
        return jnp.concatenate([a[:, :off_kr], rope_cols(a[:, off_kr:off_ga]), a[:, off_ga:]], axis=1)

    wq = w_uq[l].reshape(Q_LORA, N_HEADS, QK_NOPE + QK_ROPE)
    wq = jnp.concatenate([wq[:, :, :QK_NOPE],
                          wq[:, :, QK_NOPE:QK_NOPE + half], jnp.zeros((Q_LORA, N_HEADS, half), F32),
                          wq[:, :, QK_NOPE + half:], jnp.zeros((Q_LORA, N_HEADS, half), F32)], axis=2)
    wkv = w_ukv[l].reshape(KV_LORA, N_HEADS, QK_NOPE + V_DIM)
    inv = ROPE_BASE ** (-jnp.arange(half, dtype=F32) / half)
    ang = jnp.arange(seq, dtype=F32)[:, None] * inv[None, :]
    cos, sin, zs = jnp.cos(ang), jnp.sin(ang), jnp.zeros((seq, half), F32)
    wr_hi = w_router[l].astype(BF16)
    wr_lo = (w_router[l] - wr_hi.astype(F32)).astype(BF16)
    wr = (jnp.zeros((D_MODEL, 2 * LANES), BF16).at[:, :N_EXPERTS].set(wr_hi)
          .at[:, LANES:LANES + N_EXPERTS].set(wr_lo))
    return {
        "w_in": repack(wi).astype(BF16), "b_in": repack(bi),
        "ln_v_g": ln_v_g[l][None, :], "ln_v_b": ln_v_b[l][None, :],
        "w_s": w_spatial[l].astype(BF16),
        "b_s": jnp.broadcast_to(b_spatial[l][:, :, None], (GM_GROUPS, CHUNK, CHUNK)),
        "w_oa": w_oa[l].astype(BF16),
        "q_norm_g": q_norm_g[l][None, :], "w_uq": wq.reshape(Q_LORA, N_HEADS * HEAD_W).T.astype(BF16),
        "kv_norm_g": kv_norm_g[l][None, :],
        "w_uk": wkv[:, :, :QK_NOPE].reshape(KV_LORA, N_HEADS * QK_NOPE).astype(BF16),
        "w_uvt": wkv[:, :, QK_NOPE:].reshape(KV_LORA, N_HEADS * V_DIM).T.astype(BF16),
        "cos": jnp.concatenate([cos, zs, cos, zs], axis=1), "sin": jnp.concatenate([-sin, zs, sin, zs], axis=1),
        "w_ob": w_ob[l].astype(BF16), "w_out": w_out[l].astype(BF16),
        "ln1_g": ln1_g[l][None, :], "ln1_b": ln1_b[l][None, :], "w_router": wr,
        "ln2_g": ln2_g[l][None, :], "ln2_b": ln2_b[l][None, :],
    }


def _layer(x, c, l, alpha, w_ada, b_ada, w1, w3, w2, packed):
    bsz, seq, _ = x.shape
    t = bsz * seq
    cap = EC_CAPACITY_FACTOR * t // N_EXPERTS
    x2 = x.reshape(t, D_MODEL)
    mod3 = _mod(c, w_ada[l], b_ada[l])
    u, vn, q, k, vt, sga, sgb = _inproj(x2, mod3, packed, bsz, seq)
    ap = _spatial(u, vn, sga, packed)
    o = _attention(q, k, vt).reshape(t, D_MODEL)
    x1, h2, aff3 = _post(x2, mod3, o, ap, sgb, packed, bsz, seq, alpha)
    nb = t // TM
    spos, flat, cum, tokp, cnt = _route(aff3, cap)
    idx, gate, dst = _slots(cum[:, :, 0], spos, aff3, flat, cap)
    yrows = _moe(idx, dst, gate, h2, w1[l], w3[l], w2[l], cap)
    pb, pc, pv = _pair_schedule(tokp[:, 0, 0], N_EXPERTS * cap, nb)
    y = _combine(pb, pc, pv, yrows, tokp, cnt, x1, mod3, packed, seq, alpha)
    return y.reshape(bsz, seq, D_MODEL)


def kernel(x_prompt, x_sample, c_prompt, c_sample, w_ada, b_ada, w_in, b_in, ln_v_g, ln_v_b, w_spatial, b_spatial,
           w_oa, q_norm_g, w_uq, kv_norm_g, w_ukv, w_ob, w_out, ln1_g, ln1_b, w_router, w1, w3, w2, ln2_g, ln2_b):
    depth = w_ada.shape[0]
    alpha = (2.0 * depth) ** 0.25
    w1, w3, w2 = w1.astype(BF16), w3.astype(BF16), w2.astype(BF16)
    outs = []
    for x, c in ((x_prompt, c_prompt), (x_sample, c_sample)):
        for l in range(depth):
            packed = _pack_weights(l, x.shape[1], w_in, b_in, ln_v_g, ln_v_b, w_spatial, b_spatial, w_oa,
                                   q_norm_g, w_uq, kv_norm_g, w_ukv, w_ob, w_out, ln1_g, ln1_b, w_router,
                                   ln2_g, ln2_b)
            x = _layer(x, c, l, alpha, w_ada, b_ada, w1, w3, w2, packed)
        outs.append(x)
    return tuple(outs)
```

```python
import functools
import math

import jax
import jax.numpy as jnp
from jax import lax
from jax.experimental import pallas as pl
from jax.experimental.pallas import tpu as pltpu

F32 = jnp.float32
BF16 = jnp.bfloat16
I32 = jnp.int32

D_MODEL = 1024
GM_WIDTH = 1024
GM_GROUPS = 8
CHUNK = 128
N_HEADS = 8
QK_NOPE = 128
QK_ROPE = 64
V_DIM = 128
V_ROWS = 144
Q_LORA = 384
KV_LORA = 256
ROPE_BASE = 10000.0
N_EXPERTS = 16
EC_CAPACITY_FACTOR = 2
D_FF = 2048
LN_EPS = 1e-5
RMS_EPS = 1e-6

LANES = 128
SUBLANES = 8
HEAD_W = 256
ROW_TILES = D_MODEL // LANES

C_U, C_V, C_CQ, C_CKV, C_KR, C_GA, C_GB, C_END = 0, 1024, 2048, 2432, 2688, 2816, 3840, 4864

TM_IN = 512
ATTN_VMEM_RESERVE = 6 * 1024 * 1024
TM = 256
TM_WIDE = 1024
SLOT_TILE = 256
COMBINE_ROWS = 1024
SLOT_TILES_PER_STEP = 4
SLOT_BLOCKS_PER_TRIP = 6
FF_TILE = 2048
M_CHUNK = 512
MOE_ROWS = 1024
V7X_VMEM_BYTES = 64 * 1024 * 1024
VMEM_LIMIT = V7X_VMEM_BYTES * 7 // 8
BF16_INT_BITS = 8
BF16_INT = 1 << BF16_INT_BITS
NO_SLOT = -(1 << 30)


def _cparams(sem):
    return pltpu.CompilerParams(dimension_semantics=sem, vmem_limit_bytes=VMEM_LIMIT)


def _ln(x):
    mu = jnp.mean(x, axis=-1, keepdims=True)
    xc = x - mu
    var = jnp.mean(xc * xc, axis=-1, keepdims=True)
    return xc * lax.rsqrt(var + LN_EPS)


def _rms(x):
    return x * lax.rsqrt(jnp.mean(x * x, axis=-1, keepdims=True) + RMS_EPS)


def _gelu(x):
    return 0.5 * x * (1.0 + lax.erf(x * (2.0 ** -0.5)))


def _dot(a, b):
    return jnp.dot(a, b, preferred_element_type=F32)


def _dot_nt(a, b):
    return lax.dot_general(a, b, (((1,), (1,)), ((), ())), preferred_element_type=F32)


def _mod_kernel(c_ref, w_ref, b_ref, o_ref):
    c = c_ref[...]
    s = c * jax.nn.sigmoid(c)
    o_ref[...] = jnp.dot(s, w_ref[...], preferred_element_type=F32,
                         precision=lax.Precision.HIGHEST) + b_ref[...]


def _mod(c, w_ada, b_ada):
    bsz = c.shape[0]
    bp = -(-bsz // SUBLANES) * SUBLANES
    cp = jnp.zeros((bp, D_MODEL), F32).at[:bsz].set(c)
    n = w_ada.shape[1]
    tn = 1024
    out = pl.pallas_call(
        _mod_kernel,
        grid=(n // tn,),
        in_specs=[pl.BlockSpec((bp, D_MODEL), lambda j: (0, 0)),
                  pl.BlockSpec((D_MODEL, tn), lambda j: (0, j)),
                  pl.BlockSpec((1, tn), lambda j: (0, j))],
        out_specs=pl.BlockSpec((bp, tn), lambda j: (0, j)),
        out_shape=jax.ShapeDtypeStruct((bp, n), F32),
        compiler_params=_cparams(("arbitrary",)),
        name="mod",
    )(cp, w_ada, b_ada.reshape(1, n))
    return out[:bsz].reshape(bsz, 6, D_MODEL)


def _inproj_kernel(x_ref, mod_ref, w_ref, b_ref, lvg_ref, lvb_ref, qg_ref, wuq_ref, kvg_ref,
                   wuk_ref, wuvt_ref, cos_ref, sin_ref, cost_ref, sint_ref,
                   u_ref, vn_ref, q_ref, k_ref, vt_ref, sga_ref, sgb_ref, *, qscale):
    x = x_ref[...]
    sh1 = mod_ref[0, 0:1, :]
    sc1 = mod_ref[0, 1:2, :]
    h = (_ln(x) * (1.0 + sc1) + sh1).astype(BF16)

    def proj(a, b):
        return _dot(h, w_ref[:, a:b]) + b_ref[:, a:b]

    cos = cos_ref[...]
    sin = sin_ref[...]

    def rope(t):
        return t * cos + pltpu.roll(t, 64, 1) * sin

    u_ref[...] = _gelu(proj(C_U, C_V)).astype(BF16)
    v = _gelu(proj(C_V, C_CQ))
    vn_ref[...] = (_ln(v) * lvg_ref[...] + lvb_ref[...]).astype(BF16)

    cqn = (_rms(proj(C_CQ, C_CKV)) * qg_ref[...]).astype(BF16)
    qt = _dot_nt(wuq_ref[...], cqn) * qscale
    cost = cost_ref[...]
    sint = sint_ref[...]
    for hh in range(N_HEADS):
        o = hh * HEAD_W
        q_ref[0, hh, 0:QK_NOPE, :] = qt[o:o + QK_NOPE, :].astype(BF16)
        r = qt[o + QK_NOPE:o + HEAD_W, :]
        swapped = jnp.concatenate([r[64:], r[:64]], axis=0)
        q_ref[0, hh, QK_NOPE:HEAD_W, :] = (r * cost + swapped * sint).astype(BF16)

    ckvn = (_rms(proj(C_CKV, C_KR)) * kvg_ref[...]).astype(BF16)
    kn = _dot(ckvn, wuk_ref[...])
    kr = rope(proj(C_KR, C_GA)).astype(BF16)
    for hh in range(N_HEADS):
        k_ref[0, hh, :, 0:QK_NOPE] = kn[:, hh * QK_NOPE:(hh + 1) * QK_NOPE].astype(BF16)
        k_ref[0, hh, :, QK_NOPE:HEAD_W] = kr
    vt = _dot_nt(wuvt_ref[...], ckvn)
    tm = vt.shape[1]
    tail = jnp.where(lax.broadcasted_iota(I32, (V_ROWS - V_DIM, tm), 0) == 0, 1.0, 0.0).astype(BF16)
    for hh in range(N_HEADS):
        vt_ref[0, hh, 0, 0:V_DIM, :] = vt[hh * V_DIM:(hh + 1) * V_DIM, :].astype(BF16)
        vt_ref[0, hh, 0, V_DIM:V_ROWS, :] = tail

    sga_ref[...] = jax.nn.sigmoid(proj(C_GA, C_GB)).astype(BF16)
    sgb_ref[...] = jax.nn.sigmoid(proj(C_GB, C_END)).astype(BF16)


def _inproj(x2, mod3, wp, bsz, seq):
    t = bsz * seq
    tm = min(TM_IN, seq)
    spb = seq // tm
    full = lambda a: pl.BlockSpec(a.shape, lambda i: (0,) * a.ndim)
    qscale = (QK_NOPE + QK_ROPE) ** -0.5 * math.log2(math.e)
    row = pl.BlockSpec((tm, D_MODEL), lambda i: (i, 0))
    qk_spec = pl.BlockSpec((1, N_HEADS, tm, HEAD_W), lambda i: (i // spb, 0, i % spb, 0))
    qt_spec = pl.BlockSpec((1, N_HEADS, HEAD_W, tm), lambda i: (i // spb, 0, 0, i % spb))
    ins = [x2, mod3, wp["w_in"], wp["b_in"], wp["ln_v_g"], wp["ln_v_b"], wp["q_norm_g"], wp["w_uq"],
           wp["kv_norm_g"], wp["w_uk"], wp["w_uvt"], wp["cos"], wp["sin"], wp["cos"].T, wp["sin"].T]
    in_specs = [row, pl.BlockSpec((1, 6, D_MODEL), lambda i: (i // spb, 0, 0))]
    in_specs += [full(a) for a in ins[2:11]]
    in_specs += [pl.BlockSpec((tm, LANES), lambda i: (i % spb, 0))] * 2
    in_specs += [pl.BlockSpec((LANES, tm), lambda i: (0, i % spb))] * 2
    return pl.pallas_call(
        functools.partial(_inproj_kernel, qscale=qscale),
        grid=(t // tm,),
        in_specs=in_specs,
        out_specs=[row, row, qt_spec, qk_spec,
                   pl.BlockSpec((1, N_HEADS, 1, V_ROWS, tm), lambda i: (i // spb, 0, i % spb, 0, 0)),
                   row, row],
        out_shape=[jax.ShapeDtypeStruct((t, D_MODEL), BF16),
                   jax.ShapeDtypeStruct((t, D_MODEL), BF16),
                   jax.ShapeDtypeStruct((bsz, N_HEADS, HEAD_W, seq), BF16),
                   jax.ShapeDtypeStruct((bsz, N_HEADS, seq, HEAD_W), BF16),
                   jax.ShapeDtypeStruct((bsz, N_HEADS, spb, V_ROWS, tm), BF16),
                   jax.ShapeDtypeStruct((t, D_MODEL), BF16),
                   jax.ShapeDtypeStruct((t, D_MODEL), BF16)],
        compiler_params=_cparams(("arbitrary",)),
        name="inproj",
    )(*ins)


def _spatial_kernel(u_ref, vn_ref, ws_ref, bs_ref, woa_ref, sga_ref, o_ref, g_ref):
    tm = u_ref.shape[0]
    nc = tm // CHUNK
    for g in range(GM_GROUPS):
        c0 = g * CHUNK
        rhs = jnp.concatenate([vn_ref[n * CHUNK:(n + 1) * CHUNK, c0:c0 + CHUNK] for n in range(nc)], axis=1)
        mixed = _dot(ws_ref[g], rhs)
        bias = bs_ref[g]
        for n in range(nc):
            m = mixed[:, n * CHUNK:(n + 1) * CHUNK] + bias
            uu = u_ref[n * CHUNK:(n + 1) * CHUNK, c0:c0 + CHUNK].astype(F32)
            g_ref[n * CHUNK:(n + 1) * CHUNK, c0:c0 + CHUNK] = (uu * m).astype(BF16)
    ya = _dot(g_ref[...], woa_ref[...])
    o_ref[...] = (sga_ref[...].astype(F32) * ya).astype(BF16)


def _spatial(u, vn, sga, wp):
    t = u.shape[0]
    tm = TM_WIDE
    row = pl.BlockSpec((tm, D_MODEL), lambda i: (i, 0))
    full = lambda a: pl.BlockSpec(a.shape, lambda i: (0,) * a.ndim)
    return pl.pallas_call(
        _spatial_kernel,
        grid=(t // tm,),
        in_specs=[row, row, full(wp["w_s"]), full(wp["b_s"]), full(wp["w_oa"]), row],
        out_specs=row,
        out_shape=jax.ShapeDtypeStruct((t, D_MODEL), BF16),
        scratch_shapes=[pltpu.VMEM((tm, GM_WIDTH), BF16)],
        compiler_params=_cparams(("arbitrary",)),
        name="spatial",
    )(u, vn, wp["w_s"], wp["b_s"], wp["w_oa"], sga)


def _attn_kernel(q_ref, k_ref, vt_ref, o_ref, acc_ref, sa_ref, sb_ref, pa_ref, pb_ref, *, nk, tk):
    q = q_ref[0, 0]
    tq = q.shape[1]
    acc_ref[...] = jnp.zeros_like(acc_ref)

    def scores(c, s_ref):
        s = _dot(k_ref[0, 0, pl.ds(pl.multiple_of(c * tk, tk), tk), :], q)
        s_ref[...] = s
        return jnp.max(s.reshape(tk // SUBLANES, SUBLANES, tq), axis=0)

    def softmax(s_ref, p_ref, m, cmax):
        m_new = jnp.maximum(m, jnp.max(cmax, axis=0, keepdims=True))
        p_ref[...] = jnp.exp2(s_ref[...] - m_new).astype(BF16)
        return m_new, jnp.exp2(m - m_new)

    def accumulate(alpha, c, p_ref):
        return alpha * acc_ref[...] + _dot(vt_ref[0, 0, c], p_ref[...])

    cmax_a = scores(0, sa_ref)
    cmax_b = scores(1, sb_ref)
    m, alpha = softmax(sa_ref, pa_ref, jnp.full((1, tq), -jnp.inf, F32), cmax_a)

    def body(i, carry):
        m, alpha, cmax_b = carry
        h = 2 * i + 1
        cmax_a = scores(h + 1, sa_ref)
        acc_ref[...] = accumulate(alpha, h - 1, pa_ref)
        m, alpha = softmax(sb_ref, pb_ref, m, cmax_b)
        cmax_b = scores(h + 2, sb_ref)
        acc_ref[...] = accumulate(alpha, h, pb_ref)
        m, alpha = softmax(sa_ref, pa_ref, m, cmax_a)
        return m, alpha, cmax_b

    m, alpha, cmax_b = lax.fori_loop(0, nk // 2 - 1, body, (m, alpha, cmax_b))
    acc_ref[...] = accumulate(alpha, nk - 2, pa_ref)
    m, alpha = softmax(sb_ref, pb_ref, m, cmax_b)
    acc = accumulate(alpha, nk - 1, pb_ref)
    o = acc[0:V_DIM, :] / acc[V_DIM:V_DIM + 1, :]
    o_ref[0] = o.T.astype(BF16)


def _attention(q, k, vt):
    bsz, nh, _, seq = q.shape
    nk, tk = vt.shape[2], vt.shape[4]
    assert nk % 2 == 0
    kv_bytes = 2 * (seq * HEAD_W + nk * V_ROWS * tk)
    per_query = tk * (4 + 4 + 2 + 2) + 2 * 2 * HEAD_W + 2 * 2 * V_DIM + 4 * V_ROWS

    def tile(kv_buffers):
        room = VMEM_LIMIT - ATTN_VMEM_RESERVE - kv_buffers * kv_bytes
        return min(seq, 1 << (room // per_query).bit_length() - 1)

    kv_buffers = 2 if tile(2) == tile(1) else 1
    tq = tile(kv_buffers)
    kv_mode = pl.Buffered(kv_buffers)
    return pl.pallas_call(
        functools.partial(_attn_kernel, nk=nk, tk=tk),
        grid=(bsz, nh, seq // tq),
        in_specs=[pl.BlockSpec((1, 1, HEAD_W, tq), lambda b, h, i: (b, h, 0, i)),
                  pl.BlockSpec((1, 1, seq, HEAD_W), lambda b, h, i: (b, h, 0, 0), pipeline_mode=kv_mode),
                  pl.BlockSpec((1, 1, nk, V_ROWS, tk), lambda b, h, i: (b, h, 0, 0, 0), pipeline_mode=kv_mode)],
        out_specs=pl.BlockSpec((1, tq, V_DIM), lambda b, h, i: (b, i, h)),
        out_shape=jax.ShapeDtypeStruct((bsz, seq, nh * V_DIM), BF16),
        scratch_shapes=[pltpu.VMEM((V_ROWS, tq), F32), pltpu.VMEM((tk, tq), F32), pltpu.VMEM((tk, tq), F32),
                        pltpu.VMEM((tk, tq), BF16), pltpu.VMEM((tk, tq), BF16)],
        compiler_params=_cparams(("arbitrary", "arbitrary", "arbitrary")),
        name="attn",
    )(q, k, vt)


def _post_kernel(x_ref, mod_ref, o_ref, ap_ref, sgb_ref, wob_ref, wout_ref, g1_ref, b1_ref, wr_ref,
                 x1_ref, h2_ref, aff_ref, *, alpha):
    gate1 = mod_ref[0, 2:3, :]
    sh2 = mod_ref[0, 3:4, :]
    sc2 = mod_ref[0, 4:5, :]
    for j in range(x_ref.shape[0] // TM):
        rows = slice(j * TM, (j + 1) * TM)
        yb = _dot(o_ref[rows, :], wob_ref[...])
        mixin = (ap_ref[rows, :].astype(F32) + sgb_ref[rows, :].astype(F32) * yb).astype(BF16)
        mix = _dot(mixin, wout_ref[...])
        x1 = _ln(alpha * x_ref[rows, :] + gate1 * mix) * g1_ref[...] + b1_ref[...]
        x1_ref[rows, :] = x1
        h2 = _ln(x1) * (1.0 + sc2) + sh2
        for kk in range(ROW_TILES):
            h2_ref[pl.ds(j * TM * ROW_TILES + kk, TM, stride=ROW_TILES), :] = h2[:, kk * LANES:(kk + 1) * LANES]
        h_hi = h2.astype(BF16)
        h_lo = (h2 - h_hi.astype(F32)).astype(BF16)
        prod = _dot(h_hi, wr_ref[...]) + _dot(h_lo, wr_ref[...])
        logits = prod[:, 0:LANES] + prod[:, LANES:2 * LANES]
        lane = lax.broadcasted_iota(I32, logits.shape, 1)
        logits = jnp.where(lane < N_EXPERTS, logits, -jnp.inf)
        ex = jnp.exp(logits - jnp.max(logits, axis=-1, keepdims=True))
        aff = ex / jnp.sum(ex, axis=-1, keepdims=True)
        aff_ref[j] = aff.T[0:N_EXPERTS, :]


def _post(x2, mod3, o, ap, sgb, wp, bsz, seq, alpha):
    t = x2.shape[0]
    tm = TM_WIDE
    spb = seq // tm
    row = pl.BlockSpec((tm, D_MODEL), lambda i: (i, 0))
    full = lambda a: pl.BlockSpec(a.shape, lambda i: (0,) * a.ndim)
    return pl.pallas_call(
        functools.partial(_post_kernel, alpha=alpha),
        grid=(t // tm,),
        in_specs=[row, pl.BlockSpec((1, 6, D_MODEL), lambda i: (i // spb, 0, 0)), row, row, row,
                  full(wp["w_ob"]), full(wp["w_out"]), full(wp["ln1_g"]), full(wp["ln1_b"]),
                  full(wp["w_router"])],
        out_specs=[row, pl.BlockSpec((tm * ROW_TILES, LANES), lambda i: (i, 0)),
                   pl.BlockSpec((tm // TM, N_EXPERTS, TM), lambda i: (i, 0, 0))],
        out_shape=[jax.ShapeDtypeStruct((t, D_MODEL), F32),
                   jax.ShapeDtypeStruct((t * ROW_TILES, LANES), F32),
                   jax.ShapeDtypeStruct((t // TM, N_EXPERTS, TM), F32)],
        compiler_params=_cparams(("arbitrary",)),
        name="post",
    )(x2, mod3, o, ap, sgb, wp["w_ob"], wp["w_out"], wp["ln1_g"], wp["ln1_b"], wp["w_router"])


def _route_kernel(aff_ref, spos_ref, flat_ref, cum_ref, tokp_ref, cnt_ref, op_ref, *, cap, nb):
    bits = pltpu.bitcast(aff_ref[...], I32)
    tm = bits.shape[2]

    def count(mask):
        c = jnp.sum(jnp.where(mask, 1, 0), axis=0, keepdims=True)
        return jnp.sum(c, axis=2, keepdims=True)

    def bisect(i, thr):
        cand = thr | lax.shift_left(jnp.int32(1), 30 - i)
        return jnp.where(count(bits >= cand) >= cap, cand, thr)

    thr = lax.fori_loop(0, 31, bisect, jnp.zeros((1, N_EXPERTS, 1), I32))
    gt = bits > thr
    eq = bits == thr
    need = cap - count(gt)

    n = nb * N_EXPERTS
    ebits = N_EXPERTS.bit_length() - 1
    emask = N_EXPERTS - 1
    r = lax.broadcasted_iota(I32, (tm, tm), 0)
    c = lax.broadcasted_iota(I32, (tm, tm), 1)
    upper = jnp.where(r <= c, 1.0, 0.0).astype(BF16)
    ones = jnp.ones((tm, tm), BF16)
    rows_per = min(256, n)

    def fill(pred):
        for k in range(n // rows_per):
            ri = k * rows_per + lax.broadcasted_iota(I32, (rows_per, n), 0)
            ci = lax.broadcasted_iota(I32, (rows_per, n), 1)
            op_ref[k * rows_per:(k + 1) * rows_per, :] = jnp.where(pred(ri, ci), 1.0, 0.0).astype(BF16)

    def prefix(x3):
        x = x3.reshape(n, tm)
        xb = x.astype(BF16)
        incl = _dot(xb, upper)
        tot = _dot(xb, ones)
        hi = jnp.floor(tot * (1.0 / BF16_INT))
        lo = tot - float(BF16_INT) * hi
        offs = float(BF16_INT) * _dot(op_ref[...], hi.astype(BF16)) + _dot(op_ref[...], lo.astype(BF16))
        shape = (nb, N_EXPERTS, tm)
        return (incl - x + offs).reshape(shape), offs.reshape(shape), tot.reshape(shape)

    fill(lambda ri, ci: ((ri & emask) == (ci & emask)) & (ci < ri))
    rank_eq, _, _ = prefix(jnp.where(eq, 1.0, 0.0))
    sel = gt | (eq & (rank_eq < need.astype(F32)))
    self32 = jnp.where(sel, 1.0, 0.0)
    pos, offs, tot = prefix(self32)
    cum_ref[0:nb] = offs[:, :, 0:LANES].astype(I32)
    cum_ref[nb] = (offs[nb - 1] + tot[nb - 1])[:, 0:LANES].astype(I32)
    spos_ref[...] = jnp.where(sel, pos.astype(I32), -1)

    cnt = jnp.sum(self32, axis=1, keepdims=True)
    cnt_ref[...] = jnp.broadcast_to(cnt, cnt_ref.shape).astype(I32)
    tokp, _, _ = prefix(jnp.broadcast_to(cnt, (nb, N_EXPERTS, tm)))
    tokp_ref[...] = tokp.astype(I32)

    fill(lambda ri, ci: ((ri >> ebits) == (ci >> ebits)) & ((ci & emask) < (ri & emask)))
    within = _dot(op_ref[...], self32.reshape(n, tm).astype(BF16)).reshape(nb, N_EXPERTS, tm)
    flat_ref[...] = (tokp + within).astype(I32)


def _route(aff3, cap):
    nb, ne, tm = aff3.shape
    assert ne == N_EXPERTS and ne & (ne - 1) == 0 and ne * tm <= BF16_INT * BF16_INT
    shp = jax.ShapeDtypeStruct((nb, ne, tm), I32)
    return pl.pallas_call(
        functools.partial(_route_kernel, cap=cap, nb=nb),
        out_shape=[shp, shp, jax.ShapeDtypeStruct((nb + 1, ne, LANES), I32), shp, shp],
        scratch_shapes=[pltpu.VMEM((nb * ne, nb * ne), BF16)],
        compiler_params=pltpu.CompilerParams(vmem_limit_bytes=VMEM_LIMIT),
        name="route",
    )(aff3)


def _slots_kernel(blo_ref, bhi_ref, spos_ref, aff_ref, flat_ref, idx_ref, gate_ref, dst_ref, *, tm, nt, tps):
    e = pl.program_id(0)
    for jj in range(tps):
        _slot_tile(e, pl.program_id(1) * tps + jj, jj, blo_ref, bhi_ref, spos_ref, aff_ref, flat_ref,
                   idx_ref, gate_ref, dst_ref, tm, nt)


def _slot_tile(e, j, jj, blo_ref, bhi_ref, spos_ref, aff_ref, flat_ref, idx_ref, gate_ref, dst_ref, tm, nt):
    slot = j * SLOT_TILE + lax.broadcasted_iota(I32, (SLOT_TILE, 1), 0)
    lane = lax.broadcasted_iota(I32, (1, tm), 1).astype(F32)
    zero = jnp.zeros((1, tm), F32)
    lo = blo_ref[e * nt + j]
    hi = bhi_ref[e * nt + j]

    def block(b, off):
        onehot = jnp.where(slot == spos_ref[b, pl.ds(e, 1), :] + off, 1.0, 0.0).astype(BF16)
        g = aff_ref[b, pl.ds(e, 1), :]
        g0 = g.astype(BF16).astype(F32)
        g1 = (g - g0).astype(BF16).astype(F32)
        g2 = g - g0 - g1
        fl = flat_ref[b, pl.ds(e, 1), :]
        rows = [lane, (jnp.zeros((1, tm), I32) + b).astype(F32), g0, g1, g2,
                lax.shift_right_logical(fl, BF16_INT_BITS).astype(F32), (fl & (BF16_INT - 1)).astype(F32), zero]
        vals = jnp.concatenate(rows + [zero] * 8, axis=0).astype(BF16)
        return _dot_nt(vals, onehot)

    def body(i, acc):
        b0 = lo + SLOT_BLOCKS_PER_TRIP * i
        for u in range(SLOT_BLOCKS_PER_TRIP):
            b = b0 + u
            acc = acc + block(jnp.minimum(b, hi - 1), jnp.where(b < hi, 0, NO_SLOT))
        return acc

    trips = (hi - lo + SLOT_BLOCKS_PER_TRIP - 1) // SLOT_BLOCKS_PER_TRIP
    a = lax.fori_loop(0, trips, body, jnp.zeros((16, SLOT_TILE), F32))
    idx_ref[jj] = (a[1:2] * tm + a[0:1]).astype(I32)
    gate_ref[jj] = a[2:3] + a[3:4] + a[4:5]
    dst_ref[jj] = (a[5:6] * float(BF16_INT) + a[6:7]).astype(I32)


def _slots(cum, spos, aff3, flat, cap):
    nb, ne, tm = spos.shape
    nt = cap // SLOT_TILE
    assert tm <= BF16_INT and nb <= BF16_INT and ne * cap <= BF16_INT * BF16_INT
    edges = jnp.arange(nt + 1, dtype=I32) * SLOT_TILE
    cs, ce = cum[:-1].T, cum[1:].T
    blo = jnp.sum(ce[:, None, :] <= edges[None, :-1, None], axis=2).astype(I32).reshape(-1)
    bhi = jnp.sum(cs[:, None, :] < edges[None, 1:, None], axis=2).astype(I32).reshape(-1)
    full = lambda a: pl.BlockSpec(a.shape, lambda e, j, lo, hi: (0,) * a.ndim)
    tps = math.gcd(nt, SLOT_TILES_PER_STEP)
    out = pl.BlockSpec((tps, 1, SLOT_TILE), lambda e, j, lo, hi: (e * (nt // tps) + j, 0, 0))
    return pl.pallas_call(
        functools.partial(_slots_kernel, tm=tm, nt=nt, tps=tps),
        grid_spec=pltpu.PrefetchScalarGridSpec(
            num_scalar_prefetch=2,
            grid=(ne, nt // tps),
            in_specs=[full(spos), full(aff3), full(flat)],
            out_specs=[out, out, out]),
        out_shape=[jax.ShapeDtypeStruct((ne * nt, 1, SLOT_TILE), I32),
                   jax.ShapeDtypeStruct((ne * nt, 1, SLOT_TILE), F32),
                   jax.ShapeDtypeStruct((ne * nt, 1, SLOT_TILE), I32)],
        compiler_params=_cparams(("arbitrary", "arbitrary")),
        name="slots",
    )(blo, bhi, spos, aff3, flat)


def _moe_kernel(idx_ref, idxn_ref, dstp_ref, dst_ref, gate_ref, w1_ref, w3_ref, w2_ref, h2_hbm, y_hbm,
                rin_ref, rout_ref, xe_ref, acc_ref, sem, *, ch, nf, mch):
    f = pl.program_id(2)
    blk = pl.program_id(0) * pl.num_programs(1) + pl.program_id(1)
    nblk = pl.num_programs(0) * pl.num_programs(1)
    par = blk & 1
    nm = ch // mch
    per_chunk = ch // (nf * nm)
    rows = ch * ROW_TILES

    def gather_row(ids_ref, s, buf):
        t = ids_ref[0, 0, s]
        return pltpu.make_async_copy(h2_hbm.at[pl.ds(pl.multiple_of(t * ROW_TILES, ROW_TILES), ROW_TILES), :],
                                     rin_ref.at[buf, pl.ds(pl.multiple_of(s * ROW_TILES, ROW_TILES), ROW_TILES), :],
                                     sem.at[buf])

    def gather_all(buf):
        return pltpu.make_async_copy(h2_hbm.at[pl.ds(0, rows), :], rin_ref.at[buf], sem.at[buf])

    def scatter_row(ids_ref, s):
        d = ids_ref[0, 0, s]
        return pltpu.make_async_copy(rout_ref.at[pl.ds(pl.multiple_of(s * ROW_TILES, ROW_TILES), ROW_TILES), :],
                                     y_hbm.at[pl.ds(pl.multiple_of(d * ROW_TILES, ROW_TILES), ROW_TILES), :],
                                     sem.at[2])

    def scatter_all():
        return pltpu.make_async_copy(rout_ref, y_hbm.at[pl.ds(0, rows), :], sem.at[2])

    @pl.when((blk == 0) & (f == 0))
    def _():
        rout_ref[...] = jnp.zeros_like(rout_ref)

        def start(s, _):
            gather_row(idx_ref, s, 0).start()
            return 0
        lax.fori_loop(0, ch, start, 0)

    @pl.when(f == 0)
    def _():
        gather_all(par).wait()
        for mc in range(nm):
            base = mc * mch * ROW_TILES
            parts = [rin_ref[par, pl.ds(base + kk, mch, stride=ROW_TILES), :] for kk in range(ROW_TILES)]
            xe_ref[mc * mch:(mc + 1) * mch, :] = jnp.concatenate(parts, axis=1).astype(BF16)

    for mc in range(nm):
        sl = slice(mc * mch, (mc + 1) * mch)
        xm = xe_ref[sl, :]
        a = _dot(xm, w1_ref[0])
        hid = (a * jax.nn.sigmoid(a) * _dot(xm, w3_ref[0])).astype(BF16)
        part = _dot(hid, w2_ref[0])
        first = (f * nm + mc) * per_chunk
        for i in range(per_chunk):
            gather_row(idxn_ref, first + i, 1 - par).start()
            scatter_row(dstp_ref, first + i).start(priority=i % 2)

        @pl.when(f == 0)
        def _():
            acc_ref[sl, :] = part

        @pl.when(f > 0)
        def _():
            acc_ref[sl, :] += part

    @pl.when(f == nf - 1)
    def _():
        scatter_all().wait()
        for mc in range(nm):
            sl = slice(mc * mch, (mc + 1) * mch)
            ye = acc_ref[sl, :] * gate_ref[sl, :]
            base = mc * mch * ROW_TILES
            for kk in range(ROW_TILES):
                rout_ref[pl.ds(base + kk, mch, stride=ROW_TILES), :] = ye[:, kk * LANES:(kk + 1) * LANES]

        @pl.when(blk == nblk - 1)
        def _():
            def start(s, _):
                scatter_row(dst_ref, s).start()
                return 0
            lax.fori_loop(0, ch, start, 0)
            scatter_all().wait()
            gather_all(1 - par).wait()


def _moe(idx, dst, gate, h2rows, w1, w3, w2, cap):
    ne = N_EXPERTS
    ch = min(MOE_ROWS, cap)
    nh = cap // ch
    nf = D_FF // FF_TILE
    mch = min(M_CHUNK, ch)
    assert ch % (nf * (ch // mch)) == 0
    smem = lambda: pl.BlockSpec((1, 1, ch), lambda e, h, f: (e * nh + h, 0, 0), memory_space=pltpu.SMEM)
    smem_next = pl.BlockSpec((1, 1, ch), lambda e, h, f: (jnp.minimum(e * nh + h + 1, ne * nh - 1), 0, 0),
                             memory_space=pltpu.SMEM)
    smem_prev = pl.BlockSpec((1, 1, ch), lambda e, h, f: (jnp.maximum(e * nh + h - 1, 0), 0, 0),
                             memory_space=pltpu.SMEM)
    return pl.pallas_call(
        functools.partial(_moe_kernel, ch=ch, nf=nf, mch=mch),
        grid=(ne, nh, nf),
        in_specs=[smem(), smem_next, smem_prev, smem(),
                  pl.BlockSpec((ch, 1), lambda e, h, f: (e * nh + h, 0)),
                  pl.BlockSpec((1, D_MODEL, FF_TILE), lambda e, h, f: (e, 0, f)),
                  pl.BlockSpec((1, D_MODEL, FF_TILE), lambda e, h, f: (e, 0, f)),
                  pl.BlockSpec((1, FF_TILE, D_MODEL), lambda e, h, f: (e, f, 0)),
                  pl.BlockSpec(memory_space=pl.ANY)],
        out_specs=pl.BlockSpec(memory_space=pl.ANY),
        out_shape=jax.ShapeDtypeStruct((ne * cap * ROW_TILES, LANES), F32),
        scratch_shapes=[pltpu.VMEM((2, ch * ROW_TILES, LANES), F32),
                        pltpu.VMEM((ch * ROW_TILES, LANES), F32),
                        pltpu.VMEM((ch, D_MODEL), BF16),
                        pltpu.VMEM((ch, D_MODEL), F32),
                        pltpu.SemaphoreType.DMA((3,))],
        compiler_params=_cparams(("arbitrary", "arbitrary", "arbitrary")),
        name="moe",
    )(idx.reshape(ne * nh, 1, ch), idx.reshape(ne * nh, 1, ch), dst.reshape(ne * nh, 1, ch),
      dst.reshape(ne * nh, 1, ch), gate.reshape(ne * cap, 1), w1, w3, w2, h2rows)


def _combine_kernel(pb_ref, pc_ref, pv_ref, y_ref, tokp_ref, cnt_ref, x1_ref, mod_ref, g2_ref, b2_ref,
                    o_ref, acc_ref, start_ref, count_ref, *, alpha):
    k = pl.program_id(0)
    blk = pb_ref[k]
    prev = pb_ref[jnp.maximum(k - 1, 0)]
    nxt = pb_ref[jnp.minimum(k + 1, pl.num_programs(0) - 1)]
    valid = pv_ref[k] == 1
    first = valid & ((k == 0) | (prev != blk))
    last = valid & ((k == pl.num_programs(0) - 1) | (nxt != blk) | (pv_ref[jnp.minimum(k + 1, pl.num_programs(0) - 1)] == 0))

    @pl.when(first)
    def _():
        acc_ref[...] = jnp.zeros_like(acc_ref)
        start_ref[...] = tokp_ref[0].astype(F32).T[:, 0:1].astype(I32)
        count_ref[...] = cnt_ref[0].astype(F32).T[:, 0:1].astype(I32)

    @pl.when(valid)
    def _():
        parts = [y_ref[pl.ds(kk, COMBINE_ROWS, stride=ROW_TILES), :] for kk in range(ROW_TILES)]
        rows = jnp.concatenate(parts, axis=1).astype(BF16)
        r = pc_ref[k] * COMBINE_ROWS + lax.broadcasted_iota(I32, (1, COMBINE_ROWS), 1)
        start = start_ref[...]
        own = jnp.where((r >= start) & (r < start + count_ref[...]), 1.0, 0.0).astype(BF16)
        acc_ref[...] += _dot(own, rows)

    @pl.when(last)
    def _():
        gate2 = mod_ref[0, 5:6, :]
        o_ref[...] = _ln(alpha * x1_ref[...] + gate2 * acc_ref[...]) * g2_ref[...] + b2_ref[...]


def _combine(pb, pc, pv, yrows, tokp_col, cnt_col, x1, mod3, wp, seq, alpha):
    t = x1.shape[0]
    tm = TM
    spb = seq // tm
    npairs = pb.shape[0]
    full = lambda a: pl.BlockSpec(a.shape, lambda k, pb, pc, pv: (0,) * a.ndim)
    row = pl.BlockSpec((tm, D_MODEL), lambda k, pb, pc, pv: (pb[k], 0))
    colspec = pl.BlockSpec((1, N_EXPERTS, tm), lambda k, pb, pc, pv: (pb[k], 0, 0))
    return pl.pallas_call(
        functools.partial(_combine_kernel, alpha=alpha),
        grid_spec=pltpu.PrefetchScalarGridSpec(
            num_scalar_prefetch=3,
            grid=(npairs,),
            in_specs=[pl.BlockSpec((COMBINE_ROWS * ROW_TILES, LANES), lambda k, pb, pc, pv: (pc[k], 0)),
                      colspec, colspec, row,
                      pl.BlockSpec((1, 6, D_MODEL), lambda k, pb, pc, pv: (pb[k] // spb, 0, 0)),
                      full(wp["ln2_g"]), full(wp["ln2_b"])],
            out_specs=row,
            scratch_shapes=[pltpu.VMEM((tm, D_MODEL), F32), pltpu.VMEM((tm, 1), I32), pltpu.VMEM((tm, 1), I32)]),
        out_shape=jax.ShapeDtypeStruct((t, D_MODEL), F32),
        compiler_params=_cparams(("arbitrary",)),
        name="combine",
    )(pb, pc, pv, yrows, tokp_col, cnt_col, x1, mod3, wp["ln2_g"], wp["ln2_b"])


def _pair_schedule(row_start, n_rows_total, nb):
    nchunks = n_rows_total // COMBINE_ROWS
    npairs = nb + nchunks
    rs = row_start
    re = jnp.concatenate([rs[1:], jnp.array([n_rows_total], I32)])
    c_lo = jnp.minimum(rs // COMBINE_ROWS, nchunks - 1)
    c_hi = jnp.maximum(c_lo, (re - 1) // COMBINE_ROWS)
    n_b = c_hi - c_lo + 1
    ends = jnp.cumsum(n_b)
    starts = ends - n_b
    k = jnp.arange(npairs, dtype=I32)
    valid = k < ends[-1]
    kk = jnp.minimum(k, ends[-1] - 1)
    b = jnp.sum(ends[None, :] <= kk[:, None], axis=1).astype(I32)
    c = c_lo[b] + (kk - starts[b])
    return b.astype(I32), c.astype(I32), valid.astype(I32)


def _pack_weights(l, seq, w_in, b_in, ln_v_g, ln_v_b, w_spatial, b_spatial, w_oa, q_norm_g, w_uq, kv_norm_g,
                  w_ukv, w_ob, w_out, ln1_g, ln1_b, w_router, ln2_g, ln2_b):
    half = QK_ROPE // 2
    z32 = lambda rows: jnp.zeros((rows, half), F32)

    def rope_cols(w):
        return jnp.concatenate([w[:, :half], z32(w.shape[0]), w[:, half:], z32(w.shape[0])], axis=1)

    wi, bi = w_in[l], b_in[l][None, :]
    off_cq, off_ckv, off_kr = 2 * GM_WIDTH, 2 * GM_WIDTH + Q_LORA, 2 * GM_WIDTH + Q_LORA + KV_LORA
    off_ga = off_kr + QK_ROPE

    def repack(a):
        return jnp.concatenate([a[:, :off_kr], rope_cols(a[:, off_kr:off_ga]), a[:, off_ga:]], axis=1)

    wq = w_uq[l].reshape(Q_LORA, N_HEADS, QK_NOPE + QK_ROPE)
    wq = jnp.concatenate([wq[:, :, :QK_NOPE],
                          wq[:, :, QK_NOPE:QK_NOPE + half], jnp.zeros((Q_LORA, N_HEADS, half), F32),
                          wq[:, :, QK_NOPE + half:], jnp.zeros((Q_LORA, N_HEADS, half), F32)], axis=2)
    wkv = w_ukv[l].reshape(KV_LORA, N_HEADS, QK_NOPE + V_DIM)
    inv = ROPE_BASE ** (-jnp.arange(half, dtype=F32) / half)
    ang = jnp.arange(seq, dtype=F32)[:, None] * inv[None, :]
    cos, sin, zs = jnp.cos(ang), jnp.sin(ang), jnp.zeros((seq, half), F32)
    wr_hi = w_router[l].astype(BF16)
    wr_lo = (w_router[l] - wr_hi.astype(F32)).astype(BF16)
    wr = (jnp.zeros((D_MODEL, 2 * LANES), BF16).at[:, :N_EXPERTS].set(wr_hi)
          .at[:, LANES:LANES + N_EXPERTS].set(wr_lo))
    return {
        "w_in": repack(wi).astype(BF16), "b_in": repack(bi),
        "ln_v_g": ln_v_g[l][None, :], "ln_v_b": ln_v_b[l][None, :],
        "w_s": w_spatial[l].astype(BF16),
        "b_s": jnp.broadcast_to(b_spatial[l][:, :, None], (GM_GROUPS, CHUNK, CHUNK)),
        "w_oa": w_oa[l].astype(BF16),
        "q_norm_g": q_norm_g[l][None, :], "w_uq": wq.reshape(Q_LORA, N_HEADS * HEAD_W).T.astype(BF16),
        "kv_norm_g": kv_norm_g[l][None, :],
        "w_uk": wkv[:, :, :QK_NOPE].reshape(KV_LORA, N_HEADS * QK_NOPE).astype(BF16),
        "w_uvt": wkv[:, :, QK_NOPE:].reshape(KV_LORA, N_HEADS * V_DIM).T.astype(BF16),
        "cos": jnp.concatenate([cos, zs, cos, zs], axis=1), "sin": jnp.concatenate([-sin, zs, sin, zs], axis=1),
        "w_ob": w_ob[l].astype(BF16), "w_out": w_out[l].astype(BF16),
        "ln1_g": ln1_g[l][None, :], "ln1_b": ln1_b[l][None, :], "w_router": wr,
        "ln2_g": ln2_g[l][None, :], "ln2_b": ln2_b[l][None, :],
    }


def _layer(x, c, l, alpha, w_ada, b_ada, w1, w3, w2, packed):
    bsz, seq, _ = x.shape
    t = bsz * seq
    cap = EC_CAPACITY_FACTOR * t // N_EXPERTS
    x2 = x.reshape(t, D_MODEL)
    mod3 = _mod(c, w_ada[l], b_ada[l])
    u, vn, q, k, vt, sga, sgb = _inproj(x2, mod3, packed, bsz, seq)
    ap = _spatial(u, vn, sga, packed)
    o = _attention(q, k, vt).reshape(t, D_MODEL)
    x1, h2, aff3 = _post(x2, mod3, o, ap, sgb, packed, bsz, seq, alpha)
    nb = t // TM
    spos, flat, cum, tokp, cnt = _route(aff3, cap)
    idx, gate, dst = _slots(cum[:, :, 0], spos, aff3, flat, cap)
    yrows = _moe(idx, dst, gate, h2, w1[l], w3[l], w2[l], cap)
    pb, pc, pv = _pair_schedule(tokp[:, 0, 0], N_EXPERTS * cap, nb)
    y = _combine(pb, pc, pv, yrows, tokp, cnt, x1, mod3, packed, seq, alpha)
    return y.reshape(bsz, seq, D_MODEL)


def kernel(x_prompt, x_sample, c_prompt, c_sample, w_ada, b_ada, w_in, b_in, ln_v_g, ln_v_b, w_spatial, b_spatial,
           w_oa, q_norm_g, w_uq, kv_norm_g, w_ukv, w_ob, w_out, ln1_g, ln1_b, w_router, w1, w3, w2, ln2_g, ln2_b):
    depth = w_ada.shape[0]
    alpha = (2.0 * depth) ** 0.25
    w1, w3, w2 = w1.astype(BF16), w3.astype(BF16), w2.astype(BF16)
    outs = []
    for x, c in ((x_prompt, c_prompt), (x_sample, c_sample)):
        for l in range(depth):
            packed = _pack_weights(l, x.shape[1], w_in, b_in, ln_v_g, ln_v_b, w_spatial, b_spatial, w_oa,
                                   q_norm_g, w_uq, kv_norm_g, w_ukv, w_ob, w_out, ln1_g, ln1_b, w_router,
                                   ln2_g, ln2_b)
            x = _layer(x, c, l, alpha, w_ada, b_ada, w1, w3, w2, packed)
        outs.append(x)
    return tuple(outs)
```
